```python
import math
import jax, jax.numpy as jnp
from jax import lax
import numpy as np

D_MODEL = 1024
BATCH = 16
SEQ = 2048
DEPTH = 1
DEC_BATCH = 32
DEC_SEQ = 1
PAST_LEN = 16384
PAGE_SIZE = 128

HEAD_DIM = 64
N_HEADS_A = 8
N_KV_A = 2
GROUP_A = N_HEADS_A // N_KV_A
N_IDX_HEADS = 8
IDX_DIM = 64
N_HEADS_B = 8
WIDTH_A = N_HEADS_A * HEAD_DIM
WIDTH_B = N_HEADS_B * HEAD_DIM
TOPK_MAX = 256
N_BUCKETS = 32
MAX_DISTANCE = 128
N_GROUPS = 4
EXPERTS_PER_GROUP = 8
TOP_E = 2
D_EXPERT = 256
Q_BLOCK = 128
EPS = 1e-6
SPLIT_WIDTHS = (WIDTH_A, N_KV_A * HEAD_DIM, N_KV_A * HEAD_DIM, N_IDX_HEADS * IDX_DIM, N_IDX_HEADS, IDX_DIM,
                WIDTH_B, WIDTH_B, WIDTH_B, D_MODEL, D_MODEL)
D_IN = sum(SPLIT_WIDTHS)

kernel_name = "hybrid_dsa_stickbreak_hmoe_step"


def _rmsnorm(x, g):
    xf = x.astype(jnp.float32)
    y = xf * lax.rsqrt(jnp.mean(xf * xf, axis=-1, keepdims=True) + EPS)
    return (y * g.astype(jnp.float32)).astype(x.dtype)


def _t5_bucket(dist):
    n = jnp.maximum(dist, 0)
    max_exact = N_BUCKETS // 2
    nf = jnp.maximum(n, max_exact).astype(jnp.float32)
    large = max_exact + (jnp.log(nf / max_exact) / math.log(MAX_DISTANCE / max_exact)
                         * (N_BUCKETS - max_exact)).astype(jnp.int32)
    return jnp.where(n < max_exact, n, jnp.minimum(large, N_BUCKETS - 1))


def _take_rows(rows, idx):
    return jax.vmap(lambda r, i: r[i])(rows, idx)


def _project(xn, w_in):
    B, T, _ = xn.shape
    points = [int(p) for p in np.cumsum(SPLIT_WIDTHS)[:-1]]
    qa, ka, va, qi, wi, ki, qb, kb, vb, ga, gb = jnp.split(xn @ w_in, points, axis=-1)
    return (qa.reshape(B, T, N_HEADS_A, HEAD_DIM), ka.reshape(B, T, N_KV_A, HEAD_DIM),
            va.reshape(B, T, N_KV_A, HEAD_DIM), qi.reshape(B, T, N_IDX_HEADS, IDX_DIM),
            wi * N_IDX_HEADS ** -0.5, ki,
            qb.reshape(B, T, N_HEADS_B, HEAD_DIM), kb.reshape(B, T, N_HEADS_B, HEAD_DIM),
            vb.reshape(B, T, N_HEADS_B, HEAD_DIM), jax.nn.sigmoid(ga), jax.nn.sigmoid(gb))


def _dsa_attend(q, q_i, w_i, k_i, fetch, q_pos, bias_table, n_top):
    B, Q = q.shape[:2]
    L = k_i.shape[1]
    rel = jax.nn.relu(jnp.einsum('bqhd,bsd->bqhs', q_i, k_i) * IDX_DIM ** -0.5)
    score = jnp.einsum('bqhs,bqh->bqs', rel, w_i).astype(jnp.float32)
    causal = jnp.arange(L, dtype=jnp.int32)[None, :] <= q_pos[:, None]
    score = jnp.where(causal[None], score, -jnp.inf)
    _, sel = lax.top_k(score, n_top)
    k_sel, v_sel = fetch(sel)
    dist = q_pos[None, :, None] - sel
    bias = bias_table[_t5_bucket(dist)].astype(jnp.float32)
    bias = bias.reshape(B, Q, n_top, N_KV_A, GROUP_A).transpose(0, 1, 3, 4, 2)
    qg = q.reshape(B, Q, N_KV_A, GROUP_A, HEAD_DIM)
    logits = jnp.einsum('bqcgd,bqncd->bqcgn', qg, k_sel).astype(jnp.float32) * HEAD_DIM ** -0.5 + bias
    logits = jnp.where((dist >= 0)[:, :, None, None, :], logits, -jnp.inf)
    p = jax.nn.softmax(logits, axis=-1).astype(v_sel.dtype)
    o = jnp.einsum('bqcgn,bqncd->bqcgd', p, v_sel)
    return o.reshape(B, Q, WIDTH_A)


def _stick_breaking(q, k, v, q_pos):
    B, Q = q.shape[:2]
    L = k.shape[1]
    z = jnp.einsum('bqhd,bshd->bhqs', q, k).astype(jnp.float32) * HEAD_DIM ** -0.5
    strict = (jnp.arange(L, dtype=jnp.int32)[None, :] < q_pos[:, None])[None, None]
    log_keep = jnp.where(strict, jax.nn.log_sigmoid(-z), 0.0)
    after = lax.cumsum(log_keep, axis=3, reverse=True) - log_keep
    a = jnp.where(strict, jnp.exp(jax.nn.log_sigmoid(z) + after), 0.0)
    o = jnp.einsum('bhqs,bshd->bqhd', a.astype(v.dtype), v)
    return o.reshape(B, Q, WIDTH_B)


def _hier_moe(xn, w_rg, b_rg, w_re, b_re, w_gate, w_up, w_down):
    shp = xn.shape
    t = xn.reshape(-1, D_MODEL)
    p_group = jax.nn.softmax((t @ w_rg).astype(jnp.float32) + b_rg.astype(jnp.float32), axis=-1)
    g_w, g_idx = lax.top_k(p_group, 1)
    logit_e = jnp.einsum('nd,gde->nge', t, w_re).astype(jnp.float32) + b_re.astype(jnp.float32)
    logit_e = jnp.take_along_axis(logit_e, g_idx[:, :, None], axis=1)[:, 0]
    e_w, e_idx = lax.top_k(jax.nn.softmax(logit_e, axis=-1), TOP_E)
    e_w = e_w / jnp.sum(e_w, axis=-1, keepdims=True) * g_w
    within = jnp.sum(jax.nn.one_hot(e_idx, EXPERTS_PER_GROUP, dtype=jnp.float32) * e_w[..., None], axis=1)
    combine = (jax.nn.one_hot(g_idx[:, 0], N_GROUPS, dtype=jnp.float32)[:, :, None]
               * within[:, None, :]).astype(xn.dtype)
    y = jnp.zeros_like(t)
    for g in range(N_GROUPS):
        hid = jax.nn.silu(jnp.einsum('nd,edf->nef', t, w_gate[g])) * jnp.einsum('nd,edf->nef', t, w_up[g])
        y = y + jnp.einsum('nef,efd->nd', hid * combine[:, g, :, None], w_down[g])
    return y.reshape(shp)


def _merge_and_ffn(x, o_a, o_b, g_a, g_b, w_proj_a, w_proj_b, w_out, ln2_g, w_rg, b_rg, w_re, b_re,
                   w_gate, w_up, w_down):
    mixed = g_a * (o_a @ w_proj_a) + g_b * (o_b @ w_proj_b)
    h = x + mixed @ w_out
    return h + _hier_moe(_rmsnorm(h, ln2_g), w_rg, b_rg, w_re, b_re, w_gate, w_up, w_down)


def _prompt_mixers(q_a, k_a, v_a, q_i, w_i, k_i, q_b, k_b, v_b, bias_table):
    B, S = q_a.shape[:2]
    nb = S // Q_BLOCK
    n_top = min(TOPK_MAX, S // 4)

    def fetch(sel):
        return _take_rows(k_a, sel), _take_rows(v_a, sel)

    def blocks(a):
        return a.reshape((B, nb, Q_BLOCK) + a.shape[2:]).swapaxes(0, 1)

    def body(args):
        i, qa, qi, wi, qb = args
        pos = i * Q_BLOCK + jnp.arange(Q_BLOCK, dtype=jnp.int32)
        return (_dsa_attend(qa, qi, wi, k_i, fetch, pos, bias_table, n_top),
                _stick_breaking(qb, k_b, v_b, pos))

    o_a, o_b = lax.map(body, (jnp.arange(nb, dtype=jnp.int32), blocks(q_a), blocks(q_i), blocks(w_i), blocks(q_b)))
    return o_a.swapaxes(0, 1).reshape(B, S, WIDTH_A), o_b.swapaxes(0, 1).reshape(B, S, WIDTH_B)


def _sample_mixers(q_a, k_a, v_a, q_i, w_i, k_i, q_b, k_b, v_b,
                   cache_k_a, cache_v_a, cache_kidx_a, cache_k_b, cache_v_b, page_table, bias_table):
    Bd, T = q_a.shape[:2]
    n_top = min(TOPK_MAX, (PAST_LEN + T) // 4)
    pos = PAST_LEN + jnp.arange(T, dtype=jnp.int32)

    def gather_all(cache):
        return cache[page_table].reshape((Bd, PAST_LEN) + cache.shape[2:])

    k_i_all = jnp.concatenate([gather_all(cache_kidx_a), k_i], axis=1)

    def fetch(sel):
        s_past = jnp.minimum(sel, PAST_LEN - 1)
        phys = jax.vmap(lambda pt, p: pt[p])(page_table, s_past // PAGE_SIZE)
        off = s_past % PAGE_SIZE
        s_new = jnp.clip(sel - PAST_LEN, 0, T - 1)
        is_new = (sel >= PAST_LEN)[..., None, None]
        k_sel = jnp.where(is_new, _take_rows(k_a, s_new), cache_k_a[phys, off])
        v_sel = jnp.where(is_new, _take_rows(v_a, s_new), cache_v_a[phys, off])
        return k_sel, v_sel

    o_a = _dsa_attend(q_a, q_i, w_i, k_i_all, fetch, pos, bias_table, n_top)
    k_b_all = jnp.concatenate([gather_all(cache_k_b), k_b], axis=1)
    v_b_all = jnp.concatenate([gather_all(cache_v_b), v_b], axis=1)
    o_b = _stick_breaking(q_b, k_b_all, v_b_all, pos)
    return o_a, o_b


def setup_inputs(seed: int = 0) -> dict:
    key = jax.random.key(seed)
    ks = jax.random.split(key, 24)
    n_pages = PAST_LEN // PAGE_SIZE
    n_used = DEC_BATCH * n_pages
    n_pool = n_used + n_used // 4

    def nrm(k, shape, scale=1.0):
        return jax.random.normal(k, shape, jnp.float32) * scale

    page_table = jax.random.permutation(ks[7], n_pool)[:n_used].reshape(DEC_BATCH, n_pages).astype(jnp.int32)
    return {
        'x_prompt': nrm(ks[0], (BATCH, SEQ, D_MODEL)),
        'x_sample': nrm(ks[1], (DEC_BATCH, DEC_SEQ, D_MODEL)),
        'cache_k_a': nrm(ks[2], (DEPTH, n_pool, PAGE_SIZE, N_KV_A, HEAD_DIM)),
        'cache_v_a': nrm(ks[3], (DEPTH, n_pool, PAGE_SIZE, N_KV_A, HEAD_DIM)),
        'cache_kidx_a': nrm(ks[4], (DEPTH, n_pool, PAGE_SIZE, IDX_DIM)),
        'cache_k_b': nrm(ks[5], (DEPTH, n_pool, PAGE_SIZE, N_HEADS_B, HEAD_DIM)),
        'cache_v_b': nrm(ks[6], (DEPTH, n_pool, PAGE_SIZE, N_HEADS_B, HEAD_DIM)),
        'page_table': page_table,
        'rel_bias_table': nrm(ks[8], (N_BUCKETS, N_HEADS_A), 0.5),
        'ln1_g': 1.0 + nrm(ks[9], (DEPTH, D_MODEL), 0.02),
        'w_in': nrm(ks[10], (DEPTH, D_MODEL, D_IN), D_MODEL ** -0.5),
        'w_proj_a': nrm(ks[11], (DEPTH, WIDTH_A, D_MODEL), WIDTH_A ** -0.5),
        'w_proj_b': nrm(ks[12], (DEPTH, WIDTH_B, D_MODEL), WIDTH_B ** -0.5),
        'w_out': nrm(ks[13], (DEPTH, D_MODEL, D_MODEL), D_MODEL ** -0.5),
        'ln2_g': 1.0 + nrm(ks[14], (DEPTH, D_MODEL), 0.02),
        'w_router_group': nrm(ks[15], (DEPTH, D_MODEL, N_GROUPS), D_MODEL ** -0.5),
        'b_router_group': nrm(ks[16], (DEPTH, N_GROUPS), 0.01),
        'w_router_expert': nrm(ks[17], (DEPTH, N_GROUPS, D_MODEL, EXPERTS_PER_GROUP), D_MODEL ** -0.5),
        'b_router_expert': nrm(ks[18], (DEPTH, N_GROUPS, EXPERTS_PER_GROUP), 0.01),
        'w_gate': nrm(ks[19], (DEPTH, N_GROUPS, EXPERTS_PER_GROUP, D_MODEL, D_EXPERT), D_MODEL ** -0.5),
        'w_up': nrm(ks[20], (DEPTH, N_GROUPS, EXPERTS_PER_GROUP, D_MODEL, D_EXPERT), D_MODEL ** -0.5),
        'w_down': nrm(ks[21], (DEPTH, N_GROUPS, EXPERTS_PER_GROUP, D_EXPERT, D_MODEL), D_EXPERT ** -0.5),
        'ln_f_g': 1.0 + nrm(ks[22], (D_MODEL,), 0.02),
    }


def reference(x_prompt, x_sample, cache_k_a, cache_v_a, cache_kidx_a, cache_k_b, cache_v_b, page_table,
              rel_bias_table, ln1_g, w_in, w_proj_a, w_proj_b, w_out, ln2_g, w_router_group, b_router_group,
              w_router_expert, b_router_expert, w_gate, w_up, w_down, ln_f_g):
    xp, xs = x_prompt, x_sample
    kap, vap, kip, kbp, vbp = [], [], [], [], []
    kas, vas, kis, kbs, vbs = [], [], [], [], []
    for l in range(DEPTH):
        ffn_w = (w_proj_a[l], w_proj_b[l], w_out[l], ln2_g[l], w_router_group[l], b_router_group[l],
                 w_router_expert[l], b_router_expert[l], w_gate[l], w_up[l], w_down[l])
        qa, ka, va, qi, wi, ki, qb, kb, vb, ga, gb = _project(_rmsnorm(xp, ln1_g[l]), w_in[l])
        o_a, o_b = _prompt_mixers(qa, ka, va, qi, wi, ki, qb, kb, vb, rel_bias_table)
        xp = _merge_and_ffn(xp, o_a, o_b, ga, gb, *ffn_w)
        kap.append(ka); vap.append(va); kip.append(ki); kbp.append(kb); vbp.append(vb)
        qa, ka, va, qi, wi, ki, qb, kb, vb, ga, gb = _project(_rmsnorm(xs, ln1_g[l]), w_in[l])
        o_a, o_b = _sample_mixers(qa, ka, va, qi, wi, ki, qb, kb, vb, cache_k_a[l], cache_v_a[l],
                                  cache_kidx_a[l], cache_k_b[l], cache_v_b[l], page_table, rel_bias_table)
        xs = _merge_and_ffn(xs, o_a, o_b, ga, gb, *ffn_w)
        kas.append(ka); vas.append(va); kis.append(ki); kbs.append(kb); vbs.append(vb)
    y_prompt = _rmsnorm(xp, ln_f_g)
    y_sample = _rmsnorm(xs, ln_f_g)
    return (y_prompt, y_sample,
            jnp.stack(kap), jnp.stack(vap), jnp.stack(kip), jnp.stack(kbp), jnp.stack(vbp),
            jnp.stack(kas), jnp.stack(vas), jnp.stack(kis), jnp.stack(kbs), jnp.stack(vbs))
```

```python
import functools
import math

import jax
import jax.numpy as jnp
import numpy as np
from jax import lax
from jax.experimental import pallas as pl
from jax.experimental.pallas import tpu as pltpu

F32 = jnp.float32
BF16 = jnp.bfloat16
I32 = jnp.int32

HEAD_DIM = 64
N_HEADS_A = 8
N_KV_A = 2
GROUP_A = N_HEADS_A // N_KV_A
N_IDX_HEADS = 8
IDX_DIM = 64
N_HEADS_B = 8
WIDTH_A = N_HEADS_A * HEAD_DIM
WIDTH_B = N_HEADS_B * HEAD_DIM
TOPK_MAX = 256
N_BUCKETS = 32
MAX_DISTANCE = 128
N_GROUPS = 4
EXPERTS_PER_GROUP = 8
N_EXPERTS = N_GROUPS * EXPERTS_PER_GROUP
TOP_E = 2
Q_BLOCK = 128
EPS = 1e-6

LANES = 128
VMEM_LIMIT_BYTES = 56 * 1024 * 1024

PAGES_PER_STEP = 8
NEG = -1e30
INT_MIN = -(2 ** 31)
ROUTER_LANES = LANES

_NT = (((1,), (1,)), ((), ()))


def _cparams(n_axes):
    return pltpu.CompilerParams(dimension_semantics=("arbitrary",) * n_axes,
                                vmem_limit_bytes=VMEM_LIMIT_BYTES)


def _const_spec(shape):
    nd = len(shape)
    return pl.BlockSpec(shape, lambda *_: (0,) * nd)


def _div_pow2(x, n):
    assert n & (n - 1) == 0
    return x >> (n.bit_length() - 1)


def _sigmoid(x):
    return 1.0 / (1.0 + jnp.exp(-x))


def _half_masks(dtype):
    lane = lax.broadcasted_iota(I32, (LANES, LANES), 1)
    lo = jnp.where(lane < HEAD_DIM, 1.0, 0.0).astype(dtype)
    hi = jnp.where(lane < HEAD_DIM, 0.0, 1.0).astype(dtype)
    return lo, hi


def _pair_block_diag(blk, lo, hi):
    return jnp.concatenate([blk * lo, blk * hi], axis=0)


def _order_key(score):
    score = jnp.where(score == 0.0, 0.0, score)
    b = lax.bitcast_convert_type(score, I32)
    return b ^ ((b >> 31) & jnp.int32(0x7FFFFFFF))


def _t5_bucket(dist):
    n = jnp.maximum(dist, 0)
    max_exact = N_BUCKETS // 2
    nf = jnp.maximum(n, max_exact).astype(F32)
    large = max_exact + (jnp.log(nf / max_exact) / math.log(MAX_DISTANCE / max_exact)
                         * (N_BUCKETS - max_exact)).astype(I32)
    return jnp.where(n < max_exact, n, jnp.minimum(large, N_BUCKETS - 1))


def _proj_kernel(x_ref, g_ref, wqa, wkv, wkvd, wqi, wkiw, wqb, wkb, wvb, wga, wgb,
                 qa_o, ka_o, va_o, kad_o, vad_o, qi_o, ki_o, kid_o, wi_o,
                 qb_o, kb_o, vb_o, kb16_o, vb16_o, ga_o, gb_o):
    x = x_ref[...]
    ms = jnp.mean(x * x, axis=-1, keepdims=True)
    xn = (x * lax.rsqrt(ms + EPS) * g_ref[...]).astype(BF16)

    def mm(w):
        return jnp.dot(xn, w[...], preferred_element_type=F32)

    qa_o[...] = mm(wqa).astype(BF16)
    kv = mm(wkv)
    ka_o[...] = kv[:, :N_KV_A * HEAD_DIM]
    va_o[...] = kv[:, N_KV_A * HEAD_DIM:]
    kvd = mm(wkvd).astype(BF16)
    kad_o[...] = kvd[:, :2 * LANES]
    vad_o[...] = kvd[:, 2 * LANES:]
    qi_o[...] = mm(wqi).astype(BF16)
    kiw = mm(wkiw)
    ki_o[...] = kiw[:, :IDX_DIM]
    kid_o[...] = kiw[:, :LANES].astype(BF16)
    wi_o[...] = kiw[:, LANES:LANES + N_IDX_HEADS] * (N_IDX_HEADS ** -0.5)
    qb_o[...] = mm(wqb).astype(BF16)
    kb = mm(wkb)
    kb_o[...] = kb
    kb16_o[...] = kb.astype(BF16)
    vb = mm(wvb)
    vb_o[...] = vb
    vb16_o[...] = vb.astype(BF16)
    ga_o[...] = _sigmoid(mm(wga))
    gb_o[...] = _sigmoid(mm(wgb))


def _split_w_in(w_in):
    d_model = w_in.shape[0]
    widths = (WIDTH_A, N_KV_A * HEAD_DIM, N_KV_A * HEAD_DIM, N_IDX_HEADS * IDX_DIM, N_IDX_HEADS, IDX_DIM,
              WIDTH_B, WIDTH_B, WIDTH_B, d_model, d_model)
    points = [int(p) for p in np.cumsum(widths)[:-1]]
    qa, ka, va, qi, wi, ki, qb, kb, vb, ga, gb = jnp.split(w_in, points, axis=1)
    dup = lambda a: jnp.concatenate([a[:, :HEAD_DIM], a[:, :HEAD_DIM], a[:, HEAD_DIM:], a[:, HEAD_DIM:]], axis=1)
    wkv = jnp.concatenate([ka, va], axis=1)
    wkvd = jnp.concatenate([dup(ka), dup(va)], axis=1)
    wkiw = jnp.concatenate([ki, ki, wi, jnp.zeros((d_model, LANES - N_IDX_HEADS), w_in.dtype)], axis=1)
    return tuple(a.astype(BF16) for a in (qa, wkv, wkvd, qi, wkiw, qb, kb, vb, ga, gb))


def _project(x2d, ln_g, w_pieces, tm):
    n, d = x2d.shape
    assert n % tm == 0
    out_defs = [
        (WIDTH_A, BF16), (N_KV_A * HEAD_DIM, F32), (N_KV_A * HEAD_DIM, F32), (2 * LANES, BF16), (2 * LANES, BF16),
        (N_IDX_HEADS * IDX_DIM, BF16), (IDX_DIM, F32), (LANES, BF16), (N_IDX_HEADS, F32),
        (WIDTH_B, BF16), (WIDTH_B, F32), (WIDTH_B, F32), (WIDTH_B, BF16), (WIDTH_B, BF16), (d, F32), (d, F32)]
    row = lambda i: (i, 0)
    return pl.pallas_call(
        _proj_kernel,
        grid=(n // tm,),
        in_specs=[pl.BlockSpec((tm, d), row), _const_spec((1, d))] + [_const_spec(w.shape) for w in w_pieces],
        out_specs=[pl.BlockSpec((tm, w), row) for w, _ in out_defs],
        out_shape=[jax.ShapeDtypeStruct((n, w), dt) for w, dt in out_defs],
        compiler_params=_cparams(1),
        name="proj",
    )(x2d, ln_g.reshape(1, d), *w_pieces)


def _dsa_prompt_kernel(tab_ref, qi_ref, wi_ref, kid_ref, qa_ref, kad_ref, vad_ref, tri_ref, o_ref,
                       key_scr, madd_scr, bias_scr, *, n_top):
    b = pl.program_id(0)
    i = pl.program_id(1)
    qb = Q_BLOCK
    row = lax.broadcasted_iota(I32, (qb, qb), 0)
    col = lax.broadcasted_iota(I32, (qb, qb), 1)
    lo, hi = _half_masks(BF16)

    @pl.when((b == 0) & (i == 0))
    def _():
        for d in range(3):
            bucket = _t5_bucket(d * qb + row - col)
            for h in range(N_HEADS_A):
                def sel(bk, acc):
                    return jnp.where(bucket == bk, tab_ref[bk, h], acc)
                tile = lax.fori_loop(0, N_BUCKETS, sel, jnp.zeros((qb, qb), F32))
                bias_scr[d, h // 2, :, (h % 2) * qb:(h % 2 + 1) * qb] = tile

    w = wi_ref[0] * (IDX_DIM ** -0.5)
    wcols = [w[:, h:h + 1] for h in range(N_IDX_HEADS)]

    def score_body(j, carry):
        kblk = kid_ref[0, pl.ds(pl.multiple_of(j * qb, qb), qb), :]
        kbd = _pair_block_diag(kblk, lo, hi)
        acc = jnp.zeros((qb, qb), F32)
        for p in range(N_IDX_HEADS // 2):
            r = lax.dot_general(qi_ref[0, :, p * LANES:(p + 1) * LANES], kbd, _NT,
                                preferred_element_type=F32)
            acc = acc + jnp.maximum(r[:, :qb], 0.0) * wcols[2 * p] + jnp.maximum(r[:, qb:], 0.0) * wcols[2 * p + 1]
        causal = (i - j) * qb + row - col >= 0
        key_scr[j] = _order_key(jnp.where(causal, acc, -jnp.inf))
        return carry

    lax.fori_loop(0, i + 1, score_body, 0)

    def count(pred):
        def body(j, acc):
            return acc + jnp.where(pred(key_scr[j]), 1, 0)
        acc = lax.fori_loop(0, i + 1, body, jnp.zeros((qb, qb), I32))
        return jnp.sum(acc, axis=1, keepdims=True)

    thr = jnp.where(count(lambda k: k >= 0) >= n_top, 0, INT_MIN).astype(I32)

    def bit_body(t, thr):
        cand = thr | lax.shift_left(jnp.int32(1), (30 - t).astype(I32))
        return jnp.where(count(lambda k: k >= cand) >= n_top, cand, thr)

    thr = lax.fori_loop(0, 31, bit_body, thr)
    need = (n_top - count(lambda k: k > thr)).astype(F32)

    tri = tri_ref[...]

    def mask_body(j, ties_before):
        key = key_scr[j]
        eq = key == thr
        pre = jnp.dot(jnp.where(eq, 1.0, 0.0).astype(BF16), tri, preferred_element_type=F32)
        take = (key > thr) | (eq & (ties_before + pre <= need))
        causal = (i - j) * qb + row - col >= 0
        madd_scr[j] = jnp.where(take & causal, 0.0, NEG)
        return ties_before + pre[:, qb - 1:qb]

    lax.fori_loop(0, i + 1, mask_body, jnp.zeros((qb, 1), F32))

    for p in range(N_HEADS_A // 2):
        c = (2 * p) // GROUP_A
        q_p = (qa_ref[0, :, p * LANES:(p + 1) * LANES].astype(F32) * (HEAD_DIM ** -0.5)).astype(BF16)

        def att_body(j, carry):
            m0, l0, m1, l1, acc = carry
            rows = pl.ds(pl.multiple_of(j * qb, qb), qb)
            kbd = _pair_block_diag(kad_ref[0, rows, c * LANES:(c + 1) * LANES], lo, hi)
            vbd = _pair_block_diag(vad_ref[0, rows, c * LANES:(c + 1) * LANES], lo, hi)
            lg = lax.dot_general(q_p, kbd, _NT, preferred_element_type=F32)
            lg = lg + bias_scr[jnp.minimum(i - j, 2), p]
            madd = madd_scr[j]
            lg0 = lg[:, :qb] + madd
            lg1 = lg[:, qb:] + madd
            n0 = jnp.maximum(m0, jnp.max(lg0, axis=1, keepdims=True))
            n1 = jnp.maximum(m1, jnp.max(lg1, axis=1, keepdims=True))
            p0 = jnp.exp(lg0 - n0)
            p1 = jnp.exp(lg1 - n1)
            a0 = jnp.exp(m0 - n0)
            a1 = jnp.exp(m1 - n1)
            l0 = a0 * l0 + jnp.sum(p0, axis=1, keepdims=True)
            l1 = a1 * l1 + jnp.sum(p1, axis=1, keepdims=True)
            pv = jnp.dot(jnp.concatenate([p0, p1], axis=1).astype(BF16), vbd, preferred_element_type=F32)
            acc = acc * jnp.where(col < HEAD_DIM, a0, a1) + pv
            return n0, l0, n1, l1, acc

        neg = jnp.full((qb, 1), NEG, F32)
        zero = jnp.zeros((qb, 1), F32)
        _, l0, _, l1, acc = lax.fori_loop(0, i + 1, att_body, (neg, zero, neg, zero, jnp.zeros((qb, qb), F32)))
        o_ref[0, :, p * LANES:(p + 1) * LANES] = (acc / jnp.where(col < HEAD_DIM, l0, l1)).astype(o_ref.dtype)


def _dsa_prompt(bias_table, qi, wi, kid, qa, kad, vad, n_top):
    bsz, seq, _ = qi.shape
    nq = seq // Q_BLOCK
    tri = (np.arange(Q_BLOCK)[:, None] <= np.arange(Q_BLOCK)[None, :]).astype(np.float32)
    qblk = lambda w: pl.BlockSpec((1, Q_BLOCK, w), lambda b, i: (b, i, 0))
    full = lambda w: pl.BlockSpec((1, seq, w), lambda b, i: (b, 0, 0))
    return pl.pallas_call(
        functools.partial(_dsa_prompt_kernel, n_top=n_top),
        grid=(bsz, nq),
        in_specs=[pl.BlockSpec(memory_space=pltpu.SMEM), qblk(N_IDX_HEADS * IDX_DIM), qblk(N_IDX_HEADS), full(LANES),
                  qblk(WIDTH_A), full(2 * LANES), full(2 * LANES), _const_spec((Q_BLOCK, Q_BLOCK))],
        out_specs=qblk(WIDTH_A),
        out_shape=jax.ShapeDtypeStruct((bsz, seq, WIDTH_A), BF16),
        scratch_shapes=[pltpu.VMEM((nq, Q_BLOCK, Q_BLOCK), I32), pltpu.VMEM((nq, Q_BLOCK, Q_BLOCK), F32),
                        pltpu.VMEM((3, N_HEADS_A // 2, Q_BLOCK, 2 * Q_BLOCK), F32)],
        compiler_params=_cparams(2),
        name="dsa_prompt",
    )(bias_table, qi, wi, kid, qa, kad, vad, jnp.asarray(tri, BF16))


def _suffix_matrix(n_pairs):
    n = n_pairs * Q_BLOCK
    r = np.arange(n)
    same = (r[:, None] // Q_BLOCK) == (r[None, :] // Q_BLOCK)
    u = same & (r[:, None] > r[None, :])
    w = np.concatenate([u, same], axis=1).astype(np.float32)
    return jnp.asarray(np.concatenate([w, w], axis=0), BF16)


def _log_keep(z):
    return -(jnp.maximum(z, 0.0) + jnp.log1p(jnp.exp(-jnp.abs(z))))


def _split_hi_lo(x):
    hi = x.astype(BF16)
    lo = (x - hi.astype(F32)).astype(BF16)
    return jnp.concatenate([hi, lo], axis=1)


def _sb_prompt_kernel(qb_ref, kb_ref, vb_ref, w_ref, o_ref):
    i = pl.program_id(1)
    qb = Q_BLOCK
    row = lax.broadcasted_iota(I32, (qb, 2 * qb), 0)
    col = lax.broadcasted_iota(I32, (qb, 2 * qb), 1)
    strict = (col & (qb - 1)) < row
    lo, hi = _half_masks(BF16)
    wmat = w_ref[...]

    for p in range(N_HEADS_B // 2):
        q_p = (qb_ref[0, :, p * LANES:(p + 1) * LANES].astype(F32) * (HEAD_DIM ** -0.5)).astype(BF16)

        def block(j, run, acc, diagonal):
            rows = pl.ds(pl.multiple_of(j * qb, qb), qb)
            kbd = _pair_block_diag(kb_ref[0, rows, p * LANES:(p + 1) * LANES], lo, hi)
            vbd = _pair_block_diag(vb_ref[0, rows, p * LANES:(p + 1) * LANES], lo, hi)
            z = lax.dot_general(q_p, kbd, _NT, preferred_element_type=F32)
            lk = _log_keep(z)
            if diagonal:
                lk = jnp.where(strict, lk, 0.0)
            res = jnp.dot(_split_hi_lo(lk), wmat, preferred_element_type=F32)
            a = jnp.exp(z + lk + res[:, :2 * qb] + run)
            if diagonal:
                a = jnp.where(strict, a, 0.0)
            acc = acc + jnp.dot(a.astype(BF16), vbd, preferred_element_type=F32)
            return run + res[:, 2 * qb:], acc

        run, acc = block(i, jnp.zeros((qb, 2 * qb), F32), jnp.zeros((qb, qb), F32), True)

        def body(t, carry):
            return block(i - 1 - t, carry[0], carry[1], False)

        _, acc = lax.fori_loop(0, i, body, (run, acc))
        o_ref[0, :, p * LANES:(p + 1) * LANES] = acc.astype(o_ref.dtype)


def _sb_prompt(qb, kb16, vb16):
    bsz, seq, width = qb.shape
    nq = seq // Q_BLOCK
    wmat = _suffix_matrix(2)
    qblk = pl.BlockSpec((1, Q_BLOCK, width), lambda b, i: (b, i, 0))
    full = pl.BlockSpec((1, seq, width), lambda b, i: (b, 0, 0))
    return pl.pallas_call(
        _sb_prompt_kernel,
        grid=(bsz, nq),
        in_specs=[qblk, full, full, _const_spec(wmat.shape)],
        out_specs=qblk,
        out_shape=jax.ShapeDtypeStruct((bsz, seq, width), BF16),
        compiler_params=_cparams(2),
        name="sb_prompt",
    )(qb, kb16, vb16, wmat)


def _page_specs(block_tail, n_pages, reverse):
    def spec(p):
        def index_map(b, g, pt):
            page = g * PAGES_PER_STEP + p
            if reverse:
                page = n_pages - 1 - page
            return (pt[b, page],) + (0,) * len(block_tail)
        return pl.BlockSpec((1,) + block_tail, index_map)
    return [spec(p) for p in range(PAGES_PER_STEP)]


def _idx_sample_kernel(pt_ref, q_ref, w_ref, *refs):
    k_refs, o_ref = refs[:PAGES_PER_STEP], refs[PAGES_PER_STEP]
    q = q_ref[0].astype(BF16)
    w = w_ref[0] * (IDX_DIM ** -0.5)
    for p in range(PAGES_PER_STEP):
        r = lax.dot_general(q, k_refs[p][0].astype(BF16), _NT, preferred_element_type=F32)
        o_ref[0, p] = jnp.sum(jnp.maximum(r, 0.0) * w, axis=0, keepdims=True)


def _idx_sample(page_table, qi3, wi3, cache_kidx):
    bd, n_pages = page_table.shape
    per_b = lambda shape: pl.BlockSpec((1,) + shape, lambda b, g, pt: (b,) + (0,) * len(shape))
    return pl.pallas_call(
        _idx_sample_kernel,
        grid_spec=pltpu.PrefetchScalarGridSpec(
            num_scalar_prefetch=1,
            grid=(bd, n_pages // PAGES_PER_STEP),
            in_specs=[per_b((N_IDX_HEADS, IDX_DIM)), per_b((N_IDX_HEADS, 1))]
            + _page_specs((Q_BLOCK, IDX_DIM), n_pages, False),
            out_specs=pl.BlockSpec((1, PAGES_PER_STEP, 1, LANES), lambda b, g, pt: (b, g, 0, 0)),
        ),
        out_shape=jax.ShapeDtypeStruct((bd, n_pages, 1, LANES), F32),
        compiler_params=_cparams(2),
        name="idx_sample",
    )(page_table, qi3, wi3, *([cache_kidx] * PAGES_PER_STEP))


def _select_sample_kernel(sc_ref, q_ref, w_ref, k_ref, tri_ref, low_ref, madd_o, maddn_o, *, n_top):
    n_pages = sc_ref.shape[1]
    key = _order_key(sc_ref[0])
    r_new = jnp.sum(q_ref[0] * k_ref[0], axis=1, keepdims=True)
    s_new = jnp.sum(jnp.maximum(r_new * (IDX_DIM ** -0.5), 0.0) * w_ref[0], axis=0, keepdims=True)
    key_new = _order_key(s_new)

    def total(x):
        return jnp.sum(jnp.sum(x, axis=1, keepdims=True), axis=0, keepdims=True)

    def count(pred):
        return total(jnp.where(pred(key), 1, 0)) + jnp.where(pred(key_new), 1, 0)

    thr = jnp.where(count(lambda k: k >= 0) >= n_top, 0, INT_MIN).astype(I32)

    def bit_body(t, thr):
        cand = thr | lax.shift_left(jnp.int32(1), (30 - t).astype(I32))
        return jnp.where(count(lambda k: k >= cand) >= n_top, cand, thr)

    thr = lax.fori_loop(0, 31, bit_body, thr)
    need = (n_top - count(lambda k: k > thr)).astype(F32)
    eq = jnp.where(key == thr, 1.0, 0.0).astype(BF16)
    pre = jnp.dot(eq, tri_ref[...], preferred_element_type=F32)
    tot = jnp.broadcast_to(pre[:, LANES - 1:LANES], pre.shape).astype(BF16)
    before = jnp.dot(low_ref[...], tot, preferred_element_type=F32)
    take = (key > thr) | ((key == thr) & (before + pre <= need))
    madd_o[0] = jnp.where(take, 0.0, NEG)
    ties_past = total(jnp.where(key == thr, 1.0, 0.0))
    take_new = (key_new > thr) | ((key_new == thr) & (ties_past + 1.0 <= need))
    maddn_o[0] = jnp.where(take_new, 0.0, NEG)


def _select_sample(scores, qi3, wi3, ki3, n_top):
    bd, n_pages, _ = scores.shape
    tri = jnp.asarray((np.arange(LANES)[:, None] <= np.arange(LANES)[None, :]).astype(np.float32), BF16)
    low = jnp.asarray((np.arange(n_pages)[:, None] > np.arange(n_pages)[None, :]).astype(np.float32), BF16)
    per_b = lambda shape: pl.BlockSpec((1,) + shape, lambda b: (b,) + (0,) * len(shape))
    return pl.pallas_call(
        functools.partial(_select_sample_kernel, n_top=n_top),
        grid=(bd,),
        in_specs=[per_b((n_pages, LANES)), per_b((N_IDX_HEADS, IDX_DIM)), per_b((N_IDX_HEADS, 1)),
                  per_b((1, IDX_DIM)), _const_spec(tri.shape), _const_spec(low.shape)],
        out_specs=[per_b((n_pages, LANES)), per_b((1, 1))],
        out_shape=[jax.ShapeDtypeStruct((bd, n_pages, LANES), F32), jax.ShapeDtypeStruct((bd, 1, 1), F32)],
        compiler_params=_cparams(1),
        name="select_sample",
    )(scores, qi3, wi3, ki3, tri, low)


def _dsa_sample_kernel(pt_ref, tabt_ref, qz_ref, madd_ref, maddn_ref, kn_ref, vn_ref, *refs,
                       past_len):
    k_refs = refs[:PAGES_PER_STEP]
    v_refs = refs[PAGES_PER_STEP:2 * PAGES_PER_STEP]
    o_ref, m_scr, l_scr, acc_scr = refs[2 * PAGES_PER_STEP:]
    g = pl.program_id(1)
    n_steps = pl.num_programs(1)
    qz = qz_ref[0]
    qz16 = qz.astype(BF16)
    lane = lax.broadcasted_iota(I32, (1, LANES), 1)
    tabt = tabt_ref[...]

    def bias_of(dist):
        bucket = _t5_bucket(dist)
        out = jnp.zeros((N_HEADS_A, dist.shape[1]), F32)
        for bk in range(N_BUCKETS):
            out = jnp.where(bucket == bk, tabt[:, bk:bk + 1], out)
        return out

    @pl.when(g == 0)
    def _():
        m_scr[...] = jnp.full(m_scr.shape, NEG, F32)
        l_scr[...] = jnp.zeros(l_scr.shape, F32)
        acc_scr[...] = jnp.zeros(acc_scr.shape, F32)

    def update(lg, v):
        m_old = m_scr[...]
        m_new = jnp.maximum(m_old, jnp.max(lg, axis=1, keepdims=True))
        pr = jnp.exp(lg - m_new)
        alpha = jnp.exp(m_old - m_new)
        l_scr[...] = alpha * l_scr[...] + jnp.sum(pr, axis=1, keepdims=True)
        m_scr[...] = m_new
        return alpha, pr

    for p in range(PAGES_PER_STEP):
        page = g * PAGES_PER_STEP + p
        lg = lax.dot_general(qz16, k_refs[p][0].astype(BF16), _NT, preferred_element_type=F32)
        lg = lg + bias_of(past_len - (page * Q_BLOCK + lane)) + madd_ref[0, p]
        alpha, pr = update(lg, None)
        acc_scr[...] = alpha * acc_scr[...] + jnp.dot(pr.astype(BF16), v_refs[p][0].astype(BF16),
                                                      preferred_element_type=F32)

    @pl.when(g == n_steps - 1)
    def _():
        lg = jnp.sum(qz * kn_ref[0], axis=1, keepdims=True)
        lg = lg + bias_of(jnp.zeros((1, 1), I32)) + maddn_ref[0]
        alpha, pr = update(lg, None)
        acc = alpha * acc_scr[...] + pr * vn_ref[0]
        out = acc / l_scr[...]
        head = lax.broadcasted_iota(I32, (N_HEADS_A, LANES), 0)
        lanes8 = lax.broadcasted_iota(I32, (N_HEADS_A, LANES), 1)
        mine = _div_pow2(lanes8, HEAD_DIM) == _div_pow2(head, GROUP_A)
        out = jnp.where(mine, out, 0.0)
        o_ref[0] = out[:, :HEAD_DIM] + out[:, HEAD_DIM:]


def _dsa_sample(page_table, bias_table, qz, madd, maddn, k_new, v_new, cache_k, cache_v, past_len):
    bd, n_pages = page_table.shape
    per_b = lambda shape: pl.BlockSpec((1,) + shape, lambda b, g, pt: (b,) + (0,) * len(shape))
    kv_specs = _page_specs((Q_BLOCK, N_KV_A * HEAD_DIM), n_pages, False)
    return pl.pallas_call(
        functools.partial(_dsa_sample_kernel, past_len=past_len),
        grid_spec=pltpu.PrefetchScalarGridSpec(
            num_scalar_prefetch=1,
            grid=(bd, n_pages // PAGES_PER_STEP),
            in_specs=[pl.BlockSpec((N_HEADS_A, N_BUCKETS), lambda b, g, pt: (0, 0)),
                      per_b((N_HEADS_A, LANES)),
                      pl.BlockSpec((1, PAGES_PER_STEP, 1, LANES), lambda b, g, pt: (b, g, 0, 0)),
                      per_b((1, 1)), per_b((1, LANES)), per_b((1, LANES))] + kv_specs + kv_specs,
            out_specs=per_b((N_HEADS_A, HEAD_DIM)),
            scratch_shapes=[pltpu.VMEM((N_HEADS_A, 1), F32), pltpu.VMEM((N_HEADS_A, 1), F32),
                            pltpu.VMEM((N_HEADS_A, LANES), F32)],
        ),
        out_shape=jax.ShapeDtypeStruct((bd, N_HEADS_A, HEAD_DIM), F32),
        compiler_params=_cparams(2),
        name="dsa_sample",
    )(page_table, bias_table.T, qz, madd, maddn, k_new, v_new,
      *([cache_k] * PAGES_PER_STEP), *([cache_v] * PAGES_PER_STEP))


def _sb_sample_kernel(pt_ref, qbd_ref, kn_ref, vn_ref, w_ref, *refs, past_len):
    k_refs = refs[:PAGES_PER_STEP]
    v_refs = refs[PAGES_PER_STEP:2 * PAGES_PER_STEP]
    o_ref, run_scr, acc_scr = refs[2 * PAGES_PER_STEP:]
    g = pl.program_id(1)
    n_steps = pl.num_programs(1)
    n_pages = n_steps * PAGES_PER_STEP
    qbd = qbd_ref[0]
    qbd16 = qbd.astype(BF16)
    lane = lax.broadcasted_iota(I32, (1, LANES), 1)
    q_pos = past_len

    @pl.when(g == 0)
    def _():
        pos = past_len + lax.broadcasted_iota(I32, (N_HEADS_B, 1), 1)
        keep = pos < q_pos
        z = jnp.sum(qbd * kn_ref[0], axis=1, keepdims=True)
        lk = jnp.where(keep, _log_keep(z), 0.0)
        a = jnp.where(keep, jnp.exp(z + lk), 0.0)
        run_scr[...] = jnp.broadcast_to(lk, run_scr.shape)
        acc_scr[...] = a * vn_ref[0]

    for p in range(PAGES_PER_STEP):
        page = n_pages - 1 - (g * PAGES_PER_STEP + p)
        keep = (page * Q_BLOCK + lane) < q_pos
        z = lax.dot_general(qbd16, k_refs[p][0].astype(BF16), _NT, preferred_element_type=F32)
        lk = jnp.where(keep, _log_keep(z), 0.0)
        res = jnp.dot(_split_hi_lo(lk), w_ref[...], preferred_element_type=F32)
        a = jnp.where(keep, jnp.exp(z + lk + res[:, :LANES] + run_scr[...]), 0.0)
        acc_scr[...] += jnp.dot(a.astype(BF16), v_refs[p][0].astype(BF16), preferred_element_type=F32)
        run_scr[...] += res[:, LANES:]

    @pl.when(g == n_steps - 1)
    def _():
        head = lax.broadcasted_iota(I32, acc_scr.shape, 0)
        lanes = lax.broadcasted_iota(I32, acc_scr.shape, 1)
        o_ref[0] = jnp.sum(jnp.where(_div_pow2(lanes, HEAD_DIM) == head, acc_scr[...], 0.0), axis=0, keepdims=True)


def _sb_sample(page_table, qbd, k_new, v_new, cache_k, cache_v, past_len):
    bd, n_pages = page_table.shape
    width = cache_k.shape[-1]
    wmat = _suffix_matrix(1)
    per_b = lambda shape: pl.BlockSpec((1,) + shape, lambda b, g, pt: (b,) + (0,) * len(shape))
    kv_specs = _page_specs((Q_BLOCK, width), n_pages, True)
    return pl.pallas_call(
        functools.partial(_sb_sample_kernel, past_len=past_len),
        grid_spec=pltpu.PrefetchScalarGridSpec(
            num_scalar_prefetch=1,
            grid=(bd, n_pages // PAGES_PER_STEP),
            in_specs=[per_b((N_HEADS_B, width)), per_b((1, width)), per_b((1, width)),
                      pl.BlockSpec(wmat.shape, lambda b, g, pt: (0, 0))] + kv_specs + kv_specs,
            out_specs=per_b((1, width)),
            scratch_shapes=[pltpu.VMEM((N_HEADS_B, LANES), F32), pltpu.VMEM((N_HEADS_B, width), F32)],
        ),
        out_shape=jax.ShapeDtypeStruct((bd, 1, width), F32),
        compiler_params=_cparams(2),
        name="sb_sample",
    )(page_table, qbd, k_new, v_new, wmat, *([cache_k] * PAGES_PER_STEP), *([cache_v] * PAGES_PER_STEP))


def _merge_kernel(x_ref, oa_ref, ob_ref, ga_ref, gb_ref, wpa, wpb, wout, ln2_ref, wr, br,
                  h_o, hn_o, comb_o):
    mixed = (ga_ref[...] * jnp.dot(oa_ref[...], wpa[...], preferred_element_type=F32)
             + gb_ref[...] * jnp.dot(ob_ref[...], wpb[...], preferred_element_type=F32))
    h = x_ref[...] + jnp.dot(mixed.astype(BF16), wout[...], preferred_element_type=F32)
    h_o[...] = h
    ms = jnp.mean(h * h, axis=-1, keepdims=True)
    hn = (h * lax.rsqrt(ms + EPS) * ln2_ref[...]).astype(BF16)
    hn_o[...] = hn

    logits = jnp.dot(hn, wr[...], preferred_element_type=F32) + br[...]
    lane = lax.broadcasted_iota(I32, logits.shape, 1)
    big = jnp.int32(ROUTER_LANES)

    def softmax_over(mask):
        m = jnp.max(jnp.where(mask, logits, NEG), axis=1, keepdims=True)
        e = jnp.where(mask, jnp.exp(logits - m), 0.0)
        return e / jnp.sum(e, axis=1, keepdims=True)

    def top1(prob, mask):
        best = jnp.max(jnp.where(mask, prob, -1.0), axis=1, keepdims=True)
        idx = jnp.min(jnp.where(mask & (prob == best), lane, big), axis=1, keepdims=True)
        return best, idx

    is_group = lane < N_GROUPS
    g_w, g_idx = top1(softmax_over(is_group), is_group)
    e_lane = lane - N_GROUPS
    in_group = (e_lane >= 0) & (e_lane < N_EXPERTS) & (_div_pow2(e_lane, EXPERTS_PER_GROUP) == g_idx)
    p_e = softmax_over(in_group)
    p1, i1 = top1(p_e, in_group)
    rest = in_group & (lane != i1)
    p2, i2 = top1(p_e, rest)
    denom = p1 + p2
    comb_o[...] = (jnp.where(lane == i1, p1 / denom * g_w, 0.0)
                   + jnp.where(lane == i2, p2 / denom * g_w, 0.0))


def _merge(x2d, o_a, o_b, g_a, g_b, wpa, wpb, wout, ln2, wr, br, tm):
    n, d = x2d.shape
    row = lambda w: pl.BlockSpec((tm, w), lambda i: (i, 0))
    consts = (wpa, wpb, wout, ln2, wr, br)
    return pl.pallas_call(
        _merge_kernel,
        grid=(n // tm,),
        in_specs=[row(d), row(WIDTH_A), row(WIDTH_B), row(d), row(d)] + [_const_spec(c.shape) for c in consts],
        out_specs=[row(d), row(d), row(ROUTER_LANES)],
        out_shape=[jax.ShapeDtypeStruct((n, d), F32), jax.ShapeDtypeStruct((n, d), BF16),
                   jax.ShapeDtypeStruct((n, ROUTER_LANES), F32)],
        compiler_params=_cparams(1),
        name="merge",
    )(x2d, o_a, o_b, g_a, g_b, *consts)


def _moe_kernel(hn_ref, comb_ref, h_ref, wgu_ref, wd_ref, lnf_ref, y_ref, acc_scr, *, final_norm):
    e = pl.program_id(1)

    @pl.when(e == 0)
    def _():
        acc_scr[...] = jnp.zeros(acc_scr.shape, F32)

    d_expert = wd_ref.shape[1]
    gu = jnp.dot(hn_ref[...], wgu_ref[0], preferred_element_type=F32)
    gate, up = gu[:, :d_expert], gu[:, d_expert:]
    comb = comb_ref[...]
    lane = lax.broadcasted_iota(I32, comb.shape, 1)
    c_e = jnp.sum(jnp.where(lane == e + N_GROUPS, comb, 0.0), axis=1, keepdims=True)
    hid = gate * _sigmoid(gate) * up * c_e
    acc_scr[...] += jnp.dot(hid.astype(BF16), wd_ref[0], preferred_element_type=F32)

    @pl.when(e == pl.num_programs(1) - 1)
    def _():
        xo = h_ref[...] + acc_scr[...]
        if final_norm:
            ms = jnp.mean(xo * xo, axis=-1, keepdims=True)
            xo = xo * lax.rsqrt(ms + EPS) * lnf_ref[...]
        y_ref[...] = xo


def _moe(hn, comb, h, wgu, wd, lnf, tm, final_norm):
    n, d = h.shape
    n_exp, _, two_f = wgu.shape
    row = lambda w: pl.BlockSpec((tm, w), lambda i, e: (i, 0))
    return pl.pallas_call(
        functools.partial(_moe_kernel, final_norm=final_norm),
        grid=(n // tm, n_exp),
        in_specs=[row(d), row(ROUTER_LANES), row(d),
                  pl.BlockSpec((1, d, two_f), lambda i, e: (e, 0, 0)),
                  pl.BlockSpec((1, two_f // 2, d), lambda i, e: (e, 0, 0)),
                  pl.BlockSpec((1, d), lambda i, e: (0, 0))],
        out_specs=row(d),
        out_shape=jax.ShapeDtypeStruct((n, d), F32),
        scratch_shapes=[pltpu.VMEM((tm, d), F32)],
        compiler_params=_cparams(2),
        name="moe",
    )(hn, comb, h, wgu, wd, lnf)


def _tile(n, pref):
    return pref if n % pref == 0 else n


def kernel(x_prompt, x_sample, cache_k_a, cache_v_a, cache_kidx_a, cache_k_b, cache_v_b, page_table,
           rel_bias_table, ln1_g, w_in, w_proj_a, w_proj_b, w_out, ln2_g, w_router_group, b_router_group,
           w_router_expert, b_router_expert, w_gate, w_up, w_down, ln_f_g):
    bsz, seq, d = x_prompt.shape
    bd, t_new, _ = x_sample.shape
    assert t_new == 1, "the sample group decodes one token per sequence"
    depth = w_in.shape[0]
    pool = cache_k_a.shape[1]
    n_pages = page_table.shape[1]
    past_len = n_pages * Q_BLOCK
    assert n_pages % PAGES_PER_STEP == 0 and seq % Q_BLOCK == 0
    n_p, n_s = bsz * seq, bd * t_new
    d_expert = w_gate.shape[-1]

    xp, xs = x_prompt.reshape(n_p, d), x_sample.reshape(n_s, d)
    new_p, new_s = [], []
    for l in range(depth):
        w_pieces = _split_w_in(w_in[l])
        wpa, wpb, wout = (w.astype(BF16) for w in (w_proj_a[l], w_proj_b[l], w_out[l]))
        ln2 = ln2_g[l].reshape(1, d)
        wr = jnp.concatenate([w_router_group[l], jnp.moveaxis(w_router_expert[l], 0, 1).reshape(d, N_EXPERTS),
                              jnp.zeros((d, ROUTER_LANES - N_GROUPS - N_EXPERTS), F32)], axis=1).astype(BF16)
        br = jnp.concatenate([b_router_group[l], b_router_expert[l].reshape(N_EXPERTS),
                              jnp.zeros((ROUTER_LANES - N_GROUPS - N_EXPERTS,), F32)]).reshape(1, ROUTER_LANES)
        wgu = jnp.concatenate([w_gate[l], w_up[l]], axis=-1).reshape(N_EXPERTS, d, 2 * d_expert).astype(BF16)
        wd = w_down[l].reshape(N_EXPERTS, d_expert, d).astype(BF16)
        lnf = ln_f_g.reshape(1, d)

        def ffn(x2d, o_a, o_b, g_a, g_b, tm_merge, tm_moe):
            h, hn, comb = _merge(x2d, o_a, o_b, g_a, g_b, wpa, wpb, wout, ln2, wr, br, tm_merge)
            return _moe(hn, comb, h, wgu, wd, lnf, tm_moe, final_norm=(l == depth - 1))

        (qa, ka, va, kad, vad, qi, ki, kid, wi, qb, kb, vb, kb16, vb16, ga, gb) = _project(
            xp, ln1_g[l], w_pieces, _tile(n_p, 512))
        r3 = lambda a: a.reshape(bsz, seq, a.shape[-1])
        n_top = min(TOPK_MAX, seq // 4)
        o_a = _dsa_prompt(rel_bias_table, r3(qi), r3(wi), r3(kid), r3(qa), r3(kad), r3(vad), n_top)
        o_b = _sb_prompt(r3(qb), r3(kb16), r3(vb16))
        xp = ffn(xp, o_a.reshape(n_p, WIDTH_A), o_b.reshape(n_p, WIDTH_B), ga, gb, _tile(n_p, 512), _tile(n_p, 1024))
        new_p.append((ka, va, ki, kb, vb))

        (qa, ka, va, _, _, qi, ki, _, wi, qb, kb, vb, _, _, ga, gb) = _project(xs, ln1_g[l], w_pieces, n_s)
        n_top = min(TOPK_MAX, (past_len + t_new) // 4)
        qi3 = qi.astype(F32).reshape(bd, N_IDX_HEADS, IDX_DIM)
        wi3 = wi.reshape(bd, N_IDX_HEADS, 1)
        scores = _idx_sample(page_table, qi3, wi3, cache_kidx_a[l])
        madd, maddn = _select_sample(scores.reshape(bd, n_pages, LANES), qi3, wi3, ki.reshape(bd, 1, IDX_DIM), n_top)
        qa4 = qa.astype(F32).reshape(bd, N_HEADS_A, 1, HEAD_DIM) * (HEAD_DIM ** -0.5)
        kv_of_head = (np.arange(N_HEADS_A)[:, None] // GROUP_A) == np.arange(N_KV_A)[None, :]
        qz = jnp.where(kv_of_head[None, :, :, None], qa4, 0.0).reshape(bd, N_HEADS_A, N_KV_A * HEAD_DIM)
        o_a = _dsa_sample(page_table, rel_bias_table, qz, madd.reshape(bd, n_pages, 1, LANES), maddn,
                          ka.reshape(bd, 1, -1), va.reshape(bd, 1, -1),
                          cache_k_a[l].reshape(pool, Q_BLOCK, -1), cache_v_a[l].reshape(pool, Q_BLOCK, -1), past_len)
        qb4 = qb.astype(F32).reshape(bd, N_HEADS_B, 1, HEAD_DIM) * (HEAD_DIM ** -0.5)
        own = np.eye(N_HEADS_B, dtype=bool)
        qbd = jnp.where(own[None, :, :, None], qb4, 0.0).reshape(bd, N_HEADS_B, WIDTH_B)
        o_b = _sb_sample(page_table, qbd, kb.reshape(bd, 1, -1), vb.reshape(bd, 1, -1),
                         cache_k_b[l].reshape(pool, Q_BLOCK, -1), cache_v_b[l].reshape(pool, Q_BLOCK, -1), past_len)
        xs = ffn(xs, o_a.reshape(n_s, WIDTH_A).astype(BF16), o_b.reshape(n_s, WIDTH_B).astype(BF16), ga, gb, n_s, n_s)
        new_s.append((ka, va, ki, kb, vb))

    def stack(rows, idx, lead, tail):
        return jnp.stack([r[idx].reshape(lead + tail) for r in rows])

    outs = [xp.reshape(bsz, seq, d), xs.reshape(bd, t_new, d)]
    tails = ((N_KV_A, HEAD_DIM), (N_KV_A, HEAD_DIM), (IDX_DIM,), (N_HEADS_B, HEAD_DIM), (N_HEADS_B, HEAD_DIM))
    for rows, lead in ((new_p, (bsz, seq)), (new_s, (bd, t_new))):
        for idx, tail in enumerate(tails):
            outs.append(stack(rows, idx, lead, tail))
    return tuple(outs)
```

```python
import functools
import math

import jax
import jax.numpy as jnp
import numpy as np
from jax import lax
from jax.experimental import pallas as pl
from jax.experimental.pallas import tpu as pltpu

F32 = jnp.float32
BF16 = jnp.bfloat16
I32 = jnp.int32

HEAD_DIM = 64
N_HEADS_A = 8
N_KV_A = 2
GROUP_A = N_HEADS_A // N_KV_A
N_IDX_HEADS = 8
IDX_DIM = 64
N_HEADS_B = 8
WIDTH_A = N_HEADS_A * HEAD_DIM
WIDTH_B = N_HEADS_B * HEAD_DIM
TOPK_MAX = 256
N_BUCKETS = 32
MAX_DISTANCE = 128
N_GROUPS = 4
EXPERTS_PER_GROUP = 8
N_EXPERTS = N_GROUPS * EXPERTS_PER_GROUP
TOP_E = 2
Q_BLOCK = 128
EPS = 1e-6

LANES = 128
VMEM_LIMIT_BYTES = 56 * 1024 * 1024

PAGES_PER_STEP = 8
NEG = -1e30
INT_MIN = -(2 ** 31)
ROUTER_LANES = LANES

_NT = (((1,), (1,)), ((), ()))


def _cparams(n_axes):
    return pltpu.CompilerParams(dimension_semantics=("arbitrary",) * n_axes,
                                vmem_limit_bytes=VMEM_LIMIT_BYTES)


def _const_spec(shape):
    nd = len(shape)
    return pl.BlockSpec(shape, lambda *_: (0,) * nd)


def _div_pow2(x, n):
    assert n & (n - 1) == 0
    return x >> (n.bit_length() - 1)


def _sigmoid(x):
    return 1.0 / (1.0 + jnp.exp(-x))


def _half_masks(dtype):
    lane = lax.broadcasted_iota(I32, (LANES, LANES), 1)
    lo = jnp.where(lane < HEAD_DIM, 1.0, 0.0).astype(dtype)
    hi = jnp.where(lane < HEAD_DIM, 0.0, 1.0).astype(dtype)
    return lo, hi


def _pair_block_diag(blk, lo, hi):
    return jnp.concatenate([blk * lo, blk * hi], axis=0)


def _row_half_masks(dtype):
    r = lax.broadcasted_iota(I32, (LANES, LANES), 0)
    top = jnp.where(r < HEAD_DIM, 1.0, 0.0).astype(dtype)
    bottom = jnp.where(r < HEAD_DIM, 0.0, 1.0).astype(dtype)
    return top, bottom


def _pair_block_diag_t(blk_t, top, bottom):
    return jnp.concatenate([blk_t * top, blk_t * bottom], axis=1)


def _col_to_row(x):
    n = x.shape[0]
    eye = lax.broadcasted_iota(I32, (n, n), 0) == lax.broadcasted_iota(I32, (n, n), 1)
    return jnp.sum(jnp.where(eye, x, 0.0), axis=0, keepdims=True)


def _order_key(score):
    score = jnp.where(score == 0.0, 0.0, score)
    b = lax.bitcast_convert_type(score, I32)
    return b ^ ((b >> 31) & jnp.int32(0x7FFFFFFF))


def _t5_bucket(dist):
    n = jnp.maximum(dist, 0)
    max_exact = N_BUCKETS // 2
    nf = jnp.maximum(n, max_exact).astype(F32)
    large = max_exact + (jnp.log(nf / max_exact) / math.log(MAX_DISTANCE / max_exact)
                         * (N_BUCKETS - max_exact)).astype(I32)
    return jnp.where(n < max_exact, n, jnp.minimum(large, N_BUCKETS - 1))


def _proj_kernel(x_ref, g_ref, wqa, wkv, wkvd, wqi, wkiw, wqb, wkb, wvb, wga, wgb,
                 qa_o, ka_o, va_o, kad_o, vad_o, qi_o, ki_o, kid_o, wi_o,
                 qb_o, kb_o, vb_o, kb16_o, vb16_o, ga_o, gb_o):
    x = x_ref[...]
    ms = jnp.mean(x * x, axis=-1, keepdims=True)
    xn = (x * lax.rsqrt(ms + EPS) * g_ref[...]).astype(BF16)

    def mm(w):
        return jnp.dot(xn, w[...], preferred_element_type=F32)

    qa_o[...] = mm(wqa).astype(BF16)
    kv = mm(wkv)
    ka_o[...] = kv[:, :N_KV_A * HEAD_DIM]
    va_o[...] = kv[:, N_KV_A * HEAD_DIM:]
    kvd = mm(wkvd).astype(BF16)
    kad_o[...] = kvd[:, :2 * LANES]
    vad_o[...] = kvd[:, 2 * LANES:]
    qi_o[...] = mm(wqi).astype(BF16)
    kiw = mm(wkiw)
    ki_o[...] = kiw[:, :IDX_DIM]
    kid_o[...] = kiw[:, :LANES].astype(BF16)
    wi_o[...] = kiw[:, LANES:LANES + N_IDX_HEADS] * (N_IDX_HEADS ** -0.5)
    qb_o[...] = mm(wqb).astype(BF16)
    kb = mm(wkb)
    kb_o[...] = kb
    kb16_o[...] = kb.astype(BF16)
    vb = mm(wvb)
    vb_o[...] = vb
    vb16_o[...] = vb.astype(BF16)
    ga_o[...] = _sigmoid(mm(wga))
    gb_o[...] = _sigmoid(mm(wgb))


def _split_w_in(w_in):
    d_model = w_in.shape[0]
    widths = (WIDTH_A, N_KV_A * HEAD_DIM, N_KV_A * HEAD_DIM, N_IDX_HEADS * IDX_DIM, N_IDX_HEADS, IDX_DIM,
              WIDTH_B, WIDTH_B, WIDTH_B, d_model, d_model)
    points = [int(p) for p in np.cumsum(widths)[:-1]]
    qa, ka, va, qi, wi, ki, qb, kb, vb, ga, gb = jnp.split(w_in, points, axis=1)
    dup = lambda a: jnp.concatenate([a[:, :HEAD_DIM], a[:, :HEAD_DIM], a[:, HEAD_DIM:], a[:, HEAD_DIM:]], axis=1)
    wkv = jnp.concatenate([ka, va], axis=1)
    wkvd = jnp.concatenate([dup(ka), dup(va)], axis=1)
    wkiw = jnp.concatenate([ki, ki, wi, jnp.zeros((d_model, LANES - N_IDX_HEADS), w_in.dtype)], axis=1)
    return tuple(a.astype(BF16) for a in (qa, wkv, wkvd, qi, wkiw, qb, kb, vb, ga, gb))


def _project(x2d, ln_g, w_pieces, tm):
    n, d = x2d.shape
    assert n % tm == 0
    out_defs = [
        (WIDTH_A, BF16), (N_KV_A * HEAD_DIM, F32), (N_KV_A * HEAD_DIM, F32), (2 * LANES, BF16), (2 * LANES, BF16),
        (N_IDX_HEADS * IDX_DIM, BF16), (IDX_DIM, F32), (LANES, BF16), (N_IDX_HEADS, F32),
        (WIDTH_B, BF16), (WIDTH_B, F32), (WIDTH_B, F32), (WIDTH_B, BF16), (WIDTH_B, BF16), (d, F32), (d, F32)]
    row = lambda i: (i, 0)
    return pl.pallas_call(
        _proj_kernel,
        grid=(n // tm,),
        in_specs=[pl.BlockSpec((tm, d), row), _const_spec((1, d))] + [_const_spec(w.shape) for w in w_pieces],
        out_specs=[pl.BlockSpec((tm, w), row) for w, _ in out_defs],
        out_shape=[jax.ShapeDtypeStruct((n, w), dt) for w, dt in out_defs],
        compiler_params=_cparams(1),
        name="proj",
    )(x2d, ln_g.reshape(1, d), *w_pieces)


def _dsa_prompt_kernel(tab_ref, qi_ref, wit_ref, kid_ref, qa_ref, kad_ref, vadt_ref, tril_ref, o_ref,
                       key_scr, madd_scr, bias_scr, acc_scr, *, n_top):
    b = pl.program_id(0)
    i = pl.program_id(1)
    qb = Q_BLOCK
    row = lax.broadcasted_iota(I32, (qb, qb), 0)
    col = lax.broadcasted_iota(I32, (qb, qb), 1)
    lo, hi = _half_masks(BF16)
    top, bottom = _row_half_masks(BF16)

    @pl.when((b == 0) & (i == 0))
    def _():
        for d in range(3):
            bucket = _t5_bucket(d * qb + col - row)
            for h in range(N_HEADS_A):
                def sel(bk, acc):
                    return jnp.where(bucket == bk, tab_ref[bk, h], acc)
                tile = lax.fori_loop(0, N_BUCKETS, sel, jnp.zeros((qb, qb), F32))
                bias_scr[d, h // 2, (h % 2) * qb:(h % 2 + 1) * qb, :] = tile

    def causal(j):
        return (i - j) * qb + col - row >= 0

    w = wit_ref[0] * (IDX_DIM ** -0.5)
    wrows = [w[h:h + 1, :] for h in range(N_IDX_HEADS)]

    def score_body(j, carry):
        kblk = kid_ref[0, pl.ds(pl.multiple_of(j * qb, qb), qb), :]
        kbd = _pair_block_diag(kblk, lo, hi)
        acc = jnp.zeros((qb, qb), F32)
        for p in range(N_IDX_HEADS // 2):
            r = lax.dot_general(kbd, qi_ref[0, :, p * LANES:(p + 1) * LANES], _NT,
                                preferred_element_type=F32)
            acc = acc + jnp.maximum(r[:qb], 0.0) * wrows[2 * p] + jnp.maximum(r[qb:], 0.0) * wrows[2 * p + 1]
        key_scr[j] = _order_key(jnp.where(causal(j), acc, -jnp.inf))
        return carry

    lax.fori_loop(0, i + 1, score_body, 0)

    def count(pred):
        def body(j, acc):
            return acc + jnp.where(pred(key_scr[j]), 1, 0)
        acc = lax.fori_loop(0, i + 1, body, jnp.zeros((qb, qb), I32))
        return jnp.sum(acc, axis=0, keepdims=True)

    thr = jnp.where(count(lambda k: k >= 0) >= n_top, 0, INT_MIN).astype(I32)

    def bit_body(t, thr):
        cand = thr | lax.shift_left(jnp.int32(1), (30 - t).astype(I32))
        return jnp.where(count(lambda k: k >= cand) >= n_top, cand, thr)

    thr = lax.fori_loop(0, 31, bit_body, thr)
    need = (n_top - count(lambda k: k > thr)).astype(F32)

    tril = tril_ref[...]

    def mask_body(j, ties_before):
        key = key_scr[j]
        eq = key == thr
        pre = jnp.dot(tril, jnp.where(eq, 1.0, 0.0).astype(BF16), preferred_element_type=F32)
        take = (key > thr) | (eq & (ties_before + pre <= need))
        madd_scr[j] = jnp.where(take & causal(j), 0.0, NEG)
        return ties_before + pre[qb - 1:qb, :]

    lax.fori_loop(0, i + 1, mask_body, jnp.zeros((1, qb), F32))

    n_pairs = N_HEADS_A // 2
    q_ps = [(qa_ref[0, :, p * LANES:(p + 1) * LANES].astype(F32) * (HEAD_DIM ** -0.5)).astype(BF16)
            for p in range(n_pairs)]
    acc_scr[...] = jnp.zeros(acc_scr.shape, F32)

    def att_body(j, stats):
        rows = pl.ds(pl.multiple_of(j * qb, qb), qb)
        madd = madd_scr[j]
        dsel = jnp.minimum(i - j, 2)
        out = []
        for p in range(n_pairs):
            c = (2 * p) // GROUP_A
            m0, l0, m1, l1 = stats[4 * p:4 * p + 4]
            kbd = _pair_block_diag(kad_ref[0, rows, c * LANES:(c + 1) * LANES], lo, hi)
            vbd = _pair_block_diag_t(vadt_ref[0, j, c * LANES:(c + 1) * LANES, :], top, bottom)
            lg = lax.dot_general(kbd, q_ps[p], _NT, preferred_element_type=F32) + bias_scr[dsel, p]
            lg0 = lg[:qb] + madd
            lg1 = lg[qb:] + madd
            n0 = jnp.maximum(m0, jnp.max(lg0, axis=0, keepdims=True))
            n1 = jnp.maximum(m1, jnp.max(lg1, axis=0, keepdims=True))
            p0 = jnp.exp(lg0 - n0)
            p1 = jnp.exp(lg1 - n1)
            a0 = jnp.exp(m0 - n0)
            a1 = jnp.exp(m1 - n1)
            l0 = a0 * l0 + jnp.sum(p0, axis=0, keepdims=True)
            l1 = a1 * l1 + jnp.sum(p1, axis=0, keepdims=True)
            pv = jnp.dot(vbd, jnp.concatenate([p0, p1], axis=0).astype(BF16), preferred_element_type=F32)
            acc_scr[p] = acc_scr[p] * jnp.where(row < HEAD_DIM, a0, a1) + pv
            out += [n0, l0, n1, l1]
        return tuple(out)

    neg = jnp.full((1, qb), NEG, F32)
    zero = jnp.zeros((1, qb), F32)
    stats = lax.fori_loop(0, i + 1, att_body, (neg, zero, neg, zero) * n_pairs)
    for p in range(n_pairs):
        denom = jnp.where(row < HEAD_DIM, stats[4 * p + 1], stats[4 * p + 3])
        o_ref[0, p * LANES:(p + 1) * LANES, :] = (acc_scr[p] / denom).astype(o_ref.dtype)


def _dsa_prompt(bias_table, qi, wit, kid, qa, kad, vadt, n_top):
    bsz, seq, _ = qi.shape
    nq = seq // Q_BLOCK
    tril = (np.arange(Q_BLOCK)[:, None] >= np.arange(Q_BLOCK)[None, :]).astype(np.float32)
    qblk = lambda w: pl.BlockSpec((1, Q_BLOCK, w), lambda b, i: (b, i, 0))
    full = lambda w: pl.BlockSpec((1, seq, w), lambda b, i: (b, 0, 0))
    return pl.pallas_call(
        functools.partial(_dsa_prompt_kernel, n_top=n_top),
        grid=(bsz, nq),
        in_specs=[pl.BlockSpec(memory_space=pltpu.SMEM), qblk(N_IDX_HEADS * IDX_DIM),
                  pl.BlockSpec((1, N_IDX_HEADS, Q_BLOCK), lambda b, i: (b, 0, i)), full(LANES),
                  qblk(WIDTH_A), full(2 * LANES),
                  pl.BlockSpec((1, nq, 2 * LANES, Q_BLOCK), lambda b, i: (b, 0, 0, 0)),
                  _const_spec((Q_BLOCK, Q_BLOCK))],
        out_specs=pl.BlockSpec((1, WIDTH_A, Q_BLOCK), lambda b, i: (b, 0, i)),
        out_shape=jax.ShapeDtypeStruct((bsz, WIDTH_A, seq), BF16),
        scratch_shapes=[pltpu.VMEM((nq, Q_BLOCK, Q_BLOCK), I32), pltpu.VMEM((nq, Q_BLOCK, Q_BLOCK), F32),
                        pltpu.VMEM((3, N_HEADS_A // 2, 2 * Q_BLOCK, Q_BLOCK), F32),
                        pltpu.VMEM((N_HEADS_A // 2, LANES, Q_BLOCK), F32)],
        compiler_params=_cparams(2),
        name="dsa_prompt",
    )(bias_table, qi, wit, kid, qa, kad, vadt, jnp.asarray(tril, BF16))


def _suffix_matrix(n_pairs):
    n = n_pairs * Q_BLOCK
    r = np.arange(n)
    same = (r[:, None] // Q_BLOCK) == (r[None, :] // Q_BLOCK)
    u = same & (r[:, None] > r[None, :])
    w = np.concatenate([u, same], axis=1).astype(np.float32)
    return jnp.asarray(np.concatenate([w, w], axis=0), BF16)


def _log_keep(z):
    return -(jnp.maximum(z, 0.0) + jnp.log(1.0 + jnp.exp(-jnp.abs(z))))


def _split_hi_lo(x):
    hi = x.astype(BF16)
    lo = (x - hi.astype(F32)).astype(BF16)
    return jnp.concatenate([hi, lo], axis=1)


def _sb_prompt_kernel(qb_ref, kb_ref, vb_ref, w_ref, o_ref, run_scr, acc_scr):
    i = pl.program_id(1)
    qb = Q_BLOCK
    n_pairs = N_HEADS_B // 2
    row = lax.broadcasted_iota(I32, (qb, 2 * qb), 0)
    col = lax.broadcasted_iota(I32, (qb, 2 * qb), 1)
    strict = (col & (qb - 1)) < row
    lo, hi = _half_masks(BF16)
    q_ps = [(qb_ref[0, :, p * LANES:(p + 1) * LANES].astype(F32) * (HEAD_DIM ** -0.5)).astype(BF16)
            for p in range(n_pairs)]

    def block(j, diagonal):
        rows = pl.ds(pl.multiple_of(j * qb, qb), qb)
        for p in range(n_pairs):
            kbd = _pair_block_diag(kb_ref[0, rows, p * LANES:(p + 1) * LANES], lo, hi)
            vbd = _pair_block_diag(vb_ref[0, rows, p * LANES:(p + 1) * LANES], lo, hi)
            z = lax.dot_general(q_ps[p], kbd, _NT, preferred_element_type=F32)
            lk = _log_keep(z)
            if diagonal:
                lk = jnp.where(strict, lk, 0.0)
            res = jnp.dot(_split_hi_lo(lk), w_ref[...], preferred_element_type=F32)
            x = z + lk + res[:, :2 * qb]
            if diagonal:
                a = jnp.where(strict, jnp.exp(x), 0.0)
                run_scr[p] = res[:, 2 * qb:]
                acc_scr[p] = jnp.dot(a.astype(BF16), vbd, preferred_element_type=F32)
            else:
                a = jnp.exp(x + run_scr[p])
                run_scr[p] += res[:, 2 * qb:]
                acc_scr[p] += jnp.dot(a.astype(BF16), vbd, preferred_element_type=F32)

    block(i, True)

    def body(t, carry):
        block(i - 1 - t, False)
        return carry

    lax.fori_loop(0, i, body, 0)
    for p in range(n_pairs):
        o_ref[0, :, p * LANES:(p + 1) * LANES] = acc_scr[p].astype(o_ref.dtype)


def _sb_prompt(qb, kb16, vb16):
    bsz, seq, width = qb.shape
    nq = seq // Q_BLOCK
    wmat = _suffix_matrix(2)
    qblk = pl.BlockSpec((1, Q_BLOCK, width), lambda b, i: (b, i, 0))
    full = pl.BlockSpec((1, seq, width), lambda b, i: (b, 0, 0))
    return pl.pallas_call(
        _sb_prompt_kernel,
        grid=(bsz, nq),
        in_specs=[qblk, full, full, _const_spec(wmat.shape)],
        out_specs=qblk,
        out_shape=jax.ShapeDtypeStruct((bsz, seq, width), BF16),
        scratch_shapes=[pltpu.VMEM((N_HEADS_B // 2, Q_BLOCK, 2 * Q_BLOCK), F32),
                        pltpu.VMEM((N_HEADS_B // 2, Q_BLOCK, Q_BLOCK), F32)],
        compiler_params=_cparams(2),
        name="sb_prompt",
    )(qb, kb16, vb16, wmat)


def _page_specs(block_tail, n_pages, reverse):
    def spec(p):
        def index_map(b, g, pt):
            page = g * PAGES_PER_STEP + p
            if reverse:
                page = n_pages - 1 - page
            return (pt[b, page],) + (0,) * len(block_tail)
        return pl.BlockSpec((1,) + block_tail, index_map)
    return [spec(p) for p in range(PAGES_PER_STEP)]


def _idx_sample_kernel(pt_ref, q_ref, w_ref, *refs):
    k_refs, o_ref = refs[:PAGES_PER_STEP], refs[PAGES_PER_STEP]
    q = q_ref[0].astype(BF16)
    w = w_ref[0] * (IDX_DIM ** -0.5)
    for p in range(PAGES_PER_STEP):
        r = jnp.dot(q, k_refs[p][0].astype(BF16), preferred_element_type=F32)
        o_ref[0, p] = jnp.sum(jnp.maximum(r, 0.0) * w, axis=0, keepdims=True)


def _idx_sample(page_table, qi3, wi3, cache_kidx):
    bd, n_pages = page_table.shape
    per_b = lambda shape: pl.BlockSpec((1,) + shape, lambda b, g, pt: (b,) + (0,) * len(shape))
    return pl.pallas_call(
        _idx_sample_kernel,
        grid_spec=pltpu.PrefetchScalarGridSpec(
            num_scalar_prefetch=1,
            grid=(bd, n_pages // PAGES_PER_STEP),
            in_specs=[per_b((N_IDX_HEADS, IDX_DIM)), per_b((N_IDX_HEADS, 1))]
            + _page_specs((IDX_DIM, Q_BLOCK), n_pages, False),
            out_specs=pl.BlockSpec((1, PAGES_PER_STEP, 1, LANES), lambda b, g, pt: (b, g, 0, 0)),
        ),
        out_shape=jax.ShapeDtypeStruct((bd, n_pages, 1, LANES), F32),
        compiler_params=_cparams(2),
        name="idx_sample",
    )(page_table, qi3, wi3, *([cache_kidx] * PAGES_PER_STEP))


def _select_sample_kernel(sc_ref, q_ref, w_ref, k_ref, tri_ref, low_ref, madd_o, maddn_o, *, n_top):
    n_pages = sc_ref.shape[1]
    key = _order_key(sc_ref[0])
    r_new = jnp.sum(q_ref[0] * k_ref[0], axis=1, keepdims=True)
    s_new = jnp.sum(jnp.maximum(r_new * (IDX_DIM ** -0.5), 0.0) * w_ref[0], axis=0, keepdims=True)
    key_new = _order_key(s_new)

    def total(x):
        return jnp.sum(jnp.sum(x, axis=1, keepdims=True), axis=0, keepdims=True)

    def count(pred):
        return total(jnp.where(pred(key), 1, 0)) + jnp.where(pred(key_new), 1, 0)

    thr = jnp.where(count(lambda k: k >= 0) >= n_top, 0, INT_MIN).astype(I32)

    def bit_body(t, thr):
        cand = thr | lax.shift_left(jnp.int32(1), (30 - t).astype(I32))
        return jnp.where(count(lambda k: k >= cand) >= n_top, cand, thr)

    thr = lax.fori_loop(0, 31, bit_body, thr)
    need = (n_top - count(lambda k: k > thr)).astype(F32)
    eq = jnp.where(key == thr, 1.0, 0.0).astype(BF16)
    pre = jnp.dot(eq, tri_ref[...], preferred_element_type=F32)
    tot = jnp.broadcast_to(pre[:, LANES - 1:LANES], pre.shape).astype(BF16)
    before = jnp.dot(low_ref[...], tot, preferred_element_type=F32)
    take = (key > thr) | ((key == thr) & (before + pre <= need))
    madd_o[0] = jnp.where(take, 0.0, NEG)
    ties_past = total(jnp.where(key == thr, 1.0, 0.0))
    take_new = (key_new > thr) | ((key_new == thr) & (ties_past + 1.0 <= need))
    maddn_o[0] = jnp.where(take_new, 0.0, NEG)


def _select_sample(scores, qi3, wi3, ki3, n_top):
    bd, n_pages, _ = scores.shape
    tri = jnp.asarray((np.arange(LANES)[:, None] <= np.arange(LANES)[None, :]).astype(np.float32), BF16)
    low = jnp.asarray((np.arange(n_pages)[:, None] > np.arange(n_pages)[None, :]).astype(np.float32), BF16)
    per_b = lambda shape: pl.BlockSpec((1,) + shape, lambda b: (b,) + (0,) * len(shape))
    return pl.pallas_call(
        functools.partial(_select_sample_kernel, n_top=n_top),
        grid=(bd,),
        in_specs=[per_b((n_pages, LANES)), per_b((N_IDX_HEADS, IDX_DIM)), per_b((N_IDX_HEADS, 1)),
                  per_b((1, IDX_DIM)), _const_spec(tri.shape), _const_spec(low.shape)],
        out_specs=[per_b((n_pages, LANES)), per_b((1, 1))],
        out_shape=[jax.ShapeDtypeStruct((bd, n_pages, LANES), F32), jax.ShapeDtypeStruct((bd, 1, 1), F32)],
        compiler_params=_cparams(1),
        name="select_sample",
    )(scores, qi3, wi3, ki3, tri, low)


def _sample_bias(tabt, dist):
    bucket = _t5_bucket(dist)
    out = jnp.zeros((tabt.shape[0], dist.shape[1]), F32)
    for bk in range(N_BUCKETS):
        out = jnp.where(bucket == bk, tabt[:, bk:bk + 1], out)
    return out


def _dsa_logits_kernel(pt_ref, tabt_ref, qz_ref, madd_ref, *refs, past_len):
    k_refs, o_ref = refs[:PAGES_PER_STEP], refs[PAGES_PER_STEP]
    g = pl.program_id(1)
    qz16 = qz_ref[0].astype(BF16)
    lane = lax.broadcasted_iota(I32, (1, LANES), 1)
    tabt = tabt_ref[...]
    for p in range(PAGES_PER_STEP):
        page = g * PAGES_PER_STEP + p
        lg = jnp.dot(qz16, k_refs[p][0].astype(BF16), preferred_element_type=F32)
        o_ref[0, p] = lg + _sample_bias(tabt, past_len - (page * Q_BLOCK + lane)) + madd_ref[0, p]


def _dsa_logits(page_table, bias_table, qz, madd, cache_kt, past_len):
    bd, n_pages = page_table.shape
    per_b = lambda shape: pl.BlockSpec((1,) + shape, lambda b, g, pt: (b,) + (0,) * len(shape))
    return pl.pallas_call(
        functools.partial(_dsa_logits_kernel, past_len=past_len),
        grid_spec=pltpu.PrefetchScalarGridSpec(
            num_scalar_prefetch=1,
            grid=(bd, n_pages // PAGES_PER_STEP),
            in_specs=[pl.BlockSpec((N_HEADS_A, N_BUCKETS), lambda b, g, pt: (0, 0)),
                      per_b((N_HEADS_A, LANES)),
                      pl.BlockSpec((1, PAGES_PER_STEP, 1, LANES), lambda b, g, pt: (b, g, 0, 0))]
            + _page_specs((N_KV_A * HEAD_DIM, Q_BLOCK), n_pages, False),
            out_specs=pl.BlockSpec((1, PAGES_PER_STEP, N_HEADS_A, LANES), lambda b, g, pt: (b, g, 0, 0)),
        ),
        out_shape=jax.ShapeDtypeStruct((bd, n_pages, N_HEADS_A, LANES), F32),
        compiler_params=_cparams(2),
        name="dsa_logits",
    )(page_table, bias_table.T, qz, madd, *([cache_kt] * PAGES_PER_STEP))


def _dsa_pv_kernel(pt_ref, tabt_ref, lg_ref, qz_ref, maddn_ref, kn_ref, vnc_ref, *refs):
    v_refs = refs[:PAGES_PER_STEP]
    o_ref, m_scr, lsum_scr, acc_scr = refs[PAGES_PER_STEP:]
    g = pl.program_id(1)

    def new_logit():
        lg = jnp.sum(qz_ref[0] * kn_ref[0], axis=1, keepdims=True)
        return lg + _sample_bias(tabt_ref[...], jnp.zeros((1, 1), I32)) + maddn_ref[0]

    @pl.when(g == 0)
    def _():
        m_past = jnp.max(jnp.max(lg_ref[0], axis=0), axis=1, keepdims=True)
        m_scr[...] = jnp.maximum(m_past, new_logit())
        lsum_scr[...] = jnp.zeros(lsum_scr.shape, F32)
        acc_scr[...] = jnp.zeros(acc_scr.shape, F32)

    m = m_scr[...]
    for p in range(PAGES_PER_STEP):
        pr = jnp.exp(lg_ref[0, g * PAGES_PER_STEP + p] - m)
        lsum_scr[...] += pr
        acc_scr[...] += lax.dot_general(v_refs[p][0].astype(BF16), pr.astype(BF16), _NT,
                                        preferred_element_type=F32)

    @pl.when(g == pl.num_programs(1) - 1)
    def _():
        p_new = jnp.exp(new_logit() - m)
        denom = jnp.sum(lsum_scr[...], axis=1, keepdims=True) + p_new
        out = (acc_scr[...] + vnc_ref[0] * _col_to_row(p_new)) / _col_to_row(denom)
        r = lax.broadcasted_iota(I32, out.shape, 0)
        h = lax.broadcasted_iota(I32, out.shape, 1)
        out = jnp.where(_div_pow2(r, HEAD_DIM) == _div_pow2(h, GROUP_A), out, 0.0)
        o_ref[0] = out[:HEAD_DIM] + out[HEAD_DIM:]


def _dsa_pv(page_table, bias_table, logits, qz, maddn, k_new, v_new_col, cache_vt):
    bd, n_pages = page_table.shape
    per_b = lambda shape: pl.BlockSpec((1,) + shape, lambda b, g, pt: (b,) + (0,) * len(shape))
    kv = N_KV_A * HEAD_DIM
    return pl.pallas_call(
        _dsa_pv_kernel,
        grid_spec=pltpu.PrefetchScalarGridSpec(
            num_scalar_prefetch=1,
            grid=(bd, n_pages // PAGES_PER_STEP),
            in_specs=[pl.BlockSpec((N_HEADS_A, N_BUCKETS), lambda b, g, pt: (0, 0)),
                      per_b((n_pages, N_HEADS_A, LANES)), per_b((N_HEADS_A, kv)), per_b((1, 1)),
                      per_b((1, kv)), per_b((kv, 1))] + _page_specs((kv, Q_BLOCK), n_pages, False),
            out_specs=per_b((HEAD_DIM, N_HEADS_A)),
            scratch_shapes=[pltpu.VMEM((N_HEADS_A, 1), F32), pltpu.VMEM((N_HEADS_A, LANES), F32),
                            pltpu.VMEM((kv, N_HEADS_A), F32)],
        ),
        out_shape=jax.ShapeDtypeStruct((bd, HEAD_DIM, N_HEADS_A), F32),
        compiler_params=_cparams(2),
        name="dsa_pv",
    )(page_table, bias_table.T, logits, qz, maddn, k_new, v_new_col, *([cache_vt] * PAGES_PER_STEP))


def _sb_sample_kernel(pt_ref, qbd_ref, kn_ref, vnc_ref, w_ref, *refs, past_len):
    k_refs = refs[:PAGES_PER_STEP]
    v_refs = refs[PAGES_PER_STEP:2 * PAGES_PER_STEP]
    o_ref, run_scr, acc_scr = refs[2 * PAGES_PER_STEP:]
    g = pl.program_id(1)
    n_steps = pl.num_programs(1)
    n_pages = n_steps * PAGES_PER_STEP
    qbd = qbd_ref[0]
    qbd16 = qbd.astype(BF16)
    lane = lax.broadcasted_iota(I32, (1, LANES), 1)
    q_pos = past_len

    @pl.when(g == 0)
    def _():
        pos = past_len + lax.broadcasted_iota(I32, (N_HEADS_B, 1), 1)
        keep = pos < q_pos
        z = jnp.sum(qbd * kn_ref[0], axis=1, keepdims=True)
        lk = jnp.where(keep, _log_keep(z), 0.0)
        a = jnp.where(keep, jnp.exp(z + lk), 0.0)
        run_scr[...] = jnp.broadcast_to(lk, run_scr.shape)
        acc_scr[...] = vnc_ref[0] * _col_to_row(a)

    run = run_scr[...]
    for p in range(PAGES_PER_STEP):
        page = n_pages - 1 - (g * PAGES_PER_STEP + p)
        keep = (page * Q_BLOCK + lane) < q_pos
        z = jnp.dot(qbd16, k_refs[p][0].astype(BF16), preferred_element_type=F32)
        lk = jnp.where(keep, _log_keep(z), 0.0)
        res = jnp.dot(_split_hi_lo(lk), w_ref[...], preferred_element_type=F32)
        a = jnp.where(keep, jnp.exp(z + lk + res[:, :LANES] + run), 0.0)
        acc_scr[...] += lax.dot_general(v_refs[p][0].astype(BF16), a.astype(BF16), _NT,
                                        preferred_element_type=F32)
        run = run + res[:, LANES:]
    run_scr[...] = run

    @pl.when(g == n_steps - 1)
    def _():
        r = lax.broadcasted_iota(I32, acc_scr.shape, 0)
        h = lax.broadcasted_iota(I32, acc_scr.shape, 1)
        o_ref[0] = jnp.sum(jnp.where(_div_pow2(r, HEAD_DIM) == h, acc_scr[...], 0.0), axis=1, keepdims=True)


def _sb_sample(page_table, qbd, k_new, v_new_col, cache_kt, cache_vt, past_len):
    bd, n_pages = page_table.shape
    width = cache_kt.shape[1]
    wmat = _suffix_matrix(1)
    per_b = lambda shape: pl.BlockSpec((1,) + shape, lambda b, g, pt: (b,) + (0,) * len(shape))
    kv_specs = _page_specs((width, Q_BLOCK), n_pages, True)
    return pl.pallas_call(
        functools.partial(_sb_sample_kernel, past_len=past_len),
        grid_spec=pltpu.PrefetchScalarGridSpec(
            num_scalar_prefetch=1,
            grid=(bd, n_pages // PAGES_PER_STEP),
            in_specs=[per_b((N_HEADS_B, width)), per_b((1, width)), per_b((width, 1)),
                      pl.BlockSpec(wmat.shape, lambda b, g, pt: (0, 0))] + kv_specs + kv_specs,
            out_specs=per_b((width, 1)),
            scratch_shapes=[pltpu.VMEM((N_HEADS_B, LANES), F32), pltpu.VMEM((width, N_HEADS_B), F32)],
        ),
        out_shape=jax.ShapeDtypeStruct((bd, width, 1), F32),
        compiler_params=_cparams(2),
        name="sb_sample",
    )(page_table, qbd, k_new, v_new_col, wmat, *([cache_kt] * PAGES_PER_STEP), *([cache_vt] * PAGES_PER_STEP))


def _merge_kernel(x_ref, oa_ref, ob_ref, ga_ref, gb_ref, wpa, wpb, wout, ln2_ref, wr, br,
                  h_o, hn_o, comb_o):
    mixed = (ga_ref[...] * jnp.dot(oa_ref[...], wpa[...], preferred_element_type=F32)
             + gb_ref[...] * jnp.dot(ob_ref[...], wpb[...], preferred_element_type=F32))
    h = x_ref[...] + jnp.dot(mixed.astype(BF16), wout[...], preferred_element_type=F32)
    h_o[...] = h
    ms = jnp.mean(h * h, axis=-1, keepdims=True)
    hn = (h * lax.rsqrt(ms + EPS) * ln2_ref[...]).astype(BF16)
    hn_o[...] = hn

    logits = jnp.dot(hn, wr[...], preferred_element_type=F32) + br[...]
    lane = lax.broadcasted_iota(I32, logits.shape, 1)
    big = jnp.int32(ROUTER_LANES)

    def softmax_over(mask):
        m = jnp.max(jnp.where(mask, logits, NEG), axis=1, keepdims=True)
        e = jnp.where(mask, jnp.exp(logits - m), 0.0)
        return e / jnp.sum(e, axis=1, keepdims=True)

    def top1(prob, mask):
        best = jnp.max(jnp.where(mask, prob, -1.0), axis=1, keepdims=True)
        idx = jnp.min(jnp.where(mask & (prob == best), lane, big), axis=1, keepdims=True)
        return best, idx

    is_group = lane < N_GROUPS
    g_w, g_idx = top1(softmax_over(is_group), is_group)
    e_lane = lane - N_GROUPS
    in_group = (e_lane >= 0) & (e_lane < N_EXPERTS) & (_div_pow2(e_lane, EXPERTS_PER_GROUP) == g_idx)
    p_e = softmax_over(in_group)
    p1, i1 = top1(p_e, in_group)
    rest = in_group & (lane != i1)
    p2, i2 = top1(p_e, rest)
    denom = p1 + p2
    comb_o[...] = (jnp.where(lane == i1, p1 / denom * g_w, 0.0)
                   + jnp.where(lane == i2, p2 / denom * g_w, 0.0))


def _merge(x2d, o_a, o_b, g_a, g_b, wpa, wpb, wout, ln2, wr, br, tm):
    n, d = x2d.shape
    row = lambda w: pl.BlockSpec((tm, w), lambda i: (i, 0))
    consts = (wpa, wpb, wout, ln2, wr, br)
    return pl.pallas_call(
        _merge_kernel,
        grid=(n // tm,),
        in_specs=[row(d), row(WIDTH_A), row(WIDTH_B), row(d), row(d)] + [_const_spec(c.shape) for c in consts],
        out_specs=[row(d), row(d), row(ROUTER_LANES)],
        out_shape=[jax.ShapeDtypeStruct((n, d), F32), jax.ShapeDtypeStruct((n, d), BF16),
                   jax.ShapeDtypeStruct((n, ROUTER_LANES), F32)],
        compiler_params=_cparams(1),
        name="merge",
    )(x2d, o_a, o_b, g_a, g_b, *consts)


def _moe_kernel(hn_ref, comb_ref, h_ref, wgu_ref, wd_ref, lnf_ref, y_ref, acc_scr, *, final_norm):
    e = pl.program_id(1)

    @pl.when(e == 0)
    def _():
        acc_scr[...] = jnp.zeros(acc_scr.shape, F32)

    d_expert = wd_ref.shape[1]
    gu = jnp.dot(hn_ref[...], wgu_ref[0], preferred_element_type=F32)
    gate, up = gu[:, :d_expert], gu[:, d_expert:]
    comb = comb_ref[...]
    lane = lax.broadcasted_iota(I32, comb.shape, 1)
    c_e = jnp.sum(jnp.where(lane == e + N_GROUPS, comb, 0.0), axis=1, keepdims=True)
    hid = gate * _sigmoid(gate) * up * c_e
    acc_scr[...] += jnp.dot(hid.astype(BF16), wd_ref[0], preferred_element_type=F32)

    @pl.when(e == pl.num_programs(1) - 1)
    def _():
        xo = h_ref[...] + acc_scr[...]
        if final_norm:
            ms = jnp.mean(xo * xo, axis=-1, keepdims=True)
            xo = xo * lax.rsqrt(ms + EPS) * lnf_ref[...]
        y_ref[...] = xo


def _moe(hn, comb, h, wgu, wd, lnf, tm, final_norm):
    n, d = h.shape
    n_exp, _, two_f = wgu.shape
    row = lambda w: pl.BlockSpec((tm, w), lambda i, e: (i, 0))
    return pl.pallas_call(
        functools.partial(_moe_kernel, final_norm=final_norm),
        grid=(n // tm, n_exp),
        in_specs=[row(d), row(ROUTER_LANES), row(d),
                  pl.BlockSpec((1, d, two_f), lambda i, e: (e, 0, 0)),
                  pl.BlockSpec((1, two_f // 2, d), lambda i, e: (e, 0, 0)),
                  pl.BlockSpec((1, d), lambda i, e: (0, 0))],
        out_specs=row(d),
        out_shape=jax.ShapeDtypeStruct((n, d), F32),
        scratch_shapes=[pltpu.VMEM((tm, d), F32)],
        compiler_params=_cparams(2),
        name="moe",
    )(hn, comb, h, wgu, wd, lnf)


def _tile(n, pref):
    return pref if n % pref == 0 else n


def kernel(x_prompt, x_sample, cache_k_a, cache_v_a, cache_kidx_a, cache_k_b, cache_v_b, page_table,
           rel_bias_table, ln1_g, w_in, w_proj_a, w_proj_b, w_out, ln2_g, w_router_group, b_router_group,
           w_router_expert, b_router_expert, w_gate, w_up, w_down, ln_f_g):
    bsz, seq, d = x_prompt.shape
    bd, t_new, _ = x_sample.shape
    assert t_new == 1, "the sample group decodes one token per sequence"
    depth = w_in.shape[0]
    pool = cache_k_a.shape[1]
    n_pages = page_table.shape[1]
    past_len = n_pages * Q_BLOCK
    assert n_pages % PAGES_PER_STEP == 0 and seq % Q_BLOCK == 0
    n_p, n_s = bsz * seq, bd * t_new
    d_expert = w_gate.shape[-1]

    xp, xs = x_prompt.reshape(n_p, d), x_sample.reshape(n_s, d)
    new_p, new_s = [], []
    for l in range(depth):
        w_pieces = _split_w_in(w_in[l])
        wpa, wpb, wout = (w.astype(BF16) for w in (w_proj_a[l], w_proj_b[l], w_out[l]))
        ln2 = ln2_g[l].reshape(1, d)
        wr = jnp.concatenate([w_router_group[l], jnp.moveaxis(w_router_expert[l], 0, 1).reshape(d, N_EXPERTS),
                              jnp.zeros((d, ROUTER_LANES - N_GROUPS - N_EXPERTS), F32)], axis=1).astype(BF16)
        br = jnp.concatenate([b_router_group[l], b_router_expert[l].reshape(N_EXPERTS),
                              jnp.zeros((ROUTER_LANES - N_GROUPS - N_EXPERTS,), F32)]).reshape(1, ROUTER_LANES)
        wgu = jnp.concatenate([w_gate[l], w_up[l]], axis=-1).reshape(N_EXPERTS, d, 2 * d_expert).astype(BF16)
        wd = w_down[l].reshape(N_EXPERTS, d_expert, d).astype(BF16)
        lnf = ln_f_g.reshape(1, d)

        def ffn(x2d, o_a, o_b, g_a, g_b, tm_merge, tm_moe):
            h, hn, comb = _merge(x2d, o_a, o_b, g_a, g_b, wpa, wpb, wout, ln2, wr, br, tm_merge)
            return _moe(hn, comb, h, wgu, wd, lnf, tm_moe, final_norm=(l == depth - 1))

        (qa, ka, va, kad, vad, qi, ki, kid, wi, qb, kb, vb, kb16, vb16, ga, gb) = _project(
            xp, ln1_g[l], w_pieces, _tile(n_p, 512))
        r3 = lambda a: a.reshape(bsz, seq, a.shape[-1])
        n_top = min(TOPK_MAX, seq // 4)
        wit = jnp.swapaxes(r3(wi), 1, 2)
        vadt = jnp.swapaxes(vad.reshape(bsz, seq // Q_BLOCK, Q_BLOCK, 2 * LANES), 2, 3)
        o_at = _dsa_prompt(rel_bias_table, r3(qi), wit, r3(kid), r3(qa), r3(kad), vadt, n_top)
        o_a = jnp.swapaxes(o_at, 1, 2)
        o_b = _sb_prompt(r3(qb), r3(kb16), r3(vb16))
        xp = ffn(xp, o_a.reshape(n_p, WIDTH_A), o_b.reshape(n_p, WIDTH_B), ga, gb, _tile(n_p, 512), _tile(n_p, 1024))
        new_p.append((ka, va, ki, kb, vb))

        (qa, ka, va, _, _, qi, ki, _, wi, qb, kb, vb, _, _, ga, gb) = _project(xs, ln1_g[l], w_pieces, n_s)
        n_top = min(TOPK_MAX, (past_len + t_new) // 4)
        qi3 = qi.astype(F32).reshape(bd, N_IDX_HEADS, IDX_DIM)
        wi3 = wi.reshape(bd, N_IDX_HEADS, 1)
        paged_t = lambda c: jnp.moveaxis(c, 1, -1).reshape(pool, -1, Q_BLOCK)
        scores = _idx_sample(page_table, qi3, wi3, paged_t(cache_kidx_a[l]))
        madd, maddn = _select_sample(scores.reshape(bd, n_pages, LANES), qi3, wi3, ki.reshape(bd, 1, IDX_DIM), n_top)
        qa4 = qa.astype(F32).reshape(bd, N_HEADS_A, 1, HEAD_DIM) * (HEAD_DIM ** -0.5)
        kv_of_head = (np.arange(N_HEADS_A)[:, None] // GROUP_A) == np.arange(N_KV_A)[None, :]
        qz = jnp.where(kv_of_head[None, :, :, None], qa4, 0.0).reshape(bd, N_HEADS_A, N_KV_A * HEAD_DIM)
        logits = _dsa_logits(page_table, rel_bias_table, qz, madd.reshape(bd, n_pages, 1, LANES),
                             paged_t(cache_k_a[l]), past_len)
        o_a = _dsa_pv(page_table, rel_bias_table, logits, qz, maddn, ka.reshape(bd, 1, -1), va.reshape(bd, -1, 1),
                      paged_t(cache_v_a[l]))
        o_a = jnp.swapaxes(o_a, 1, 2)
        qb4 = qb.astype(F32).reshape(bd, N_HEADS_B, 1, HEAD_DIM) * (HEAD_DIM ** -0.5)
        own = np.eye(N_HEADS_B, dtype=bool)
        qbd = jnp.where(own[None, :, :, None], qb4, 0.0).reshape(bd, N_HEADS_B, WIDTH_B)
        o_b = _sb_sample(page_table, qbd, kb.reshape(bd, 1, -1), vb.reshape(bd, -1, 1),
                         paged_t(cache_k_b[l]), paged_t(cache_v_b[l]), past_len)
        xs = ffn(xs, o_a.reshape(n_s, WIDTH_A).astype(BF16), o_b.reshape(n_s, WIDTH_B).astype(BF16), ga, gb, n_s, n_s)
        new_s.append((ka, va, ki, kb, vb))

    def stack(rows, idx, lead, tail):
        return jnp.stack([r[idx].reshape(lead + tail) for r in rows])

    outs = [xp.reshape(bsz, seq, d), xs.reshape(bd, t_new, d)]
    tails = ((N_KV_A, HEAD_DIM), (N_KV_A, HEAD_DIM), (IDX_DIM,), (N_HEADS_B, HEAD_DIM), (N_HEADS_B, HEAD_DIM))
    for rows, lead in ((new_p, (bsz, seq)), (new_s, (bd, t_new))):
        for idx, tail in enumerate(tails):
            outs.append(stack(rows, idx, lead, tail))
    return tuple(outs)
```

```python
import functools
import math

import jax
import jax.numpy as jnp
import numpy as np
from jax import lax
from jax.experimental import pallas as pl
from jax.experimental.pallas import tpu as pltpu

F32 = jnp.float32
BF16 = jnp.bfloat16
I32 = jnp.int32

HEAD_DIM = 64
N_HEADS_A = 8
N_KV_A = 2
GROUP_A = N_HEADS_A // N_KV_A
N_IDX_HEADS = 8
IDX_DIM = 64
N_HEADS_B = 8
WIDTH_A = N_HEADS_A * HEAD_DIM
WIDTH_B = N_HEADS_B * HEAD_DIM
TOPK_MAX = 256
N_BUCKETS = 32
MAX_DISTANCE = 128
N_GROUPS = 4
EXPERTS_PER_GROUP = 8
N_EXPERTS = N_GROUPS * EXPERTS_PER_GROUP
TOP_E = 2
Q_BLOCK = 128
EPS = 1e-6

Q_TILE = 256

LANES = 128
VMEM_LIMIT_BYTES = 56 * 1024 * 1024

SMALL_PAGES = 32
SB_PAGES = 16
SELECT_SEQS = 4
NEG = -1e30
INT_MIN = -(2 ** 31)
ROUTER_LANES = LANES

_NT = (((1,), (1,)), ((), ()))


def _cparams(n_axes):
    return pltpu.CompilerParams(dimension_semantics=("arbitrary",) * n_axes,
                                vmem_limit_bytes=VMEM_LIMIT_BYTES)


def _const_spec(shape):
    nd = len(shape)
    return pl.BlockSpec(shape, lambda *_: (0,) * nd)


def _div_pow2(x, n):
    assert n & (n - 1) == 0
    return x >> (n.bit_length() - 1)


def _sigmoid(x):
    return 1.0 / (1.0 + jnp.exp(-x))


def _half_masks(dtype):
    lane = lax.broadcasted_iota(I32, (LANES, LANES), 1)
    lo = jnp.where(lane < HEAD_DIM, 1.0, 0.0).astype(dtype)
    hi = jnp.where(lane < HEAD_DIM, 0.0, 1.0).astype(dtype)
    return lo, hi


def _pair_block_diag(blk, lo, hi):
    return jnp.concatenate([blk * lo, blk * hi], axis=0)


def _row_half_masks(dtype):
    r = lax.broadcasted_iota(I32, (LANES, LANES), 0)
    top = jnp.where(r < HEAD_DIM, 1.0, 0.0).astype(dtype)
    bottom = jnp.where(r < HEAD_DIM, 0.0, 1.0).astype(dtype)
    return top, bottom


def _pair_block_diag_t(blk_t, top, bottom):
    return jnp.concatenate([blk_t * top, blk_t * bottom], axis=1)


def _col_to_row(x):
    n = x.shape[0]
    eye = lax.broadcasted_iota(I32, (n, n), 0) == lax.broadcasted_iota(I32, (n, n), 1)
    return jnp.sum(jnp.where(eye, x, 0.0), axis=0, keepdims=True)


def _order_key(score):
    score = jnp.where(score == 0.0, 0.0, score)
    b = lax.bitcast_convert_type(score, I32)
    return b ^ ((b >> 31) & jnp.int32(0x7FFFFFFF))


def _t5_bucket(dist):
    n = jnp.maximum(dist, 0)
    max_exact = N_BUCKETS // 2
    nf = jnp.maximum(n, max_exact).astype(F32)
    large = max_exact + (jnp.log(nf / max_exact) / math.log(MAX_DISTANCE / max_exact)
                         * (N_BUCKETS - max_exact)).astype(I32)
    return jnp.where(n < max_exact, n, jnp.minimum(large, N_BUCKETS - 1))


def _proj_kernel(x_ref, g_ref, wqa, wkv, wkvd, wqi, wkiw, wqb, wkb, wvb, wga, wgb,
                 qa_o, ka_o, va_o, kad_o, vad_o, qi_o, ki_o, kid_o, wi_o,
                 qb_o, kb_o, vb_o, kb16_o, vb16_o, ga_o, gb_o):
    x = x_ref[...]
    ms = jnp.mean(x * x, axis=-1, keepdims=True)
    xn = (x * lax.rsqrt(ms + EPS) * g_ref[...]).astype(BF16)

    def mm(w):
        return jnp.dot(xn, w[...], preferred_element_type=F32)

    qa_o[...] = mm(wqa).astype(BF16)
    kv = mm(wkv)
    ka_o[...] = kv[:, :N_KV_A * HEAD_DIM]
    va_o[...] = kv[:, N_KV_A * HEAD_DIM:]
    kvd = mm(wkvd).astype(BF16)
    kad_o[...] = kvd[:, :2 * LANES]
    vad_o[...] = kvd[:, 2 * LANES:]
    qi_o[...] = mm(wqi).astype(BF16)
    kiw = mm(wkiw)
    ki_o[...] = kiw[:, :IDX_DIM]
    kid_o[...] = kiw[:, :LANES].astype(BF16)
    wi_o[...] = kiw[:, LANES:LANES + N_IDX_HEADS] * (N_IDX_HEADS ** -0.5)
    qb_o[...] = mm(wqb).astype(BF16)
    kb = mm(wkb)
    kb_o[...] = kb
    kb16_o[...] = kb.astype(BF16)
    vb = mm(wvb)
    vb_o[...] = vb
    vb16_o[...] = vb.astype(BF16)
    ga_o[...] = _sigmoid(mm(wga))
    gb_o[...] = _sigmoid(mm(wgb))


def _split_w_in(w_in):
    d_model = w_in.shape[0]
    widths = (WIDTH_A, N_KV_A * HEAD_DIM, N_KV_A * HEAD_DIM, N_IDX_HEADS * IDX_DIM, N_IDX_HEADS, IDX_DIM,
              WIDTH_B, WIDTH_B, WIDTH_B, d_model, d_model)
    points = [int(p) for p in np.cumsum(widths)[:-1]]
    qa, ka, va, qi, wi, ki, qb, kb, vb, ga, gb = jnp.split(w_in, points, axis=1)
    dup = lambda a: jnp.concatenate([a[:, :HEAD_DIM], a[:, :HEAD_DIM], a[:, HEAD_DIM:], a[:, HEAD_DIM:]], axis=1)
    wkv = jnp.concatenate([ka, va], axis=1)
    wkvd = jnp.concatenate([dup(ka), dup(va)], axis=1)
    wkiw = jnp.concatenate([ki, ki, wi, jnp.zeros((d_model, LANES - N_IDX_HEADS), w_in.dtype)], axis=1)
    return tuple(a.astype(BF16) for a in (qa, wkv, wkvd, qi, wkiw, qb, kb, vb, ga, gb))


def _project(x2d, ln_g, w_pieces, tm):
    n, d = x2d.shape
    assert n % tm == 0
    out_defs = [
        (WIDTH_A, BF16), (N_KV_A * HEAD_DIM, F32), (N_KV_A * HEAD_DIM, F32), (2 * LANES, BF16), (2 * LANES, BF16),
        (N_IDX_HEADS * IDX_DIM, BF16), (IDX_DIM, F32), (LANES, BF16), (N_IDX_HEADS, F32),
        (WIDTH_B, BF16), (WIDTH_B, F32), (WIDTH_B, F32), (WIDTH_B, BF16), (WIDTH_B, BF16), (d, F32), (d, F32)]
    row = lambda i: (i, 0)
    return pl.pallas_call(
        _proj_kernel,
        grid=(n // tm,),
        in_specs=[pl.BlockSpec((tm, d), row), _const_spec((1, d))] + [_const_spec(w.shape) for w in w_pieces],
        out_specs=[pl.BlockSpec((tm, w), row) for w, _ in out_defs],
        out_shape=[jax.ShapeDtypeStruct((n, w), dt) for w, dt in out_defs],
        compiler_params=_cparams(1),
        name="proj",
    )(x2d, ln_g.reshape(1, d), *w_pieces)


def _dsa_prompt_kernel(tab_ref, qi_ref, wit_ref, kid_ref, qa_ref, kad_ref, vadt_ref, tril_ref, o_ref,
                       key_scr, lg_scr, bias_scr, acc_scr, *, n_top):
    b = pl.program_id(0)
    i = pl.program_id(1)
    kb, qt = Q_BLOCK, Q_TILE
    n_blk = (i + 1) * (qt // kb)
    row = lax.broadcasted_iota(I32, (kb, qt), 0)
    col = lax.broadcasted_iota(I32, (kb, qt), 1)
    lo, hi = _half_masks(BF16)
    top, bottom = _row_half_masks(BF16)
    d_min = 1 - qt // kb

    def block_dist(j):
        return i * (qt // kb) - j

    @pl.when((b == 0) & (i == 0))
    def _():
        for d in range(d_min, 3):
            bucket = _t5_bucket(d * kb + col - row)
            for h in range(N_HEADS_A):
                def sel(bk, acc):
                    return jnp.where(bucket == bk, tab_ref[bk, h], acc)
                tile = lax.fori_loop(0, N_BUCKETS, sel, jnp.zeros((kb, qt), F32))
                bias_scr[d - d_min, h // 2, (h % 2) * kb:(h % 2 + 1) * kb, :] = tile

    def causal(j):
        return block_dist(j) * kb + col - row >= 0

    w = wit_ref[0] * (IDX_DIM ** -0.5)
    wrows = [w[h:h + 1, :] for h in range(N_IDX_HEADS)]

    def score_body(j, carry):
        kblk = kid_ref[0, pl.ds(pl.multiple_of(j * kb, kb), kb), :]
        kbd = _pair_block_diag(kblk, lo, hi)
        acc = jnp.zeros((kb, qt), F32)
        for p in range(N_IDX_HEADS // 2):
            r = lax.dot_general(kbd, qi_ref[0, :, p * LANES:(p + 1) * LANES], _NT,
                                preferred_element_type=F32)
            acc = acc + jnp.maximum(r[:kb], 0.0) * wrows[2 * p] + jnp.maximum(r[kb:], 0.0) * wrows[2 * p + 1]
        key_scr[j] = _order_key(jnp.where(causal(j), acc, -jnp.inf))
        return carry

    lax.fori_loop(0, n_blk, score_body, 0)

    def count(pred):
        def body(j, acc):
            return acc + jnp.where(pred(key_scr[j]), 1, 0)
        acc = lax.fori_loop(0, n_blk, body, jnp.zeros((kb, qt), I32))
        return jnp.sum(acc, axis=0, keepdims=True)

    thr = jnp.where(count(lambda k: k >= 0) >= n_top, 0, INT_MIN).astype(I32)

    def bit_body(t, thr):
        cand = thr | lax.shift_left(jnp.int32(1), jnp.int32(30) - t)
        return jnp.where(count(lambda k: k >= cand) >= n_top, cand, thr)

    thr = lax.fori_loop(0, 31, bit_body, thr)
    need = (n_top - count(lambda k: k > thr)).astype(F32)

    tril = tril_ref[...]

    n_pairs = N_HEADS_A // 2
    q_ps = [(qa_ref[0, :, p * LANES:(p + 1) * LANES].astype(F32) * (HEAD_DIM ** -0.5)).astype(BF16)
            for p in range(n_pairs)]

    def logits_body(j, carry):
        ties_before, maxes = carry[0], carry[1:]
        key = key_scr[j]
        eq = key == thr
        pre = jnp.dot(tril, jnp.where(eq, 1.0, 0.0).astype(BF16), preferred_element_type=F32)
        take = (key > thr) | (eq & (ties_before + pre <= need))
        madd = jnp.where(take & causal(j), 0.0, NEG)
        rows = pl.ds(pl.multiple_of(j * kb, kb), kb)
        dsel = jnp.minimum(block_dist(j), 2) - d_min
        out = [ties_before + pre[kb - 1:kb, :]]
        for p in range(n_pairs):
            c = (2 * p) // GROUP_A
            kbd = _pair_block_diag(kad_ref[0, rows, c * LANES:(c + 1) * LANES], lo, hi)
            lg = lax.dot_general(kbd, q_ps[p], _NT, preferred_element_type=F32) + bias_scr[dsel, p]
            for half in range(2):
                lgh = lg[half * kb:(half + 1) * kb] + madd
                lg_scr[j, p, half * kb:(half + 1) * kb, :] = lgh
                out.append(jnp.maximum(maxes[2 * p + half], jnp.max(lgh, axis=0, keepdims=True)))
        return tuple(out)

    neg = jnp.full((1, qt), NEG, F32)
    zero = jnp.zeros((1, qt), F32)
    maxes = lax.fori_loop(0, n_blk, logits_body, (zero,) + (neg,) * N_HEADS_A)[1:]
    acc_scr[...] = jnp.zeros(acc_scr.shape, F32)

    def pv_body(j, sums):
        out = []
        for p in range(n_pairs):
            c = (2 * p) // GROUP_A
            vbd = _pair_block_diag_t(vadt_ref[0, j, c * LANES:(c + 1) * LANES, :], top, bottom)
            p0 = jnp.exp(lg_scr[j, p, :kb, :] - maxes[2 * p])
            p1 = jnp.exp(lg_scr[j, p, kb:, :] - maxes[2 * p + 1])
            out += [sums[2 * p] + jnp.sum(p0, axis=0, keepdims=True),
                    sums[2 * p + 1] + jnp.sum(p1, axis=0, keepdims=True)]
            acc_scr[p] += jnp.dot(vbd, jnp.concatenate([p0, p1], axis=0).astype(BF16),
                                  preferred_element_type=F32)
        return tuple(out)

    sums = lax.fori_loop(0, n_blk, pv_body, (zero,) * N_HEADS_A)
    for p in range(n_pairs):
        denom = jnp.where(row < HEAD_DIM, sums[2 * p], sums[2 * p + 1])
        o_ref[0, p * LANES:(p + 1) * LANES, :] = (acc_scr[p] / denom).astype(o_ref.dtype)


def _dsa_prompt(bias_table, qi, wit, kid, qa, kad, vadt, n_top):
    bsz, seq, _ = qi.shape
    assert seq % Q_TILE == 0 and Q_TILE % Q_BLOCK == 0
    nkb = seq // Q_BLOCK
    n_bias = Q_TILE // Q_BLOCK + 2
    tril = (np.arange(Q_BLOCK)[:, None] >= np.arange(Q_BLOCK)[None, :]).astype(np.float32)
    qblk = lambda w: pl.BlockSpec((1, Q_TILE, w), lambda b, i: (b, i, 0))
    full = lambda w: pl.BlockSpec((1, seq, w), lambda b, i: (b, 0, 0))
    return pl.pallas_call(
        functools.partial(_dsa_prompt_kernel, n_top=n_top),
        grid=(bsz, seq // Q_TILE),
        in_specs=[pl.BlockSpec(memory_space=pltpu.SMEM), qblk(N_IDX_HEADS * IDX_DIM),
                  pl.BlockSpec((1, N_IDX_HEADS, Q_TILE), lambda b, i: (b, 0, i)), full(LANES),
                  qblk(WIDTH_A), full(2 * LANES),
                  pl.BlockSpec((1, nkb, 2 * LANES, Q_BLOCK), lambda b, i: (b, 0, 0, 0)),
                  _const_spec((Q_BLOCK, Q_BLOCK))],
        out_specs=pl.BlockSpec((1, WIDTH_A, Q_TILE), lambda b, i: (b, 0, i)),
        out_shape=jax.ShapeDtypeStruct((bsz, WIDTH_A, seq), BF16),
        scratch_shapes=[pltpu.VMEM((nkb, Q_BLOCK, Q_TILE), I32),
                        pltpu.VMEM((nkb, N_HEADS_A // 2, 2 * Q_BLOCK, Q_TILE), F32),
                        pltpu.VMEM((n_bias, N_HEADS_A // 2, 2 * Q_BLOCK, Q_TILE), F32),
                        pltpu.VMEM((N_HEADS_A // 2, LANES, Q_TILE), F32)],
        compiler_params=_cparams(2),
        name="dsa_prompt",
    )(bias_table, qi, wit, kid, qa, kad, vadt, jnp.asarray(tril, BF16))


def _suffix_matrix(n_heads):
    n = n_heads * Q_BLOCK
    r = np.arange(n)
    same = (r[:, None] // Q_BLOCK) == (r[None, :] // Q_BLOCK)
    u = same & (r[:, None] > r[None, :])
    w = np.concatenate([u, same], axis=1).astype(np.float32)
    return jnp.asarray(np.concatenate([w, w], axis=0), BF16)


def _log_keep(z):
    return -(jnp.maximum(z, 0.0) + jnp.log(1.0 + jnp.exp(-jnp.abs(z))))


def _split_hi_lo(x):
    hi = x.astype(BF16)
    lo = (x - hi.astype(F32)).astype(BF16)
    return jnp.concatenate([hi, lo], axis=1)


def _sb_prompt_kernel(qb_ref, kb_ref, vb_ref, w_ref, o_ref, run_scr, acc_scr):
    i = pl.program_id(1)
    kb, qt = Q_BLOCK, Q_TILE
    bpt = qt // kb
    n_pairs = N_HEADS_B // 2
    row = lax.broadcasted_iota(I32, (qt, 2 * kb), 0)
    col = lax.broadcasted_iota(I32, (qt, 2 * kb), 1) & (kb - 1)
    lo, hi = _half_masks(BF16)
    q_ps = [(qb_ref[0, :, p * LANES:(p + 1) * LANES].astype(F32) * (HEAD_DIM ** -0.5)).astype(BF16)
            for p in range(n_pairs)]

    def block(j, edge, first):
        rows = pl.ds(pl.multiple_of(j * kb, kb), kb)
        if edge:
            strict = j * kb + col < i * qt + row
        for p in range(n_pairs):
            kbd = _pair_block_diag(kb_ref[0, rows, p * LANES:(p + 1) * LANES], lo, hi)
            vbd = _pair_block_diag(vb_ref[0, rows, p * LANES:(p + 1) * LANES], lo, hi)
            z = lax.dot_general(q_ps[p], kbd, _NT, preferred_element_type=F32)
            lk = _log_keep(z)
            if edge:
                lk = jnp.where(strict, lk, 0.0)
            res = jnp.dot(_split_hi_lo(lk), w_ref[...], preferred_element_type=F32)
            x = z + lk + res[:, :2 * kb]
            if not first:
                x = x + run_scr[p]
            a = jnp.exp(x)
            if edge:
                a = jnp.where(strict, a, 0.0)
            pv = jnp.dot(a.astype(BF16), vbd, preferred_element_type=F32)
            if first:
                run_scr[p] = res[:, 2 * kb:]
                acc_scr[p] = pv
            else:
                run_scr[p] += res[:, 2 * kb:]
                acc_scr[p] += pv

    for e in range(bpt):
        block((i + 1) * bpt - 1 - e, True, e == 0)

    def body(t, carry):
        block(i * bpt - 1 - t, False, False)
        return carry

    lax.fori_loop(0, i * bpt, body, 0)
    for p in range(n_pairs):
        o_ref[0, :, p * LANES:(p + 1) * LANES] = acc_scr[p].astype(o_ref.dtype)


def _sb_prompt(qb, kb16, vb16):
    bsz, seq, width = qb.shape
    assert seq % Q_TILE == 0 and Q_TILE % Q_BLOCK == 0
    wmat = _suffix_matrix(2)
    qblk = pl.BlockSpec((1, Q_TILE, width), lambda b, i: (b, i, 0))
    full = pl.BlockSpec((1, seq, width), lambda b, i: (b, 0, 0))
    return pl.pallas_call(
        _sb_prompt_kernel,
        grid=(bsz, seq // Q_TILE),
        in_specs=[qblk, full, full, _const_spec(wmat.shape)],
        out_specs=qblk,
        out_shape=jax.ShapeDtypeStruct((bsz, seq, width), BF16),
        scratch_shapes=[pltpu.VMEM((N_HEADS_B // 2, Q_TILE, 2 * Q_BLOCK), F32),
                        pltpu.VMEM((N_HEADS_B // 2, Q_TILE, LANES), F32)],
        compiler_params=_cparams(2),
        name="sb_prompt",
    )(qb, kb16, vb16, wmat)


def _page_specs(block_tail, n_pages, reverse, pages=None):
    pages = pages or SMALL_PAGES

    def spec(p):
        def index_map(b, g, pt):
            page = g * pages + p
            if reverse:
                page = n_pages - 1 - page
            return (pt[b, page],) + (0,) * len(block_tail)
        return pl.BlockSpec((1,) + block_tail, index_map)
    return [spec(p) for p in range(pages)]


def _idx_sample_kernel(pt_ref, q_ref, w_ref, *refs):
    k_refs, o_ref = refs[:SMALL_PAGES], refs[SMALL_PAGES]
    q = q_ref[0].astype(BF16)
    w = w_ref[0] * (IDX_DIM ** -0.5)
    for p in range(SMALL_PAGES):
        r = jnp.dot(q, k_refs[p][0].astype(BF16), preferred_element_type=F32)
        o_ref[0, p] = jnp.sum(jnp.maximum(r, 0.0) * w, axis=0, keepdims=True)


def _idx_sample(page_table, qi3, wi3, cache_kidx):
    bd, n_pages = page_table.shape
    per_b = lambda shape: pl.BlockSpec((1,) + shape, lambda b, g, pt: (b,) + (0,) * len(shape))
    return pl.pallas_call(
        _idx_sample_kernel,
        grid_spec=pltpu.PrefetchScalarGridSpec(
            num_scalar_prefetch=1,
            grid=(bd, n_pages // SMALL_PAGES),
            in_specs=[per_b((N_IDX_HEADS, IDX_DIM)), per_b((N_IDX_HEADS, 1))]
            + _page_specs((IDX_DIM, Q_BLOCK), n_pages, False),
            out_specs=pl.BlockSpec((1, SMALL_PAGES, 1, LANES), lambda b, g, pt: (b, g, 0, 0)),
        ),
        out_shape=jax.ShapeDtypeStruct((bd, n_pages, 1, LANES), F32),
        compiler_params=_cparams(2),
        name="idx_sample",
    )(page_table, qi3, wi3, *([cache_kidx] * SMALL_PAGES))


def _select_sample_kernel(sc_ref, q_ref, w_ref, k_ref, tri_ref, low_ref, madd_o, maddn_o, *, n_top):
    n_seq = sc_ref.shape[0]
    seqs = range(n_seq)

    def total(x):
        return jnp.sum(jnp.sum(x, axis=1, keepdims=True), axis=0, keepdims=True)

    keys, keys_new = [], []
    for s in seqs:
        keys.append(_order_key(sc_ref[s]))
        r_new = jnp.sum(q_ref[s] * k_ref[s], axis=1, keepdims=True)
        s_new = jnp.sum(jnp.maximum(r_new * (IDX_DIM ** -0.5), 0.0) * w_ref[s], axis=0, keepdims=True)
        keys_new.append(_order_key(s_new))

    def count(s, pred):
        return total(jnp.where(pred(keys[s]), 1, 0)) + jnp.where(pred(keys_new[s]), 1, 0)

    thrs = tuple(jnp.where(count(s, lambda k: k >= 0) >= n_top, 0, INT_MIN).astype(I32) for s in seqs)

    def bit_body(t, thrs):
        bit = lax.shift_left(jnp.int32(1), jnp.int32(30) - t)
        cands = [thr | bit for thr in thrs]
        return tuple(jnp.where(count(s, lambda k: k >= cands[s]) >= n_top, cands[s], thrs[s]) for s in seqs)

    thrs = lax.fori_loop(0, 31, bit_body, thrs)
    for s in seqs:
        key, key_new, thr = keys[s], keys_new[s], thrs[s]
        need = (n_top - count(s, lambda k: k > thr)).astype(F32)
        eq = jnp.where(key == thr, 1.0, 0.0).astype(BF16)
        pre = jnp.dot(eq, tri_ref[...], preferred_element_type=F32)
        tot = jnp.broadcast_to(pre[:, LANES - 1:LANES], pre.shape).astype(BF16)
        before = jnp.dot(low_ref[...], tot, preferred_element_type=F32)
        take = (key > thr) | ((key == thr) & (before + pre <= need))
        madd_o[s] = jnp.where(take, 0.0, NEG)
        ties_past = total(jnp.where(key == thr, 1.0, 0.0))
        take_new = (key_new > thr) | ((key_new == thr) & (ties_past + 1.0 <= need))
        maddn_o[s] = jnp.where(take_new, 0.0, NEG)


def _select_sample(scores, qi3, wi3, ki3, n_top):
    bd, n_pages, _ = scores.shape
    n_seq = math.gcd(bd, SELECT_SEQS)
    tri = jnp.asarray((np.arange(LANES)[:, None] <= np.arange(LANES)[None, :]).astype(np.float32), BF16)
    low = jnp.asarray((np.arange(n_pages)[:, None] > np.arange(n_pages)[None, :]).astype(np.float32), BF16)
    per_b = lambda shape: pl.BlockSpec((n_seq,) + shape, lambda b: (b,) + (0,) * len(shape))
    return pl.pallas_call(
        functools.partial(_select_sample_kernel, n_top=n_top),
        grid=(bd // n_seq,),
        in_specs=[per_b((n_pages, LANES)), per_b((N_IDX_HEADS, IDX_DIM)), per_b((N_IDX_HEADS, 1)),
                  per_b((1, IDX_DIM)), _const_spec(tri.shape), _const_spec(low.shape)],
        out_specs=[per_b((n_pages, LANES)), per_b((1, 1))],
        out_shape=[jax.ShapeDtypeStruct((bd, n_pages, LANES), F32), jax.ShapeDtypeStruct((bd, 1, 1), F32)],
        compiler_params=_cparams(1),
        name="select_sample",
    )(scores, qi3, wi3, ki3, tri, low)


def _sample_bias(tabt, dist):
    bucket = _t5_bucket(dist)
    out = jnp.zeros((tabt.shape[0], dist.shape[1]), F32)
    for bk in range(N_BUCKETS):
        out = jnp.where(bucket == bk, tabt[:, bk:bk + 1], out)
    return out


def _dsa_logits_kernel(pt_ref, tabt_ref, qz_ref, madd_ref, *refs, past_len):
    k_refs, o_ref = refs[:SMALL_PAGES], refs[SMALL_PAGES]
    g = pl.program_id(1)
    qz16 = qz_ref[0].astype(BF16)
    lane = lax.broadcasted_iota(I32, (1, LANES), 1)
    tabt = tabt_ref[...]
    for p in range(SMALL_PAGES):
        page = g * SMALL_PAGES + p
        lg = jnp.dot(qz16, k_refs[p][0].astype(BF16), preferred_element_type=F32)
        o_ref[0, p] = lg + _sample_bias(tabt, past_len - (page * Q_BLOCK + lane)) + madd_ref[0, p]


def _dsa_logits(page_table, bias_table, qz, madd, cache_kt, past_len):
    bd, n_pages = page_table.shape
    per_b = lambda shape: pl.BlockSpec((1,) + shape, lambda b, g, pt: (b,) + (0,) * len(shape))
    return pl.pallas_call(
        functools.partial(_dsa_logits_kernel, past_len=past_len),
        grid_spec=pltpu.PrefetchScalarGridSpec(
            num_scalar_prefetch=1,
            grid=(bd, n_pages // SMALL_PAGES),
            in_specs=[pl.BlockSpec((N_HEADS_A, N_BUCKETS), lambda b, g, pt: (0, 0)),
                      per_b((N_HEADS_A, LANES)),
                      pl.BlockSpec((1, SMALL_PAGES, 1, LANES), lambda b, g, pt: (b, g, 0, 0))]
            + _page_specs((N_KV_A * HEAD_DIM, Q_BLOCK), n_pages, False),
            out_specs=pl.BlockSpec((1, SMALL_PAGES, N_HEADS_A, LANES), lambda b, g, pt: (b, g, 0, 0)),
        ),
        out_shape=jax.ShapeDtypeStruct((bd, n_pages, N_HEADS_A, LANES), F32),
        compiler_params=_cparams(2),
        name="dsa_logits",
    )(page_table, bias_table.T, qz, madd, *([cache_kt] * SMALL_PAGES))


def _dsa_pv_kernel(pt_ref, tabt_ref, lg_ref, qz_ref, maddn_ref, kn_ref, vnc_ref, *refs):
    v_refs = refs[:SMALL_PAGES]
    o_ref, m_scr, lsum_scr, acc_scr = refs[SMALL_PAGES:]
    g = pl.program_id(1)

    def new_logit():
        lg = jnp.sum(qz_ref[0] * kn_ref[0], axis=1, keepdims=True)
        return lg + _sample_bias(tabt_ref[...], jnp.zeros((1, 1), I32)) + maddn_ref[0]

    @pl.when(g == 0)
    def _():
        m_past = jnp.max(jnp.max(lg_ref[0], axis=0), axis=1, keepdims=True)
        m_scr[...] = jnp.maximum(m_past, new_logit())
        lsum_scr[...] = jnp.zeros(lsum_scr.shape, F32)
        acc_scr[...] = jnp.zeros(acc_scr.shape, F32)

    m = m_scr[...]
    for p in range(SMALL_PAGES):
        pr = jnp.exp(lg_ref[0, g * SMALL_PAGES + p] - m)
        lsum_scr[...] += pr
        acc_scr[...] += lax.dot_general(v_refs[p][0].astype(BF16), pr.astype(BF16), _NT,
                                        preferred_element_type=F32)

    @pl.when(g == pl.num_programs(1) - 1)
    def _():
        p_new = jnp.exp(new_logit() - m)
        denom = jnp.sum(lsum_scr[...], axis=1, keepdims=True) + p_new
        out = (acc_scr[...] + vnc_ref[0] * _col_to_row(p_new)) / _col_to_row(denom)
        r = lax.broadcasted_iota(I32, out.shape, 0)
        h = lax.broadcasted_iota(I32, out.shape, 1)
        out = jnp.where(_div_pow2(r, HEAD_DIM) == _div_pow2(h, GROUP_A), out, 0.0)
        o_ref[0] = out[:HEAD_DIM] + out[HEAD_DIM:]


def _dsa_pv(page_table, bias_table, logits, qz, maddn, k_new, v_new_col, cache_vt):
    bd, n_pages = page_table.shape
    per_b = lambda shape: pl.BlockSpec((1,) + shape, lambda b, g, pt: (b,) + (0,) * len(shape))
    kv = N_KV_A * HEAD_DIM
    return pl.pallas_call(
        _dsa_pv_kernel,
        grid_spec=pltpu.PrefetchScalarGridSpec(
            num_scalar_prefetch=1,
            grid=(bd, n_pages // SMALL_PAGES),
            in_specs=[pl.BlockSpec((N_HEADS_A, N_BUCKETS), lambda b, g, pt: (0, 0)),
                      per_b((n_pages, N_HEADS_A, LANES)), per_b((N_HEADS_A, kv)), per_b((1, 1)),
                      per_b((1, kv)), per_b((kv, 1))] + _page_specs((kv, Q_BLOCK), n_pages, False),
            out_specs=per_b((HEAD_DIM, N_HEADS_A)),
            scratch_shapes=[pltpu.VMEM((N_HEADS_A, 1), F32), pltpu.VMEM((N_HEADS_A, LANES), F32),
                            pltpu.VMEM((kv, N_HEADS_A), F32)],
        ),
        out_shape=jax.ShapeDtypeStruct((bd, HEAD_DIM, N_HEADS_A), F32),
        compiler_params=_cparams(2),
        name="dsa_pv",
    )(page_table, bias_table.T, logits, qz, maddn, k_new, v_new_col, *([cache_vt] * SMALL_PAGES))


def _sb_sample_kernel(pt_ref, qbd_ref, kn_ref, vnc_ref, w_ref, *refs, past_len):
    k_refs = refs[:SB_PAGES]
    v_refs = refs[SB_PAGES:2 * SB_PAGES]
    o_ref, run_scr, acc_scr = refs[2 * SB_PAGES:]
    g = pl.program_id(1)
    n_steps = pl.num_programs(1)
    n_pages = n_steps * SB_PAGES
    qbd = qbd_ref[0]
    qbd16 = qbd.astype(BF16)
    lane = lax.broadcasted_iota(I32, (1, LANES), 1)
    q_pos = past_len

    @pl.when(g == 0)
    def _():
        pos = past_len + lax.broadcasted_iota(I32, (N_HEADS_B, 1), 1)
        keep = pos < q_pos
        z = jnp.sum(qbd * kn_ref[0], axis=1, keepdims=True)
        lk = jnp.where(keep, _log_keep(z), 0.0)
        a = jnp.where(keep, jnp.exp(z + lk), 0.0)
        run_scr[...] = jnp.broadcast_to(lk, run_scr.shape)
        acc_scr[...] = vnc_ref[0] * _col_to_row(a)

    run = run_scr[...]
    for p in range(SB_PAGES):
        page = n_pages - 1 - (g * SB_PAGES + p)
        keep = (page * Q_BLOCK + lane) < q_pos
        z = jnp.dot(qbd16, k_refs[p][0].astype(BF16), preferred_element_type=F32)
        lk = jnp.where(keep, _log_keep(z), 0.0)
        res = jnp.dot(_split_hi_lo(lk), w_ref[...], preferred_element_type=F32)
        a = jnp.where(keep, jnp.exp(z + lk + res[:, :LANES] + run), 0.0)
        acc_scr[...] += lax.dot_general(v_refs[p][0].astype(BF16), a.astype(BF16), _NT,
                                        preferred_element_type=F32)
        run = run + res[:, LANES:]
    run_scr[...] = run

    @pl.when(g == n_steps - 1)
    def _():
        r = lax.broadcasted_iota(I32, acc_scr.shape, 0)
        h = lax.broadcasted_iota(I32, acc_scr.shape, 1)
        o_ref[0] = jnp.sum(jnp.where(_div_pow2(r, HEAD_DIM) == h, acc_scr[...], 0.0), axis=1, keepdims=True)


def _sb_sample(page_table, qbd, k_new, v_new_col, cache_kt, cache_vt, past_len):
    bd, n_pages = page_table.shape
    width = cache_kt.shape[1]
    wmat = _suffix_matrix(1)
    per_b = lambda shape: pl.BlockSpec((1,) + shape, lambda b, g, pt: (b,) + (0,) * len(shape))
    kv_specs = _page_specs((width, Q_BLOCK), n_pages, True, SB_PAGES)
    return pl.pallas_call(
        functools.partial(_sb_sample_kernel, past_len=past_len),
        grid_spec=pltpu.PrefetchScalarGridSpec(
            num_scalar_prefetch=1,
            grid=(bd, n_pages // SB_PAGES),
            in_specs=[per_b((N_HEADS_B, width)), per_b((1, width)), per_b((width, 1)),
                      pl.BlockSpec(wmat.shape, lambda b, g, pt: (0, 0))] + kv_specs + kv_specs,
            out_specs=per_b((width, 1)),
            scratch_shapes=[pltpu.VMEM((N_HEADS_B, LANES), F32), pltpu.VMEM((width, N_HEADS_B), F32)],
        ),
        out_shape=jax.ShapeDtypeStruct((bd, width, 1), F32),
        compiler_params=_cparams(2),
        name="sb_sample",
    )(page_table, qbd, k_new, v_new_col, wmat, *([cache_kt] * SB_PAGES), *([cache_vt] * SB_PAGES))


def _merge_kernel(x_ref, oa_ref, ob_ref, ga_ref, gb_ref, wpa, wpb, wout, ln2_ref, wr, br,
                  h_o, hn_o, comb_o):
    mixed = (ga_ref[...] * jnp.dot(oa_ref[...], wpa[...], preferred_element_type=F32)
             + gb_ref[...] * jnp.dot(ob_ref[...], wpb[...], preferred_element_type=F32))
    h = x_ref[...] + jnp.dot(mixed.astype(BF16), wout[...], preferred_element_type=F32)
    h_o[...] = h
    ms = jnp.mean(h * h, axis=-1, keepdims=True)
    hn = (h * lax.rsqrt(ms + EPS) * ln2_ref[...]).astype(BF16)
    hn_o[...] = hn

    logits = jnp.dot(hn, wr[...], preferred_element_type=F32) + br[...]
    lane = lax.broadcasted_iota(I32, logits.shape, 1)
    big = jnp.int32(ROUTER_LANES)

    def softmax_over(mask):
        m = jnp.max(jnp.where(mask, logits, NEG), axis=1, keepdims=True)
        e = jnp.where(mask, jnp.exp(logits - m), 0.0)
        return e / jnp.sum(e, axis=1, keepdims=True)

    def top1(prob, mask):
        best = jnp.max(jnp.where(mask, prob, -1.0), axis=1, keepdims=True)
        idx = jnp.min(jnp.where(mask & (prob == best), lane, big), axis=1, keepdims=True)
        return best, idx

    is_group = lane < N_GROUPS
    g_w, g_idx = top1(softmax_over(is_group), is_group)
    e_lane = lane - N_GROUPS
    in_group = (e_lane >= 0) & (e_lane < N_EXPERTS) & (_div_pow2(e_lane, EXPERTS_PER_GROUP) == g_idx)
    p_e = softmax_over(in_group)
    p1, i1 = top1(p_e, in_group)
    rest = in_group & (lane != i1)
    p2, i2 = top1(p_e, rest)
    denom = p1 + p2
    comb_o[...] = (jnp.where(lane == i1, p1 / denom * g_w, 0.0)
                   + jnp.where(lane == i2, p2 / denom * g_w, 0.0))


def _merge(x2d, o_a, o_b, g_a, g_b, wpa, wpb, wout, ln2, wr, br, tm):
    n, d = x2d.shape
    row = lambda w: pl.BlockSpec((tm, w), lambda i: (i, 0))
    consts = (wpa, wpb, wout, ln2, wr, br)
    return pl.pallas_call(
        _merge_kernel,
        grid=(n // tm,),
        in_specs=[row(d), row(WIDTH_A), row(WIDTH_B), row(d), row(d)] + [_const_spec(c.shape) for c in consts],
        out_specs=[row(d), row(d), row(ROUTER_LANES)],
        out_shape=[jax.ShapeDtypeStruct((n, d), F32), jax.ShapeDtypeStruct((n, d), BF16),
                   jax.ShapeDtypeStruct((n, ROUTER_LANES), F32)],
        compiler_params=_cparams(1),
        name="merge",
    )(x2d, o_a, o_b, g_a, g_b, *consts)


def _moe_kernel(hn_ref, comb_ref, h_ref, wgu_ref, wd_ref, lnf_ref, y_ref, acc_scr, *, final_norm):
    e = pl.program_id(1)

    @pl.when(e == 0)
    def _():
        acc_scr[...] = jnp.zeros(acc_scr.shape, F32)

    d_expert = wd_ref.shape[1]
    gu = jnp.dot(hn_ref[...], wgu_ref[0], preferred_element_type=F32)
    gate, up = gu[:, :d_expert], gu[:, d_expert:]
    comb = comb_ref[...]
    lane = lax.broadcasted_iota(I32, comb.shape, 1)
    c_e = jnp.sum(jnp.where(lane == e + N_GROUPS, comb, 0.0), axis=1, keepdims=True)
    hid = gate * _sigmoid(gate) * up * c_e
    acc_scr[...] += jnp.dot(hid.astype(BF16), wd_ref[0], preferred_element_type=F32)

    @pl.when(e == pl.num_programs(1) - 1)
    def _():
        xo = h_ref[...] + acc_scr[...]
        if final_norm:
            ms = jnp.mean(xo * xo, axis=-1, keepdims=True)
            xo = xo * lax.rsqrt(ms + EPS) * lnf_ref[...]
        y_ref[...] = xo


def _moe(hn, comb, h, wgu, wd, lnf, tm, final_norm):
    n, d = h.shape
    n_exp, _, two_f = wgu.shape
    row = lambda w: pl.BlockSpec((tm, w), lambda i, e: (i, 0))
    return pl.pallas_call(
        functools.partial(_moe_kernel, final_norm=final_norm),
        grid=(n // tm, n_exp),
        in_specs=[row(d), row(ROUTER_LANES), row(d),
                  pl.BlockSpec((1, d, two_f), lambda i, e: (e, 0, 0)),
                  pl.BlockSpec((1, two_f // 2, d), lambda i, e: (e, 0, 0)),
                  pl.BlockSpec((1, d), lambda i, e: (0, 0))],
        out_specs=row(d),
        out_shape=jax.ShapeDtypeStruct((n, d), F32),
        scratch_shapes=[pltpu.VMEM((tm, d), F32)],
        compiler_params=_cparams(2),
        name="moe",
    )(hn, comb, h, wgu, wd, lnf)


def _tile(n, pref):
    return pref if n % pref == 0 else n


def kernel(x_prompt, x_sample, cache_k_a, cache_v_a, cache_kidx_a, cache_k_b, cache_v_b, page_table,
           rel_bias_table, ln1_g, w_in, w_proj_a, w_proj_b, w_out, ln2_g, w_router_group, b_router_group,
           w_router_expert, b_router_expert, w_gate, w_up, w_down, ln_f_g):
    bsz, seq, d = x_prompt.shape
    bd, t_new, _ = x_sample.shape
    assert t_new == 1, "the sample group decodes one token per sequence"
    depth = w_in.shape[0]
    pool = cache_k_a.shape[1]
    n_pages = page_table.shape[1]
    past_len = n_pages * Q_BLOCK
    assert n_pages % SMALL_PAGES == 0 and n_pages % SB_PAGES == 0
    n_p, n_s = bsz * seq, bd * t_new
    d_expert = w_gate.shape[-1]

    xp, xs = x_prompt.reshape(n_p, d), x_sample.reshape(n_s, d)
    new_p, new_s = [], []
    for l in range(depth):
        w_pieces = _split_w_in(w_in[l])
        wpa, wpb, wout = (w.astype(BF16) for w in (w_proj_a[l], w_proj_b[l], w_out[l]))
        ln2 = ln2_g[l].reshape(1, d)
        wr = jnp.concatenate([w_router_group[l], jnp.moveaxis(w_router_expert[l], 0, 1).reshape(d, N_EXPERTS),
                              jnp.zeros((d, ROUTER_LANES - N_GROUPS - N_EXPERTS), F32)], axis=1).astype(BF16)
        br = jnp.concatenate([b_router_group[l], b_router_expert[l].reshape(N_EXPERTS),
                              jnp.zeros((ROUTER_LANES - N_GROUPS - N_EXPERTS,), F32)]).reshape(1, ROUTER_LANES)
        wgu = jnp.concatenate([w_gate[l], w_up[l]], axis=-1).reshape(N_EXPERTS, d, 2 * d_expert).astype(BF16)
        wd = w_down[l].reshape(N_EXPERTS, d_expert, d).astype(BF16)
        lnf = ln_f_g.reshape(1, d)

        def ffn(x2d, o_a, o_b, g_a, g_b, tm_merge, tm_moe):
            h, hn, comb = _merge(x2d, o_a, o_b, g_a, g_b, wpa, wpb, wout, ln2, wr, br, tm_merge)
            return _moe(hn, comb, h, wgu, wd, lnf, tm_moe, final_norm=(l == depth - 1))

        (qa, ka, va, kad, vad, qi, ki, kid, wi, qb, kb, vb, kb16, vb16, ga, gb) = _project(
            xp, ln1_g[l], w_pieces, _tile(n_p, 512))
        r3 = lambda a: a.reshape(bsz, seq, a.shape[-1])
        n_top = min(TOPK_MAX, seq // 4)
        wit = jnp.swapaxes(r3(wi), 1, 2)
        vadt = jnp.swapaxes(vad.reshape(bsz, seq // Q_BLOCK, Q_BLOCK, 2 * LANES), 2, 3)
        o_at = _dsa_prompt(rel_bias_table, r3(qi), wit, r3(kid), r3(qa), r3(kad), vadt, n_top)
        o_a = jnp.swapaxes(o_at, 1, 2)
        o_b = _sb_prompt(r3(qb), r3(kb16), r3(vb16))
        xp = ffn(xp, o_a.reshape(n_p, WIDTH_A), o_b.reshape(n_p, WIDTH_B), ga, gb, _tile(n_p, 512), _tile(n_p, 1024))
        new_p.append((ka, va, ki, kb, vb))

        (qa, ka, va, _, _, qi, ki, _, wi, qb, kb, vb, _, _, ga, gb) = _project(xs, ln1_g[l], w_pieces, n_s)
        n_top = min(TOPK_MAX, (past_len + t_new) // 4)
        qi3 = qi.astype(F32).reshape(bd, N_IDX_HEADS, IDX_DIM)
        wi3 = wi.reshape(bd, N_IDX_HEADS, 1)
        paged_t = lambda c: jnp.moveaxis(c, 1, -1).reshape(pool, -1, Q_BLOCK)
        scores = _idx_sample(page_table, qi3, wi3, paged_t(cache_kidx_a[l]))
        madd, maddn = _select_sample(scores.reshape(bd, n_pages, LANES), qi3, wi3, ki.reshape(bd, 1, IDX_DIM), n_top)
        qa4 = qa.astype(F32).reshape(bd, N_HEADS_A, 1, HEAD_DIM) * (HEAD_DIM ** -0.5)
        kv_of_head = (np.arange(N_HEADS_A)[:, None] // GROUP_A) == np.arange(N_KV_A)[None, :]
        qz = jnp.where(kv_of_head[None, :, :, None], qa4, 0.0).reshape(bd, N_HEADS_A, N_KV_A * HEAD_DIM)
        logits = _dsa_logits(page_table, rel_bias_table, qz, madd.reshape(bd, n_pages, 1, LANES),
                             paged_t(cache_k_a[l]), past_len)
        o_a = _dsa_pv(page_table, rel_bias_table, logits, qz, maddn, ka.reshape(bd, 1, -1), va.reshape(bd, -1, 1),
                      paged_t(cache_v_a[l]))
        o_a = jnp.swapaxes(o_a, 1, 2)
        qb4 = qb.astype(F32).reshape(bd, N_HEADS_B, 1, HEAD_DIM) * (HEAD_DIM ** -0.5)
        own = np.eye(N_HEADS_B, dtype=bool)
        qbd = jnp.where(own[None, :, :, None], qb4, 0.0).reshape(bd, N_HEADS_B, WIDTH_B)
        o_b = _sb_sample(page_table, qbd, kb.reshape(bd, 1, -1), vb.reshape(bd, -1, 1),
                         paged_t(cache_k_b[l]), paged_t(cache_v_b[l]), past_len)
        xs = ffn(xs, o_a.reshape(n_s, WIDTH_A).astype(BF16), o_b.reshape(n_s, WIDTH_B).astype(BF16), ga, gb, n_s, n_s)
        new_s.append((ka, va, ki, kb, vb))

    def stack(rows, idx, lead, tail):
        return jnp.stack([r[idx].reshape(lead + tail) for r in rows])

    outs = [xp.reshape(bsz, seq, d), xs.reshape(bd, t_new, d)]
    tails = ((N_KV_A, HEAD_DIM), (N_KV_A, HEAD_DIM), (IDX_DIM,), (N_HEADS_B, HEAD_DIM), (N_HEADS_B, HEAD_DIM))
    for rows, lead in ((new_p, (bsz, seq)), (new_s, (bd, t_new))):
        for idx, tail in enumerate(tails):
            outs.append(stack(rows, idx, lead, tail))
    return tuple(outs)
```

```python
import functools
import math

import jax
import jax.numpy as jnp
import numpy as np
from jax import lax
from jax.experimental import pallas as pl
from jax.experimental.pallas import tpu as pltpu

F32 = jnp.float32
BF16 = jnp.bfloat16
I32 = jnp.int32

HEAD_DIM = 64
N_HEADS_A = 8
N_KV_A = 2
GROUP_A = N_HEADS_A // N_KV_A
N_IDX_HEADS = 8
IDX_DIM = 64
N_HEADS_B = 8
WIDTH_A = N_HEADS_A * HEAD_DIM
WIDTH_B = N_HEADS_B * HEAD_DIM
TOPK_MAX = 256
N_BUCKETS = 32
MAX_DISTANCE = 128
N_GROUPS = 4
EXPERTS_PER_GROUP = 8
N_EXPERTS = N_GROUPS * EXPERTS_PER_GROUP
TOP_E = 2
Q_BLOCK = 128
EPS = 1e-6

Q_TILE = 256

LANES = 128
VMEM_LIMIT_BYTES = 56 * 1024 * 1024

SMALL_PAGES = 32
SB_PAGES = 16
SELECT_SEQS = 4
MOE_CHUNK = 1024
MOE_ROWS = 128
MOE_UNROLL = 8
ROUTE_ROWS = 8
NEG = -1e30
INT_MIN = -(2 ** 31)
ROUTER_LANES = LANES

_NT = (((1,), (1,)), ((), ()))


def _cparams(n_axes):
    return pltpu.CompilerParams(dimension_semantics=("arbitrary",) * n_axes,
                                vmem_limit_bytes=VMEM_LIMIT_BYTES)


def _const_spec(shape):
    nd = len(shape)
    return pl.BlockSpec(shape, lambda *_: (0,) * nd)


def _div_pow2(x, n):
    assert n & (n - 1) == 0
    return x >> (n.bit_length() - 1)


def _sigmoid(x):
    return 1.0 / (1.0 + jnp.exp(-x))


def _half_masks(dtype):
    lane = lax.broadcasted_iota(I32, (LANES, LANES), 1)
    lo = jnp.where(lane < HEAD_DIM, 1.0, 0.0).astype(dtype)
    hi = jnp.where(lane < HEAD_DIM, 0.0, 1.0).astype(dtype)
    return lo, hi


def _pair_block_diag(blk, lo, hi):
    return jnp.concatenate([blk * lo, blk * hi], axis=0)


def _row_half_masks(dtype):
    r = lax.broadcasted_iota(I32, (LANES, LANES), 0)
    top = jnp.where(r < HEAD_DIM, 1.0, 0.0).astype(dtype)
    bottom = jnp.where(r < HEAD_DIM, 0.0, 1.0).astype(dtype)
    return top, bottom


def _pair_block_diag_t(blk_t, top, bottom):
    return jnp.concatenate([blk_t * top, blk_t * bottom], axis=1)


def _col_to_row(x):
    n = x.shape[0]
    eye = lax.broadcasted_iota(I32, (n, n), 0) == lax.broadcasted_iota(I32, (n, n), 1)
    return jnp.sum(jnp.where(eye, x, 0.0), axis=0, keepdims=True)


def _order_key(score):
    score = jnp.where(score == 0.0, 0.0, score)
    b = lax.bitcast_convert_type(score, I32)
    return b ^ ((b >> 31) & jnp.int32(0x7FFFFFFF))


def _t5_bucket(dist):
    n = jnp.maximum(dist, 0)
    max_exact = N_BUCKETS // 2
    nf = jnp.maximum(n, max_exact).astype(F32)
    large = max_exact + (jnp.log(nf / max_exact) / math.log(MAX_DISTANCE / max_exact)
                         * (N_BUCKETS - max_exact)).astype(I32)
    return jnp.where(n < max_exact, n, jnp.minimum(large, N_BUCKETS - 1))


def _proj_kernel(x_ref, g_ref, wqa, wkv, wkvd, wqi, wkiw, wqb, wkb, wvb, wga, wgb,
                 qa_o, ka_o, va_o, kad_o, vad_o, qi_o, ki_o, kid_o, wi_o,
                 qb_o, kb_o, vb_o, kb16_o, vb16_o, ga_o, gb_o, *, feature_major):
    x = x_ref[...]
    ms = jnp.mean(x * x, axis=-1, keepdims=True)
    xn = (x * lax.rsqrt(ms + EPS) * g_ref[...]).astype(BF16)

    def mm(w):
        return jnp.dot(xn, w[...], preferred_element_type=F32)

    def put_new_rows(o_ref, val):
        if feature_major:
            o_ref[0] = val.T
        else:
            o_ref[...] = val

    qa_o[...] = mm(wqa).astype(BF16)
    kv = mm(wkv)
    put_new_rows(ka_o, kv[:, :N_KV_A * HEAD_DIM])
    put_new_rows(va_o, kv[:, N_KV_A * HEAD_DIM:])
    kvd = mm(wkvd).astype(BF16)
    kad_o[...] = kvd[:, :2 * LANES]
    vad_o[...] = kvd[:, 2 * LANES:]
    qi_o[...] = mm(wqi).astype(BF16)
    kiw = mm(wkiw)
    put_new_rows(ki_o, kiw[:, :IDX_DIM])
    kid_o[...] = kiw[:, :LANES].astype(BF16)
    wi_o[...] = kiw[:, LANES:LANES + N_IDX_HEADS] * (N_IDX_HEADS ** -0.5)
    qb_o[...] = mm(wqb).astype(BF16)
    kb = mm(wkb)
    put_new_rows(kb_o, kb)
    kb16_o[...] = kb.astype(BF16)
    vb = mm(wvb)
    put_new_rows(vb_o, vb)
    vb16_o[...] = vb.astype(BF16)
    ga_o[...] = _sigmoid(mm(wga))
    gb_o[...] = _sigmoid(mm(wgb))


def _split_w_in(w_in):
    d_model = w_in.shape[0]
    widths = (WIDTH_A, N_KV_A * HEAD_DIM, N_KV_A * HEAD_DIM, N_IDX_HEADS * IDX_DIM, N_IDX_HEADS, IDX_DIM,
              WIDTH_B, WIDTH_B, WIDTH_B, d_model, d_model)
    points = [int(p) for p in np.cumsum(widths)[:-1]]
    qa, ka, va, qi, wi, ki, qb, kb, vb, ga, gb = jnp.split(w_in, points, axis=1)
    dup = lambda a: jnp.concatenate([a[:, :HEAD_DIM], a[:, :HEAD_DIM], a[:, HEAD_DIM:], a[:, HEAD_DIM:]], axis=1)
    wkv = jnp.concatenate([ka, va], axis=1)
    wkvd = jnp.concatenate([dup(ka), dup(va)], axis=1)
    wkiw = jnp.concatenate([ki, ki, wi, jnp.zeros((d_model, LANES - N_IDX_HEADS), w_in.dtype)], axis=1)
    return tuple(a.astype(BF16) for a in (qa, wkv, wkvd, qi, wkiw, qb, kb, vb, ga, gb))


_NEW_ROW_OUTPUTS = (1, 2, 6, 10, 11)


def _project(x2d, ln_g, w_pieces, tm, seq_len=None):
    n, d = x2d.shape
    assert n % tm == 0
    out_defs = [
        (WIDTH_A, BF16), (N_KV_A * HEAD_DIM, F32), (N_KV_A * HEAD_DIM, F32), (2 * LANES, BF16), (2 * LANES, BF16),
        (N_IDX_HEADS * IDX_DIM, BF16), (IDX_DIM, F32), (LANES, BF16), (N_IDX_HEADS, F32),
        (WIDTH_B, BF16), (WIDTH_B, F32), (WIDTH_B, F32), (WIDTH_B, BF16), (WIDTH_B, BF16), (d, F32), (d, F32)]
    row = lambda i: (i, 0)
    out_specs = [pl.BlockSpec((tm, w), row) for w, _ in out_defs]
    out_shape = [jax.ShapeDtypeStruct((n, w), dt) for w, dt in out_defs]
    if seq_len is not None:
        assert seq_len % tm == 0
        tiles = seq_len // tm
        for k in _NEW_ROW_OUTPUTS:
            w, dt = out_defs[k]
            out_specs[k] = pl.BlockSpec((1, w, tm), lambda i: (i // tiles, 0, i % tiles))
            out_shape[k] = jax.ShapeDtypeStruct((n // seq_len, w, seq_len), dt)
    return pl.pallas_call(
        functools.partial(_proj_kernel, feature_major=seq_len is not None),
        grid=(n // tm,),
        in_specs=[pl.BlockSpec((tm, d), row), _const_spec((1, d))] + [_const_spec(w.shape) for w in w_pieces],
        out_specs=out_specs,
        out_shape=out_shape,
        compiler_params=_cparams(1),
        name="proj",
    )(x2d, ln_g.reshape(1, d), *w_pieces)


def _dsa_prompt_kernel(tab_ref, qi_ref, wit_ref, kid_ref, qa_ref, kad_ref, vadt_ref, tril_ref, o_ref,
                       key_scr, lg_scr, bias_scr, acc_scr, *, n_top):
    b = pl.program_id(0)
    i = pl.program_id(1)
    kb, qt = Q_BLOCK, Q_TILE
    n_blk = (i + 1) * (qt // kb)
    row = lax.broadcasted_iota(I32, (kb, qt), 0)
    col = lax.broadcasted_iota(I32, (kb, qt), 1)
    lo, hi = _half_masks(BF16)
    top, bottom = _row_half_masks(BF16)
    d_min = 1 - qt // kb

    def block_dist(j):
        return i * (qt // kb) - j

    @pl.when((b == 0) & (i == 0))
    def _():
        for d in range(d_min, 3):
            bucket = _t5_bucket(d * kb + col - row)
            for h in range(N_HEADS_A):
                def sel(bk, acc):
                    return jnp.where(bucket == bk, tab_ref[bk, h], acc)
                tile = lax.fori_loop(0, N_BUCKETS, sel, jnp.zeros((kb, qt), F32))
                bias_scr[d - d_min, h // 2, (h % 2) * kb:(h % 2 + 1) * kb, :] = tile

    def causal(j):
        return block_dist(j) * kb + col - row >= 0

    w = wit_ref[0] * (IDX_DIM ** -0.5)
    wrows = [w[h:h + 1, :] for h in range(N_IDX_HEADS)]

    def score_body(j, carry):
        kblk = kid_ref[0, pl.ds(pl.multiple_of(j * kb, kb), kb), :]
        kbd = _pair_block_diag(kblk, lo, hi)
        acc = jnp.zeros((kb, qt), F32)
        for p in range(N_IDX_HEADS // 2):
            r = lax.dot_general(kbd, qi_ref[0, :, p * LANES:(p + 1) * LANES], _NT,
                                preferred_element_type=F32)
            acc = acc + jnp.maximum(r[:kb], 0.0) * wrows[2 * p] + jnp.maximum(r[kb:], 0.0) * wrows[2 * p + 1]
        key_scr[j] = _order_key(jnp.where(causal(j), acc, -jnp.inf))
        return carry

    lax.fori_loop(0, n_blk, score_body, 0)

    def count(pred):
        def body(j, acc):
            return acc + jnp.where(pred(key_scr[j]), 1, 0)
        acc = lax.fori_loop(0, n_blk, body, jnp.zeros((kb, qt), I32))
        return jnp.sum(acc, axis=0, keepdims=True)

    thr = jnp.where(count(lambda k: k >= 0) >= n_top, 0, INT_MIN).astype(I32)

    def bit_body(t, thr):
        cand = thr | lax.shift_left(jnp.int32(1), jnp.int32(30) - t)
        return jnp.where(count(lambda k: k >= cand) >= n_top, cand, thr)

    thr = lax.fori_loop(0, 31, bit_body, thr)
    need = (n_top - count(lambda k: k > thr)).astype(F32)

    tril = tril_ref[...]

    n_pairs = N_HEADS_A // 2
    q_ps = [(qa_ref[0, :, p * LANES:(p + 1) * LANES].astype(F32) * (HEAD_DIM ** -0.5)).astype(BF16)
            for p in range(n_pairs)]

    def logits_body(j, carry):
        ties_before, maxes = carry[0], carry[1:]
        key = key_scr[j]
        eq = key == thr
        pre = jnp.dot(tril, jnp.where(eq, 1.0, 0.0).astype(BF16), preferred_element_type=F32)
        take = (key > thr) | (eq & (ties_before + pre <= need))
        madd = jnp.where(take & causal(j), 0.0, NEG)
        rows = pl.ds(pl.multiple_of(j * kb, kb), kb)
        dsel = jnp.minimum(block_dist(j), 2) - d_min
        out = [ties_before + pre[kb - 1:kb, :]]
        for p in range(n_pairs):
            c = (2 * p) // GROUP_A
            kbd = _pair_block_diag(kad_ref[0, rows, c * LANES:(c + 1) * LANES], lo, hi)
            lg = lax.dot_general(kbd, q_ps[p], _NT, preferred_element_type=F32) + bias_scr[dsel, p]
            for half in range(2):
                lgh = lg[half * kb:(half + 1) * kb] + madd
                lg_scr[j, p, half * kb:(half + 1) * kb, :] = lgh
                out.append(jnp.maximum(maxes[2 * p + half], jnp.max(lgh, axis=0, keepdims=True)))
        return tuple(out)

    neg = jnp.full((1, qt), NEG, F32)
    zero = jnp.zeros((1, qt), F32)
    maxes = lax.fori_loop(0, n_blk, logits_body, (zero,) + (neg,) * N_HEADS_A)[1:]
    acc_scr[...] = jnp.zeros(acc_scr.shape, F32)

    def pv_body(j, sums):
        out = []
        for p in range(n_pairs):
            c = (2 * p) // GROUP_A
            vbd = _pair_block_diag_t(vadt_ref[0, j, c * LANES:(c + 1) * LANES, :], top, bottom)
            p0 = jnp.exp(lg_scr[j, p, :kb, :] - maxes[2 * p])
            p1 = jnp.exp(lg_scr[j, p, kb:, :] - maxes[2 * p + 1])
            out += [sums[2 * p] + jnp.sum(p0, axis=0, keepdims=True),
                    sums[2 * p + 1] + jnp.sum(p1, axis=0, keepdims=True)]
            acc_scr[p] += jnp.dot(vbd, jnp.concatenate([p0, p1], axis=0).astype(BF16),
                                  preferred_element_type=F32)
        return tuple(out)

    sums = lax.fori_loop(0, n_blk, pv_body, (zero,) * N_HEADS_A)
    for p in range(n_pairs):
        denom = jnp.where(row < HEAD_DIM, sums[2 * p], sums[2 * p + 1])
        o_ref[0, p * LANES:(p + 1) * LANES, :] = (acc_scr[p] / denom).astype(o_ref.dtype)


def _dsa_prompt(bias_table, qi, wit, kid, qa, kad, vadt, n_top):
    bsz, seq, _ = qi.shape
    assert seq % Q_TILE == 0 and Q_TILE % Q_BLOCK == 0
    nkb = seq // Q_BLOCK
    n_bias = Q_TILE // Q_BLOCK + 2
    tril = (np.arange(Q_BLOCK)[:, None] >= np.arange(Q_BLOCK)[None, :]).astype(np.float32)
    qblk = lambda w: pl.BlockSpec((1, Q_TILE, w), lambda b, i: (b, i, 0))
    full = lambda w: pl.BlockSpec((1, seq, w), lambda b, i: (b, 0, 0))
    return pl.pallas_call(
        functools.partial(_dsa_prompt_kernel, n_top=n_top),
        grid=(bsz, seq // Q_TILE),
        in_specs=[pl.BlockSpec(memory_space=pltpu.SMEM), qblk(N_IDX_HEADS * IDX_DIM),
                  pl.BlockSpec((1, N_IDX_HEADS, Q_TILE), lambda b, i: (b, 0, i)), full(LANES),
                  qblk(WIDTH_A), full(2 * LANES),
                  pl.BlockSpec((1, nkb, 2 * LANES, Q_BLOCK), lambda b, i: (b, 0, 0, 0)),
                  _const_spec((Q_BLOCK, Q_BLOCK))],
        out_specs=pl.BlockSpec((1, WIDTH_A, Q_TILE), lambda b, i: (b, 0, i)),
        out_shape=jax.ShapeDtypeStruct((bsz, WIDTH_A, seq), BF16),
        scratch_shapes=[pltpu.VMEM((nkb, Q_BLOCK, Q_TILE), I32),
                        pltpu.VMEM((nkb, N_HEADS_A // 2, 2 * Q_BLOCK, Q_TILE), F32),
                        pltpu.VMEM((n_bias, N_HEADS_A // 2, 2 * Q_BLOCK, Q_TILE), F32),
                        pltpu.VMEM((N_HEADS_A // 2, LANES, Q_TILE), F32)],
        compiler_params=_cparams(2),
        name="dsa_prompt",
    )(bias_table, qi, wit, kid, qa, kad, vadt, jnp.asarray(tril, BF16))


def _suffix_matrix(n_heads):
    n = n_heads * Q_BLOCK
    r = np.arange(n)
    same = (r[:, None] // Q_BLOCK) == (r[None, :] // Q_BLOCK)
    u = same & (r[:, None] > r[None, :])
    w = np.concatenate([u, same], axis=1).astype(np.float32)
    return jnp.asarray(np.concatenate([w, w], axis=0), BF16)


def _log_keep(z):
    return -(jnp.maximum(z, 0.0) + jnp.log(1.0 + jnp.exp(-jnp.abs(z))))


def _split_hi_lo(x):
    hi = x.astype(BF16)
    lo = (x - hi.astype(F32)).astype(BF16)
    return jnp.concatenate([hi, lo], axis=1)


def _sb_prompt_kernel(qb_ref, kb_ref, vb_ref, w_ref, o_ref, run_scr, acc_scr):
    i = pl.program_id(1)
    kb, qt = Q_BLOCK, Q_TILE
    bpt = qt // kb
    n_pairs = N_HEADS_B // 2
    row = lax.broadcasted_iota(I32, (qt, 2 * kb), 0)
    col = lax.broadcasted_iota(I32, (qt, 2 * kb), 1) & (kb - 1)
    lo, hi = _half_masks(BF16)
    q_ps = [(qb_ref[0, :, p * LANES:(p + 1) * LANES].astype(F32) * (HEAD_DIM ** -0.5)).astype(BF16)
            for p in range(n_pairs)]

    def block(j, edge, first):
        rows = pl.ds(pl.multiple_of(j * kb, kb), kb)
        if edge:
            strict = j * kb + col < i * qt + row
        for p in range(n_pairs):
            kbd = _pair_block_diag(kb_ref[0, rows, p * LANES:(p + 1) * LANES], lo, hi)
            vbd = _pair_block_diag(vb_ref[0, rows, p * LANES:(p + 1) * LANES], lo, hi)
            z = lax.dot_general(q_ps[p], kbd, _NT, preferred_element_type=F32)
            lk = _log_keep(z)
            if edge:
                lk = jnp.where(strict, lk, 0.0)
            res = jnp.dot(_split_hi_lo(lk), w_ref[...], preferred_element_type=F32)
            x = z + lk + res[:, :2 * kb]
            if not first:
                x = x + run_scr[p]
            a = jnp.exp(x)
            if edge:
                a = jnp.where(strict, a, 0.0)
            pv = jnp.dot(a.astype(BF16), vbd, preferred_element_type=F32)
            if first:
                run_scr[p] = res[:, 2 * kb:]
                acc_scr[p] = pv
            else:
                run_scr[p] += res[:, 2 * kb:]
                acc_scr[p] += pv

    for e in range(bpt):
        block((i + 1) * bpt - 1 - e, True, e == 0)

    def body(t, carry):
        block(i * bpt - 1 - t, False, False)
        return carry

    lax.fori_loop(0, i * bpt, body, 0)
    for p in range(n_pairs):
        o_ref[0, :, p * LANES:(p + 1) * LANES] = acc_scr[p].astype(o_ref.dtype)


def _sb_prompt(qb, kb16, vb16):
    bsz, seq, width = qb.shape
    assert seq % Q_TILE == 0 and Q_TILE % Q_BLOCK == 0
    wmat = _suffix_matrix(2)
    qblk = pl.BlockSpec((1, Q_TILE, width), lambda b, i: (b, i, 0))
    full = pl.BlockSpec((1, seq, width), lambda b, i: (b, 0, 0))
    return pl.pallas_call(
        _sb_prompt_kernel,
        grid=(bsz, seq // Q_TILE),
        in_specs=[qblk, full, full, _const_spec(wmat.shape)],
        out_specs=qblk,
        out_shape=jax.ShapeDtypeStruct((bsz, seq, width), BF16),
        scratch_shapes=[pltpu.VMEM((N_HEADS_B // 2, Q_TILE, 2 * Q_BLOCK), F32),
                        pltpu.VMEM((N_HEADS_B // 2, Q_TILE, LANES), F32)],
        compiler_params=_cparams(2),
        name="sb_prompt",
    )(qb, kb16, vb16, wmat)


def _page_specs(block_tail, n_pages, reverse, pages=None):
    pages = pages or SMALL_PAGES

    def spec(p):
        def index_map(b, g, pt):
            page = g * pages + p
            if reverse:
                page = n_pages - 1 - page
            return (pt[b, page],) + (0,) * len(block_tail)
        return pl.BlockSpec((1,) + block_tail, index_map)
    return [spec(p) for p in range(pages)]


def _idx_sample_kernel(pt_ref, q_ref, w_ref, *refs):
    k_refs, o_ref = refs[:SMALL_PAGES], refs[SMALL_PAGES]
    q = q_ref[0].astype(BF16)
    w = w_ref[0] * (IDX_DIM ** -0.5)
    for p in range(SMALL_PAGES):
        r = jnp.dot(q, k_refs[p][0].astype(BF16), preferred_element_type=F32)
        o_ref[0, p] = jnp.sum(jnp.maximum(r, 0.0) * w, axis=0, keepdims=True)


def _idx_sample(page_table, qi3, wi3, cache_kidx):
    bd, n_pages = page_table.shape
    per_b = lambda shape: pl.BlockSpec((1,) + shape, lambda b, g, pt: (b,) + (0,) * len(shape))
    return pl.pallas_call(
        _idx_sample_kernel,
        grid_spec=pltpu.PrefetchScalarGridSpec(
            num_scalar_prefetch=1,
            grid=(bd, n_pages // SMALL_PAGES),
            in_specs=[per_b((N_IDX_HEADS, IDX_DIM)), per_b((N_IDX_HEADS, 1))]
            + _page_specs((IDX_DIM, Q_BLOCK), n_pages, False),
            out_specs=pl.BlockSpec((1, SMALL_PAGES, 1, LANES), lambda b, g, pt: (b, g, 0, 0)),
        ),
        out_shape=jax.ShapeDtypeStruct((bd, n_pages, 1, LANES), F32),
        compiler_params=_cparams(2),
        name="idx_sample",
    )(page_table, qi3, wi3, *([cache_kidx] * SMALL_PAGES))


def _select_sample_kernel(sc_ref, q_ref, w_ref, k_ref, tri_ref, low_ref, madd_o, maddn_o, *, n_top):
    n_seq = sc_ref.shape[0]
    seqs = range(n_seq)

    def total(x):
        return jnp.sum(jnp.sum(x, axis=1, keepdims=True), axis=0, keepdims=True)

    keys, keys_new = [], []
    for s in seqs:
        keys.append(_order_key(sc_ref[s]))
        r_new = jnp.sum(q_ref[s] * k_ref[s], axis=1, keepdims=True)
        s_new = jnp.sum(jnp.maximum(r_new * (IDX_DIM ** -0.5), 0.0) * w_ref[s], axis=0, keepdims=True)
        keys_new.append(_order_key(s_new))

    def count(s, pred):
        return total(jnp.where(pred(keys[s]), 1, 0)) + jnp.where(pred(keys_new[s]), 1, 0)

    thrs = tuple(jnp.where(count(s, lambda k: k >= 0) >= n_top, 0, INT_MIN).astype(I32) for s in seqs)

    def bit_body(t, thrs):
        bit = lax.shift_left(jnp.int32(1), jnp.int32(30) - t)
        cands = [thr | bit for thr in thrs]
        return tuple(jnp.where(count(s, lambda k: k >= cands[s]) >= n_top, cands[s], thrs[s]) for s in seqs)

    thrs = lax.fori_loop(0, 31, bit_body, thrs)
    for s in seqs:
        key, key_new, thr = keys[s], keys_new[s], thrs[s]
        need = (n_top - count(s, lambda k: k > thr)).astype(F32)
        eq = jnp.where(key == thr, 1.0, 0.0).astype(BF16)
        pre = jnp.dot(eq, tri_ref[...], preferred_element_type=F32)
        tot = jnp.broadcast_to(pre[:, LANES - 1:LANES], pre.shape).astype(BF16)
        before = jnp.dot(low_ref[...], tot, preferred_element_type=F32)
        take = (key > thr) | ((key == thr) & (before + pre <= need))
        madd_o[s] = jnp.where(take, 0.0, NEG)
        ties_past = total(jnp.where(key == thr, 1.0, 0.0))
        take_new = (key_new > thr) | ((key_new == thr) & (ties_past + 1.0 <= need))
        maddn_o[s] = jnp.where(take_new, 0.0, NEG)


def _select_sample(scores, qi3, wi3, ki3, n_top):
    bd, n_pages, _ = scores.shape
    n_seq = math.gcd(bd, SELECT_SEQS)
    tri = jnp.asarray((np.arange(LANES)[:, None] <= np.arange(LANES)[None, :]).astype(np.float32), BF16)
    low = jnp.asarray((np.arange(n_pages)[:, None] > np.arange(n_pages)[None, :]).astype(np.float32), BF16)
    per_b = lambda shape: pl.BlockSpec((n_seq,) + shape, lambda b: (b,) + (0,) * len(shape))
    return pl.pallas_call(
        functools.partial(_select_sample_kernel, n_top=n_top),
        grid=(bd // n_seq,),
        in_specs=[per_b((n_pages, LANES)), per_b((N_IDX_HEADS, IDX_DIM)), per_b((N_IDX_HEADS, 1)),
                  per_b((1, IDX_DIM)), _const_spec(tri.shape), _const_spec(low.shape)],
        out_specs=[per_b((n_pages, LANES)), per_b((1, 1))],
        out_shape=[jax.ShapeDtypeStruct((bd, n_pages, LANES), F32), jax.ShapeDtypeStruct((bd, 1, 1), F32)],
        compiler_params=_cparams(1),
        name="select_sample",
    )(scores, qi3, wi3, ki3, tri, low)


def _sample_bias(tabt, dist):
    bucket = _t5_bucket(dist)
    out = jnp.zeros((tabt.shape[0], dist.shape[1]), F32)
    for bk in range(N_BUCKETS):
        out = jnp.where(bucket == bk, tabt[:, bk:bk + 1], out)
    return out


def _dsa_logits_kernel(pt_ref, tabt_ref, qz_ref, madd_ref, *refs, past_len):
    k_refs, o_ref = refs[:SMALL_PAGES], refs[SMALL_PAGES]
    g = pl.program_id(1)
    qz16 = qz_ref[0].astype(BF16)
    lane = lax.broadcasted_iota(I32, (1, LANES), 1)
    tabt = tabt_ref[...]
    for p in range(SMALL_PAGES):
        page = g * SMALL_PAGES + p
        lg = jnp.dot(qz16, k_refs[p][0].astype(BF16), preferred_element_type=F32)
        o_ref[0, p] = lg + _sample_bias(tabt, past_len - (page * Q_BLOCK + lane)) + madd_ref[0, p]


def _dsa_logits(page_table, bias_table, qz, madd, cache_kt, past_len):
    bd, n_pages = page_table.shape
    per_b = lambda shape: pl.BlockSpec((1,) + shape, lambda b, g, pt: (b,) + (0,) * len(shape))
    return pl.pallas_call(
        functools.partial(_dsa_logits_kernel, past_len=past_len),
        grid_spec=pltpu.PrefetchScalarGridSpec(
            num_scalar_prefetch=1,
            grid=(bd, n_pages // SMALL_PAGES),
            in_specs=[pl.BlockSpec((N_HEADS_A, N_BUCKETS), lambda b, g, pt: (0, 0)),
                      per_b((N_HEADS_A, LANES)),
                      pl.BlockSpec((1, SMALL_PAGES, 1, LANES), lambda b, g, pt: (b, g, 0, 0))]
            + _page_specs((N_KV_A * HEAD_DIM, Q_BLOCK), n_pages, False),
            out_specs=pl.BlockSpec((1, SMALL_PAGES, N_HEADS_A, LANES), lambda b, g, pt: (b, g, 0, 0)),
        ),
        out_shape=jax.ShapeDtypeStruct((bd, n_pages, N_HEADS_A, LANES), F32),
        compiler_params=_cparams(2),
        name="dsa_logits",
    )(page_table, bias_table.T, qz, madd, *([cache_kt] * SMALL_PAGES))


def _dsa_pv_kernel(pt_ref, tabt_ref, lg_ref, qz_ref, maddn_ref, kn_ref, vnc_ref, *refs):
    v_refs = refs[:SMALL_PAGES]
    o_ref, m_scr, lsum_scr, acc_scr = refs[SMALL_PAGES:]
    g = pl.program_id(1)

    def new_logit():
        lg = jnp.sum(qz_ref[0] * kn_ref[0], axis=1, keepdims=True)
        return lg + _sample_bias(tabt_ref[...], jnp.zeros((1, 1), I32)) + maddn_ref[0]

    @pl.when(g == 0)
    def _():
        m_past = jnp.max(jnp.max(lg_ref[0], axis=0), axis=1, keepdims=True)
        m_scr[...] = jnp.maximum(m_past, new_logit())
        lsum_scr[...] = jnp.zeros(lsum_scr.shape, F32)
        acc_scr[...] = jnp.zeros(acc_scr.shape, F32)

    m = m_scr[...]
    for p in range(SMALL_PAGES):
        pr = jnp.exp(lg_ref[0, g * SMALL_PAGES + p] - m)
        lsum_scr[...] += pr
        acc_scr[...] += lax.dot_general(v_refs[p][0].astype(BF16), pr.astype(BF16), _NT,
                                        preferred_element_type=F32)

    @pl.when(g == pl.num_programs(1) - 1)
    def _():
        p_new = jnp.exp(new_logit() - m)
        denom = jnp.sum(lsum_scr[...], axis=1, keepdims=True) + p_new
        out = (acc_scr[...] + vnc_ref[0] * _col_to_row(p_new)) / _col_to_row(denom)
        r = lax.broadcasted_iota(I32, out.shape, 0)
        h = lax.broadcasted_iota(I32, out.shape, 1)
        out = jnp.where(_div_pow2(r, HEAD_DIM) == _div_pow2(h, GROUP_A), out, 0.0)
        o_ref[0] = out[:HEAD_DIM] + out[HEAD_DIM:]


def _dsa_pv(page_table, bias_table, logits, qz, maddn, k_new, v_new_col, cache_vt):
    bd, n_pages = page_table.shape
    per_b = lambda shape: pl.BlockSpec((1,) + shape, lambda b, g, pt: (b,) + (0,) * len(shape))
    kv = N_KV_A * HEAD_DIM
    return pl.pallas_call(
        _dsa_pv_kernel,
        grid_spec=pltpu.PrefetchScalarGridSpec(
            num_scalar_prefetch=1,
            grid=(bd, n_pages // SMALL_PAGES),
            in_specs=[pl.BlockSpec((N_HEADS_A, N_BUCKETS), lambda b, g, pt: (0, 0)),
                      per_b((n_pages, N_HEADS_A, LANES)), per_b((N_HEADS_A, kv)), per_b((1, 1)),
                      per_b((1, kv)), per_b((kv, 1))] + _page_specs((kv, Q_BLOCK), n_pages, False),
            out_specs=per_b((HEAD_DIM, N_HEADS_A)),
            scratch_shapes=[pltpu.VMEM((N_HEADS_A, 1), F32), pltpu.VMEM((N_HEADS_A, LANES), F32),
                            pltpu.VMEM((kv, N_HEADS_A), F32)],
        ),
        out_shape=jax.ShapeDtypeStruct((bd, HEAD_DIM, N_HEADS_A), F32),
        compiler_params=_cparams(2),
        name="dsa_pv",
    )(page_table, bias_table.T, logits, qz, maddn, k_new, v_new_col, *([cache_vt] * SMALL_PAGES))


def _sb_sample_kernel(pt_ref, qbd_ref, kn_ref, vnc_ref, w_ref, *refs, past_len):
    k_refs = refs[:SB_PAGES]
    v_refs = refs[SB_PAGES:2 * SB_PAGES]
    o_ref, run_scr, acc_scr = refs[2 * SB_PAGES:]
    g = pl.program_id(1)
    n_steps = pl.num_programs(1)
    n_pages = n_steps * SB_PAGES
    qbd = qbd_ref[0]
    qbd16 = qbd.astype(BF16)
    lane = lax.broadcasted_iota(I32, (1, LANES), 1)
    q_pos = past_len

    @pl.when(g == 0)
    def _():
        pos = past_len + lax.broadcasted_iota(I32, (N_HEADS_B, 1), 1)
        keep = pos < q_pos
        z = jnp.sum(qbd * kn_ref[0], axis=1, keepdims=True)
        lk = jnp.where(keep, _log_keep(z), 0.0)
        a = jnp.where(keep, jnp.exp(z + lk), 0.0)
        run_scr[...] = jnp.broadcast_to(lk, run_scr.shape)
        acc_scr[...] = vnc_ref[0] * _col_to_row(a)

    run = run_scr[...]
    for p in range(SB_PAGES):
        page = n_pages - 1 - (g * SB_PAGES + p)
        keep = (page * Q_BLOCK + lane) < q_pos
        z = jnp.dot(qbd16, k_refs[p][0].astype(BF16), preferred_element_type=F32)
        lk = jnp.where(keep, _log_keep(z), 0.0)
        res = jnp.dot(_split_hi_lo(lk), w_ref[...], preferred_element_type=F32)
        a = jnp.where(keep, jnp.exp(z + lk + res[:, :LANES] + run), 0.0)
        acc_scr[...] += lax.dot_general(v_refs[p][0].astype(BF16), a.astype(BF16), _NT,
                                        preferred_element_type=F32)
        run = run + res[:, LANES:]
    run_scr[...] = run

    @pl.when(g == n_steps - 1)
    def _():
        r = lax.broadcasted_iota(I32, acc_scr.shape, 0)
        h = lax.broadcasted_iota(I32, acc_scr.shape, 1)
        o_ref[0] = jnp.sum(jnp.where(_div_pow2(r, HEAD_DIM) == h, acc_scr[...], 0.0), axis=1, keepdims=True)


def _sb_sample(page_table, qbd, k_new, v_new_col, cache_kt, cache_vt, past_len):
    bd, n_pages = page_table.shape
    width = cache_kt.shape[1]
    wmat = _suffix_matrix(1)
    per_b = lambda shape: pl.BlockSpec((1,) + shape, lambda b, g, pt: (b,) + (0,) * len(shape))
    kv_specs = _page_specs((width, Q_BLOCK), n_pages, True, SB_PAGES)
    return pl.pallas_call(
        functools.partial(_sb_sample_kernel, past_len=past_len),
        grid_spec=pltpu.PrefetchScalarGridSpec(
            num_scalar_prefetch=1,
            grid=(bd, n_pages // SB_PAGES),
            in_specs=[per_b((N_HEADS_B, width)), per_b((1, width)), per_b((width, 1)),
                      pl.BlockSpec(wmat.shape, lambda b, g, pt: (0, 0))] + kv_specs + kv_specs,
            out_specs=per_b((width, 1)),
            scratch_shapes=[pltpu.VMEM((N_HEADS_B, LANES), F32), pltpu.VMEM((width, N_HEADS_B), F32)],
        ),
        out_shape=jax.ShapeDtypeStruct((bd, width, 1), F32),
        compiler_params=_cparams(2),
        name="sb_sample",
    )(page_table, qbd, k_new, v_new_col, wmat, *([cache_kt] * SB_PAGES), *([cache_vt] * SB_PAGES))


def _route(logits, axis):
    slot = lax.broadcasted_iota(I32, logits.shape, axis)
    big = jnp.int32(ROUTER_LANES)

    def softmax_over(mask):
        m = jnp.max(jnp.where(mask, logits, NEG), axis=axis, keepdims=True)
        e = jnp.where(mask, jnp.exp(logits - m), 0.0)
        return e / jnp.sum(e, axis=axis, keepdims=True)

    def top1(prob, mask):
        best = jnp.max(jnp.where(mask, prob, -1.0), axis=axis, keepdims=True)
        idx = jnp.min(jnp.where(mask & (prob == best), slot, big), axis=axis, keepdims=True)
        return best, idx

    is_group = slot < N_GROUPS
    g_w, g_idx = top1(softmax_over(is_group), is_group)
    e_slot = slot - N_GROUPS
    in_group = (e_slot >= 0) & (e_slot < N_EXPERTS) & (_div_pow2(e_slot, EXPERTS_PER_GROUP) == g_idx)
    p_e = softmax_over(in_group)
    p1, i1 = top1(p_e, in_group)
    p2, i2 = top1(p_e, in_group & (slot != i1))
    denom = p1 + p2
    return slot, i1, p1 / denom * g_w, i2, p2 / denom * g_w


def _merge_kernel(x_ref, oa_ref, ob_ref, ga_ref, gb_ref, wpa, wpb, wout, ln2_ref, wr, br,
                  h_o, hn_o, route_o, *, grouped):
    mixed = (ga_ref[...] * jnp.dot(oa_ref[...], wpa[...], preferred_element_type=F32)
             + gb_ref[...] * jnp.dot(ob_ref[...], wpb[...], preferred_element_type=F32))
    h = x_ref[...] + jnp.dot(mixed.astype(BF16), wout[...], preferred_element_type=F32)
    h_o[...] = h
    ms = jnp.mean(h * h, axis=-1, keepdims=True)
    hn = (h * lax.rsqrt(ms + EPS) * ln2_ref[...]).astype(BF16)

    if grouped:
        half = hn.shape[1] // 2
        hi = lax.bitcast_convert_type(hn[:, :half].astype(F32), jnp.uint32)
        lo = lax.bitcast_convert_type(hn[:, half:].astype(F32), jnp.uint32)
        hn_o[...] = hi | (lo >> 16)
        logits_t = lax.dot_general(wr[...], hn, _NT, preferred_element_type=F32) + br[...]
        _, i1, w1, i2, w2 = _route(logits_t, 0)
        rows = [(i1 - N_GROUPS).astype(F32), (i2 - N_GROUPS).astype(F32), w1, w2]
        route_o[...] = jnp.concatenate(rows + [jnp.zeros_like(w1)] * (route_o.shape[0] - len(rows)), axis=0)
    else:
        hn_o[...] = hn
        logits = jnp.dot(hn, wr[...], preferred_element_type=F32) + br[...]
        slot, i1, w1, i2, w2 = _route(logits, 1)
        route_o[...] = jnp.where(slot == i1, w1, 0.0) + jnp.where(slot == i2, w2, 0.0)


def _merge(x2d, o_a, o_b, g_a, g_b, wpa, wpb, wout, ln2, wr, br, tm, grouped):
    n, d = x2d.shape
    row = lambda w: pl.BlockSpec((tm, w), lambda i: (i, 0))
    if grouped:
        consts = (wpa, wpb, wout, ln2, wr.T, br.T)
        hn_spec, hn_shape = row(d // 2), jax.ShapeDtypeStruct((n, d // 2), jnp.uint32)
        rt_spec = pl.BlockSpec((ROUTE_ROWS, tm), lambda i: (0, i))
        rt_shape = jax.ShapeDtypeStruct((ROUTE_ROWS, n), F32)
    else:
        consts = (wpa, wpb, wout, ln2, wr, br)
        hn_spec, hn_shape = row(d), jax.ShapeDtypeStruct((n, d), BF16)
        rt_spec, rt_shape = row(ROUTER_LANES), jax.ShapeDtypeStruct((n, ROUTER_LANES), F32)
    return pl.pallas_call(
        functools.partial(_merge_kernel, grouped=grouped),
        grid=(n // tm,),
        in_specs=[row(d), row(WIDTH_A), row(WIDTH_B), row(d), row(d)] + [_const_spec(c.shape) for c in consts],
        out_specs=[row(d), hn_spec, rt_spec],
        out_shape=[jax.ShapeDtypeStruct((n, d), F32), hn_shape, rt_shape],
        compiler_params=_cparams(1),
        name="merge",
    )(x2d, o_a, o_b, g_a, g_b, *consts)


def _moe_kernel(hn_ref, comb_ref, h_ref, wgu_ref, wd_ref, lnf_ref, y_ref, acc_scr, *, final_norm):
    e = pl.program_id(1)

    @pl.when(e == 0)
    def _():
        acc_scr[...] = jnp.zeros(acc_scr.shape, F32)

    d_expert = wd_ref.shape[1]
    gu = jnp.dot(hn_ref[...], wgu_ref[0], preferred_element_type=F32)
    gate, up = gu[:, :d_expert], gu[:, d_expert:]
    comb = comb_ref[...]
    lane = lax.broadcasted_iota(I32, comb.shape, 1)
    c_e = jnp.sum(jnp.where(lane == e + N_GROUPS, comb, 0.0), axis=1, keepdims=True)
    hid = gate * _sigmoid(gate) * up * c_e
    acc_scr[...] += jnp.dot(hid.astype(BF16), wd_ref[0], preferred_element_type=F32)

    @pl.when(e == pl.num_programs(1) - 1)
    def _():
        xo = h_ref[...] + acc_scr[...]
        if final_norm:
            ms = jnp.mean(xo * xo, axis=-1, keepdims=True)
            xo = xo * lax.rsqrt(ms + EPS) * lnf_ref[...]
        y_ref[...] = xo


def _moe(hn, comb, h, wgu, wd, lnf, tm, final_norm):
    n, d = h.shape
    n_exp, _, two_f = wgu.shape
    row = lambda w: pl.BlockSpec((tm, w), lambda i, e: (i, 0))
    return pl.pallas_call(
        functools.partial(_moe_kernel, final_norm=final_norm),
        grid=(n // tm, n_exp),
        in_specs=[row(d), row(ROUTER_LANES), row(d),
                  pl.BlockSpec((1, d, two_f), lambda i, e: (e, 0, 0)),
                  pl.BlockSpec((1, two_f // 2, d), lambda i, e: (e, 0, 0)),
                  pl.BlockSpec((1, d), lambda i, e: (0, 0))],
        out_specs=row(d),
        out_shape=jax.ShapeDtypeStruct((n, d), F32),
        scratch_shapes=[pltpu.VMEM((tm, d), F32)],
        compiler_params=_cparams(2),
        name="moe",
    )(hn, comb, h, wgu, wd, lnf)


def _moe_plan_kernel(rt_ref, before_ref, lower_ref, pos_o, tt_o):
    ch = rt_ref.shape[1]
    e1 = rt_ref[0:1, :].astype(I32)
    e2 = rt_ref[1:2, :].astype(I32)
    sub = lax.broadcasted_iota(I32, (ROUTER_LANES, ch), 0)
    a1, a2 = sub == e1, sub == e2
    chosen = (jnp.where(a1, 1.0, 0.0) + jnp.where(a2, 1.0, 0.0)).astype(BF16)
    before = jnp.dot(chosen, before_ref[...], preferred_element_type=F32)
    cnt = before[:, ch - 1:ch] + chosen[:, ch - 1:ch].astype(F32)
    ntile = jnp.floor((cnt + (MOE_ROWS - 1)) * (1.0 / MOE_ROWS))
    base_t = jnp.dot(lower_ref[...], jnp.broadcast_to(ntile, (ROUTER_LANES, LANES)).astype(BF16),
                     preferred_element_type=F32)[:, 0:1]
    slot = before + base_t * MOE_ROWS
    pos = [jnp.sum(jnp.where(a, slot, 0.0), axis=0, keepdims=True) for a in (a1, a2)]
    pos_o[0] = jnp.concatenate(pos + [jnp.zeros_like(pos[0])] * (pos_o.shape[1] - 2), axis=0).astype(I32)

    tile = lax.broadcasted_iota(I32, (ROUTER_LANES, LANES), 1).astype(F32)
    exp = lax.broadcasted_iota(I32, (ROUTER_LANES, LANES), 0)
    is_exp = exp < N_EXPERTS
    te = jnp.sum(jnp.where(is_exp & (base_t + ntile <= tile), 1, 0), axis=0, keepdims=True)
    total = jnp.sum(jnp.where(is_exp, ntile, 0.0), axis=0, keepdims=True).astype(I32) + jnp.zeros_like(te)
    rows = [jnp.minimum(te, N_EXPERTS - 1), total]
    tt_o[0] = jnp.concatenate(rows + [jnp.zeros_like(te)] * (tt_o.shape[1] - len(rows)), axis=0)


def _moe_routed_kernel(tt_ref, pos_ref, rt_ref, hnp_ref, h_ref, wgu_ref, wd_ref, lnf_ref, y_ref,
                       rows_scr, *, max_tiles, final_norm):
    c = pl.program_id(0)
    i = pl.program_id(1)
    ch, half = hnp_ref.shape
    n_tiles = tt_ref[(c * 2 + 1) * max_tiles]

    @pl.when(i == 0)
    def _():
        rows_scr[:, :half] = jnp.zeros((rows_scr.shape[0], half), F32)

        def place(t, carry):
            row = lax.bitcast_convert_type(hnp_ref[pl.ds(t, 1), :], F32)
            for k in range(TOP_E):
                rows_scr[pl.ds(pos_ref[0, k, t], 1), :half] = row
            return carry

        lax.fori_loop(0, ch, place, 0, unroll=MOE_UNROLL)

    @pl.when(i < n_tiles)
    def _():
        tile = pl.ds(pl.multiple_of(i * MOE_ROWS, MOE_ROWS), MOE_ROWS)
        packed = lax.bitcast_convert_type(rows_scr[tile, :half], jnp.uint32)
        first = lax.bitcast_convert_type(packed & jnp.uint32(0xFFFF0000), F32).astype(BF16)
        second = lax.bitcast_convert_type(packed << 16, F32).astype(BF16)
        x = jnp.concatenate([first, second], axis=1)
        d_expert = wd_ref.shape[1]
        gu = jnp.dot(x, wgu_ref[0], preferred_element_type=F32)
        gate, up = gu[:, :d_expert], gu[:, d_expert:]
        hid = gate * _sigmoid(gate) * up
        rows_scr[tile, :] = jnp.dot(hid.astype(BF16), wd_ref[0], preferred_element_type=F32)

    @pl.when(i == max_tiles - 1)
    def _():
        def combine(t, carry):
            acc = h_ref[pl.ds(t, 1), :]
            for k in range(TOP_E):
                acc = acc + rt_ref[TOP_E + k, t] * rows_scr[pl.ds(pos_ref[0, k, t], 1), :]
            y_ref[pl.ds(t, 1), :] = acc
            return carry

        lax.fori_loop(0, ch, combine, 0, unroll=MOE_UNROLL)
        if final_norm:
            xo = y_ref[...]
            ms = jnp.mean(xo * xo, axis=-1, keepdims=True)
            y_ref[...] = xo * lax.rsqrt(ms + EPS) * lnf_ref[...]


def _moe_routed(hnp, rt, h, wgu, wd, lnf, final_norm):
    n, d = h.shape
    ch = MOE_CHUNK
    assert n % ch == 0
    n_chunks = n // ch
    max_tiles = TOP_E * ch // MOE_ROWS + N_EXPERTS
    assert max_tiles <= LANES
    before = jnp.asarray((np.arange(ch)[:, None] < np.arange(ch)[None, :]).astype(np.float32), BF16)
    lower = jnp.asarray((np.arange(ROUTER_LANES)[:, None] > np.arange(ROUTER_LANES)[None, :]).astype(np.float32),
                        BF16)
    pos, tt = pl.pallas_call(
        _moe_plan_kernel,
        grid=(n_chunks,),
        in_specs=[pl.BlockSpec((ROUTE_ROWS, ch), lambda c: (0, c)), _const_spec(before.shape),
                  _const_spec(lower.shape)],
        out_specs=[pl.BlockSpec((1, ROUTE_ROWS, ch), lambda c: (c, 0, 0)),
                   pl.BlockSpec((1, ROUTE_ROWS, LANES), lambda c: (c, 0, 0))],
        out_shape=[jax.ShapeDtypeStruct((n_chunks, ROUTE_ROWS, ch), I32),
                   jax.ShapeDtypeStruct((n_chunks, ROUTE_ROWS, LANES), I32)],
        compiler_params=_cparams(1),
        name="moe_plan",
    )(rt, before, lower)
    tt_flat = tt[:, :2, :max_tiles].reshape(-1)
    n_exp, _, two_f = wgu.shape
    expert_of = lambda c, i, tt: (tt[c * 2 * max_tiles + i], 0, 0)
    return pl.pallas_call(
        functools.partial(_moe_routed_kernel, max_tiles=max_tiles, final_norm=final_norm),
        grid_spec=pltpu.PrefetchScalarGridSpec(
            num_scalar_prefetch=1,
            grid=(n_chunks, max_tiles),
            in_specs=[pl.BlockSpec((1, ROUTE_ROWS, ch), lambda c, i, tt: (c, 0, 0), memory_space=pltpu.SMEM),
                      pl.BlockSpec((ROUTE_ROWS, ch), lambda c, i, tt: (0, c), memory_space=pltpu.SMEM),
                      pl.BlockSpec((ch, d // 2), lambda c, i, tt: (c, 0)),
                      pl.BlockSpec((ch, d), lambda c, i, tt: (c, 0)),
                      pl.BlockSpec((1, d, two_f), expert_of),
                      pl.BlockSpec((1, two_f // 2, d), expert_of),
                      pl.BlockSpec((1, d), lambda c, i, tt: (0, 0))],
            out_specs=pl.BlockSpec((ch, d), lambda c, i, tt: (c, 0)),
            scratch_shapes=[pltpu.VMEM((max_tiles * MOE_ROWS, d), F32)],
        ),
        out_shape=jax.ShapeDtypeStruct((n, d), F32),
        compiler_params=_cparams(2),
        name="moe_routed",
    )(tt_flat, pos, rt, hnp, h, wgu, wd, lnf)


def _tile(n, pref):
    return pref if n % pref == 0 else n


def kernel(x_prompt, x_sample, cache_k_a, cache_v_a, cache_kidx_a, cache_k_b, cache_v_b, page_table,
           rel_bias_table, ln1_g, w_in, w_proj_a, w_proj_b, w_out, ln2_g, w_router_group, b_router_group,
           w_router_expert, b_router_expert, w_gate, w_up, w_down, ln_f_g):
    bsz, seq, d = x_prompt.shape
    bd, t_new, _ = x_sample.shape
    assert t_new == 1, "the sample group decodes one token per sequence"
    depth = w_in.shape[0]
    pool = cache_k_a.shape[1]
    n_pages = page_table.shape[1]
    past_len = n_pages * Q_BLOCK
    assert n_pages % SMALL_PAGES == 0 and n_pages % SB_PAGES == 0
    n_p, n_s = bsz * seq, bd * t_new
    d_expert = w_gate.shape[-1]

    xp, xs = x_prompt.reshape(n_p, d), x_sample.reshape(n_s, d)
    new_p, new_s = [], []
    for l in range(depth):
        w_pieces = _split_w_in(w_in[l])
        wpa, wpb, wout = (w.astype(BF16) for w in (w_proj_a[l], w_proj_b[l], w_out[l]))
        ln2 = ln2_g[l].reshape(1, d)
        wr = jnp.concatenate([w_router_group[l], jnp.moveaxis(w_router_expert[l], 0, 1).reshape(d, N_EXPERTS),
                              jnp.zeros((d, ROUTER_LANES - N_GROUPS - N_EXPERTS), F32)], axis=1).astype(BF16)
        br = jnp.concatenate([b_router_group[l], b_router_expert[l].reshape(N_EXPERTS),
                              jnp.zeros((ROUTER_LANES - N_GROUPS - N_EXPERTS,), F32)]).reshape(1, ROUTER_LANES)
        wgu = jnp.concatenate([w_gate[l], w_up[l]], axis=-1).reshape(N_EXPERTS, d, 2 * d_expert).astype(BF16)
        wd = w_down[l].reshape(N_EXPERTS, d_expert, d).astype(BF16)
        lnf = ln_f_g.reshape(1, d)

        def ffn(x2d, o_a, o_b, g_a, g_b, tm_merge, tm_moe):
            final_norm = l == depth - 1
            routed = x2d.shape[0] % MOE_CHUNK == 0
            h, hn, route = _merge(x2d, o_a, o_b, g_a, g_b, wpa, wpb, wout, ln2, wr, br, tm_merge, grouped=routed)
            if routed:
                return _moe_routed(hn, route, h, wgu, wd, lnf, final_norm)
            return _moe(hn, route, h, wgu, wd, lnf, tm_moe, final_norm)

        (qa, ka, va, kad, vad, qi, ki, kid, wi, qb, kb, vb, kb16, vb16, ga, gb) = _project(
            xp, ln1_g[l], w_pieces, _tile(seq, 512), seq_len=seq)
        r3 = lambda a: a.reshape(bsz, seq, a.shape[-1])
        n_top = min(TOPK_MAX, seq // 4)
        wit = jnp.swapaxes(r3(wi), 1, 2)
        vadt = jnp.swapaxes(vad.reshape(bsz, seq // Q_BLOCK, Q_BLOCK, 2 * LANES), 2, 3)
        o_at = _dsa_prompt(rel_bias_table, r3(qi), wit, r3(kid), r3(qa), r3(kad), vadt, n_top)
        o_a = jnp.swapaxes(o_at, 1, 2)
        o_b = _sb_prompt(r3(qb), r3(kb16), r3(vb16))
        xp = ffn(xp, o_a.reshape(n_p, WIDTH_A), o_b.reshape(n_p, WIDTH_B), ga, gb, _tile(n_p, 512), _tile(n_p, 1024))
        new_p.append((ka, va, ki, kb, vb))

        (qa, ka, va, _, _, qi, ki, _, wi, qb, kb, vb, _, _, ga, gb) = _project(xs, ln1_g[l], w_pieces, n_s)
        n_top = min(TOPK_MAX, (past_len + t_new) // 4)
        qi3 = qi.astype(F32).reshape(bd, N_IDX_HEADS, IDX_DIM)
        wi3 = wi.reshape(bd, N_IDX_HEADS, 1)
        paged_t = lambda c: jnp.moveaxis(c, 1, -1).reshape(pool, -1, Q_BLOCK)
        scores = _idx_sample(page_table, qi3, wi3, paged_t(cache_kidx_a[l]))
        madd, maddn = _select_sample(scores.reshape(bd, n_pages, LANES), qi3, wi3, ki.reshape(bd, 1, IDX_DIM), n_top)
        qa4 = qa.astype(F32).reshape(bd, N_HEADS_A, 1, HEAD_DIM) * (HEAD_DIM ** -0.5)
        kv_of_head = (np.arange(N_HEADS_A)[:, None] // GROUP_A) == np.arange(N_KV_A)[None, :]
        qz = jnp.where(kv_of_head[None, :, :, None], qa4, 0.0).reshape(bd, N_HEADS_A, N_KV_A * HEAD_DIM)
        logits = _dsa_logits(page_table, rel_bias_table, qz, madd.reshape(bd, n_pages, 1, LANES),
                             paged_t(cache_k_a[l]), past_len)
        o_a = _dsa_pv(page_table, rel_bias_table, logits, qz, maddn, ka.reshape(bd, 1, -1), va.reshape(bd, -1, 1),
                      paged_t(cache_v_a[l]))
        o_a = jnp.swapaxes(o_a, 1, 2)
        qb4 = qb.astype(F32).reshape(bd, N_HEADS_B, 1, HEAD_DIM) * (HEAD_DIM ** -0.5)
        own = np.eye(N_HEADS_B, dtype=bool)
        qbd = jnp.where(own[None, :, :, None], qb4, 0.0).reshape(bd, N_HEADS_B, WIDTH_B)
        o_b = _sb_sample(page_table, qbd, kb.reshape(bd, 1, -1), vb.reshape(bd, -1, 1),
                         paged_t(cache_k_b[l]), paged_t(cache_v_b[l]), past_len)
        xs = ffn(xs, o_a.reshape(n_s, WIDTH_A).astype(BF16), o_b.reshape(n_s, WIDTH_B).astype(BF16), ga, gb, n_s, n_s)
        new_s.append((ka, va, ki, kb, vb))

    outs = [xp.reshape(bsz, seq, d), xs.reshape(bd, t_new, d)]
    tails = ((N_KV_A, HEAD_DIM), (N_KV_A, HEAD_DIM), (IDX_DIM,), (N_HEADS_B, HEAD_DIM), (N_HEADS_B, HEAD_DIM))
    for idx, tail in enumerate(tails):
        outs.append(jnp.stack([jnp.moveaxis(r[idx].reshape((bsz,) + tail + (seq,)), -1, 1) for r in new_p]))
    for idx, tail in enumerate(tails):
        outs.append(jnp.stack([r[idx].reshape((bd, t_new) + tail) for r in new_s]))
    return tuple(outs)
```

```python
import functools
import math

import jax
import jax.numpy as jnp
import numpy as np
from jax import lax
from jax.experimental import pallas as pl
from jax.experimental.pallas import tpu as pltpu

F32 = jnp.float32
BF16 = jnp.bfloat16
I32 = jnp.int32

HEAD_DIM = 64
N_HEADS_A = 8
N_KV_A = 2
GROUP_A = N_HEADS_A // N_KV_A
N_IDX_HEADS = 8
IDX_DIM = 64
N_HEADS_B = 8
WIDTH_A = N_HEADS_A * HEAD_DIM
WIDTH_B = N_HEADS_B * HEAD_DIM
TOPK_MAX = 256
N_BUCKETS = 32
MAX_DISTANCE = 128
N_GROUPS = 4
EXPERTS_PER_GROUP = 8
N_EXPERTS = N_GROUPS * EXPERTS_PER_GROUP
TOP_E = 2
Q_BLOCK = 128
EPS = 1e-6

Q_TILE = 256

LANES = 128
VMEM_LIMIT_BYTES = 56 * 1024 * 1024

SMALL_PAGES = 32
SB_PAGES = 16
SELECT_SEQS = 4
MOE_CHUNK = 1024
MOE_ROWS = 128
MOE_UNROLL = 8
ROUTE_ROWS = 8
NEG = -1e30
INT_MIN = -(2 ** 31)
ROUTER_LANES = LANES

_NT = (((1,), (1,)), ((), ()))


def _cparams(n_axes):
    return pltpu.CompilerParams(dimension_semantics=("arbitrary",) * n_axes,
                                vmem_limit_bytes=VMEM_LIMIT_BYTES)


def _const_spec(shape):
    nd = len(shape)
    return pl.BlockSpec(shape, lambda *_: (0,) * nd)


def _div_pow2(x, n):
    assert n & (n - 1) == 0
    return x >> (n.bit_length() - 1)


def _sigmoid(x):
    return 1.0 / (1.0 + jnp.exp(-x))


def _half_masks(dtype):
    lane = lax.broadcasted_iota(I32, (LANES, LANES), 1)
    lo = jnp.where(lane < HEAD_DIM, 1.0, 0.0).astype(dtype)
    hi = jnp.where(lane < HEAD_DIM, 0.0, 1.0).astype(dtype)
    return lo, hi


def _pair_block_diag(blk, lo, hi):
    return jnp.concatenate([blk * lo, blk * hi], axis=0)


def _row_half_masks(dtype):
    r = lax.broadcasted_iota(I32, (LANES, LANES), 0)
    top = jnp.where(r < HEAD_DIM, 1.0, 0.0).astype(dtype)
    bottom = jnp.where(r < HEAD_DIM, 0.0, 1.0).astype(dtype)
    return top, bottom


def _pair_block_diag_t(blk_t, top, bottom):
    return jnp.concatenate([blk_t * top, blk_t * bottom], axis=1)


def _fori_groups(n_groups, group, body, init):
    def trip(o, carry):
        for u in range(group):
            carry = body(o * group + u, carry)
        return carry
    return lax.fori_loop(0, n_groups, trip, init)


def _col_to_row(x):
    n = x.shape[0]
    eye = lax.broadcasted_iota(I32, (n, n), 0) == lax.broadcasted_iota(I32, (n, n), 1)
    return jnp.sum(jnp.where(eye, x, 0.0), axis=0, keepdims=True)


def _order_key(score):
    score = jnp.where(score == 0.0, 0.0, score)
    b = lax.bitcast_convert_type(score, I32)
    return b ^ ((b >> 31) & jnp.int32(0x7FFFFFFF))


def _t5_bucket(dist):
    n = jnp.maximum(dist, 0)
    max_exact = N_BUCKETS // 2
    nf = jnp.maximum(n, max_exact).astype(F32)
    large = max_exact + (jnp.log(nf / max_exact) / math.log(MAX_DISTANCE / max_exact)
                         * (N_BUCKETS - max_exact)).astype(I32)
    return jnp.where(n < max_exact, n, jnp.minimum(large, N_BUCKETS - 1))


def _proj_kernel(x_ref, g_ref, wqa, wkv, wkvd, wqi, wkiw, wqb, wkb, wvb, wga, wgb,
                 qa_o, ka_o, va_o, kad_o, vad_o, qi_o, ki_o, kid_o, wi_o,
                 qb_o, kb_o, vb_o, kb16_o, vb16_o, ga_o, gb_o, *, feature_major):
    x = x_ref[...]
    ms = jnp.mean(x * x, axis=-1, keepdims=True)
    xn = (x * lax.rsqrt(ms + EPS) * g_ref[...]).astype(BF16)

    def mm(w):
        return jnp.dot(xn, w[...], preferred_element_type=F32)

    def put_new_rows(o_ref, val):
        if feature_major:
            o_ref[0] = val.T
        else:
            o_ref[...] = val

    qa_o[...] = mm(wqa).astype(BF16)
    kv = mm(wkv)
    put_new_rows(ka_o, kv[:, :N_KV_A * HEAD_DIM])
    put_new_rows(va_o, kv[:, N_KV_A * HEAD_DIM:])
    kvd = mm(wkvd).astype(BF16)
    kad_o[...] = kvd[:, :2 * LANES]
    vad_o[...] = kvd[:, 2 * LANES:]
    qi_o[...] = mm(wqi).astype(BF16)
    kiw = mm(wkiw)
    put_new_rows(ki_o, kiw[:, :IDX_DIM])
    kid_o[...] = kiw[:, :LANES].astype(BF16)
    wi_o[...] = kiw[:, LANES:LANES + N_IDX_HEADS] * (N_IDX_HEADS ** -0.5)
    qb_o[...] = mm(wqb).astype(BF16)
    kb = mm(wkb)
    put_new_rows(kb_o, kb)
    kb16_o[...] = kb.astype(BF16)
    vb = mm(wvb)
    put_new_rows(vb_o, vb)
    vb16_o[...] = vb.astype(BF16)
    ga_o[...] = _sigmoid(mm(wga))
    gb_o[...] = _sigmoid(mm(wgb))


def _split_w_in(w_in):
    d_model = w_in.shape[0]
    widths = (WIDTH_A, N_KV_A * HEAD_DIM, N_KV_A * HEAD_DIM, N_IDX_HEADS * IDX_DIM, N_IDX_HEADS, IDX_DIM,
              WIDTH_B, WIDTH_B, WIDTH_B, d_model, d_model)
    points = [int(p) for p in np.cumsum(widths)[:-1]]
    qa, ka, va, qi, wi, ki, qb, kb, vb, ga, gb = jnp.split(w_in, points, axis=1)
    dup = lambda a: jnp.concatenate([a[:, :HEAD_DIM], a[:, :HEAD_DIM], a[:, HEAD_DIM:], a[:, HEAD_DIM:]], axis=1)
    wkv = jnp.concatenate([ka, va], axis=1)
    wkvd = jnp.concatenate([dup(ka), dup(va)], axis=1)
    wkiw = jnp.concatenate([ki, ki, wi, jnp.zeros((d_model, LANES - N_IDX_HEADS), w_in.dtype)], axis=1)
    return tuple(a.astype(BF16) for a in (qa, wkv, wkvd, qi, wkiw, qb, kb, vb, ga, gb))


_NEW_ROW_OUTPUTS = (1, 2, 6, 10, 11)


def _project(x2d, ln_g, w_pieces, tm, seq_len=None):
    n, d = x2d.shape
    assert n % tm == 0
    out_defs = [
        (WIDTH_A, BF16), (N_KV_A * HEAD_DIM, F32), (N_KV_A * HEAD_DIM, F32), (2 * LANES, BF16), (2 * LANES, BF16),
        (N_IDX_HEADS * IDX_DIM, BF16), (IDX_DIM, F32), (LANES, BF16), (N_IDX_HEADS, F32),
        (WIDTH_B, BF16), (WIDTH_B, F32), (WIDTH_B, F32), (WIDTH_B, BF16), (WIDTH_B, BF16), (d, F32), (d, F32)]
    row = lambda i: (i, 0)
    out_specs = [pl.BlockSpec((tm, w), row) for w, _ in out_defs]
    out_shape = [jax.ShapeDtypeStruct((n, w), dt) for w, dt in out_defs]
    if seq_len is not None:
        assert seq_len % tm == 0
        tiles = seq_len // tm
        for k in _NEW_ROW_OUTPUTS:
            w, dt = out_defs[k]
            out_specs[k] = pl.BlockSpec((1, w, tm), lambda i: (i // tiles, 0, i % tiles))
            out_shape[k] = jax.ShapeDtypeStruct((n // seq_len, w, seq_len), dt)
    return pl.pallas_call(
        functools.partial(_proj_kernel, feature_major=seq_len is not None),
        grid=(n // tm,),
        in_specs=[pl.BlockSpec((tm, d), row), _const_spec((1, d))] + [_const_spec(w.shape) for w in w_pieces],
        out_specs=out_specs,
        out_shape=out_shape,
        compiler_params=_cparams(1),
        name="proj",
    )(x2d, ln_g.reshape(1, d), *w_pieces)


def _dsa_prompt_kernel(tab_ref, qi_ref, wit_ref, kid_ref, qa_ref, kad_ref, vadt_ref, tril_ref, o_ref,
                       key_scr, lg_scr, bias_scr, acc_scr, *, n_top):
    b = pl.program_id(0)
    i = pl.program_id(1)
    kb, qt = Q_BLOCK, Q_TILE
    bpt = qt // kb
    row = lax.broadcasted_iota(I32, (kb, qt), 0)
    col = lax.broadcasted_iota(I32, (kb, qt), 1)
    lo, hi = _half_masks(BF16)
    top, bottom = _row_half_masks(BF16)
    d_min = 1 - bpt

    def block_dist(j):
        return i * bpt - j

    @pl.when((b == 0) & (i == 0))
    def _():
        for d in range(d_min, 3):
            bucket = _t5_bucket(d * kb + col - row)
            for h in range(N_HEADS_A):
                def sel(bk, acc):
                    return jnp.where(bucket == bk, tab_ref[bk, h], acc)
                tile = lax.fori_loop(0, N_BUCKETS, sel, jnp.zeros((kb, qt), F32))
                bias_scr[d - d_min, h // 2, (h % 2) * kb:(h % 2 + 1) * kb, :] = tile

    def causal(j):
        return block_dist(j) * kb + col - row >= 0

    w = wit_ref[0] * (IDX_DIM ** -0.5)
    wrows = [w[h:h + 1, :] for h in range(N_IDX_HEADS)]

    def score_body(j, carry):
        kblk = kid_ref[0, pl.ds(pl.multiple_of(j * kb, kb), kb), :]
        kbd = _pair_block_diag(kblk, lo, hi)
        acc = jnp.zeros((kb, qt), F32)
        for p in range(N_IDX_HEADS // 2):
            r = lax.dot_general(kbd, qi_ref[0, :, p * LANES:(p + 1) * LANES], _NT,
                                preferred_element_type=F32)
            acc = acc + jnp.maximum(r[:kb], 0.0) * wrows[2 * p] + jnp.maximum(r[kb:], 0.0) * wrows[2 * p + 1]
        key_scr[j] = _order_key(jnp.where(causal(j), acc, -jnp.inf))
        return carry

    _fori_groups(i + 1, bpt, score_body, 0)

    def count(pred):
        def body(j, acc):
            return acc + jnp.where(pred(key_scr[j]), 1, 0)
        acc = _fori_groups(i + 1, bpt, body, jnp.zeros((kb, qt), I32))
        return jnp.sum(acc, axis=0, keepdims=True)

    thr = jnp.where(count(lambda k: k >= 0) >= n_top, 0, INT_MIN).astype(I32)

    def bit_body(t, thr):
        cand = thr | lax.shift_left(jnp.int32(1), jnp.int32(30) - t)
        return jnp.where(count(lambda k: k >= cand) >= n_top, cand, thr)

    thr = lax.fori_loop(0, 31, bit_body, thr)
    need = (n_top - count(lambda k: k > thr)).astype(F32)

    tril = tril_ref[...]

    n_pairs = N_HEADS_A // 2
    q_ps = [(qa_ref[0, :, p * LANES:(p + 1) * LANES].astype(F32) * (HEAD_DIM ** -0.5)).astype(BF16)
            for p in range(n_pairs)]

    def logits_body(j, carry):
        ties_before, maxes = carry[0], carry[1:]
        key = key_scr[j]
        eq = key == thr
        pre = jnp.dot(tril, jnp.where(eq, 1.0, 0.0).astype(BF16), preferred_element_type=F32)
        take = (key > thr) | (eq & (ties_before + pre <= need))
        madd = jnp.where(take & causal(j), 0.0, NEG)
        rows = pl.ds(pl.multiple_of(j * kb, kb), kb)
        dsel = jnp.minimum(block_dist(j), 2) - d_min
        out = [ties_before + pre[kb - 1:kb, :]]
        for p in range(n_pairs):
            c = (2 * p) // GROUP_A
            kbd = _pair_block_diag(kad_ref[0, rows, c * LANES:(c + 1) * LANES], lo, hi)
            lg = lax.dot_general(kbd, q_ps[p], _NT, preferred_element_type=F32) + bias_scr[dsel, p]
            for half in range(2):
                lgh = lg[half * kb:(half + 1) * kb] + madd
                lg_scr[j, p, half * kb:(half + 1) * kb, :] = lgh
                out.append(jnp.maximum(maxes[2 * p + half], jnp.max(lgh, axis=0, keepdims=True)))
        return tuple(out)

    neg = jnp.full((1, qt), NEG, F32)
    zero = jnp.zeros((1, qt), F32)
    maxes = _fori_groups(i + 1, bpt, logits_body, (zero,) + (neg,) * N_HEADS_A)[1:]
    acc_scr[...] = jnp.zeros(acc_scr.shape, F32)

    def pv_body(j, sums):
        out = []
        for p in range(n_pairs):
            c = (2 * p) // GROUP_A
            vbd = _pair_block_diag_t(vadt_ref[0, j, c * LANES:(c + 1) * LANES, :], top, bottom)
            p0 = jnp.exp(lg_scr[j, p, :kb, :] - maxes[2 * p])
            p1 = jnp.exp(lg_scr[j, p, kb:, :] - maxes[2 * p + 1])
            out += [sums[2 * p] + jnp.sum(p0, axis=0, keepdims=True),
                    sums[2 * p + 1] + jnp.sum(p1, axis=0, keepdims=True)]
            acc_scr[p] += jnp.dot(vbd, jnp.concatenate([p0, p1], axis=0).astype(BF16),
                                  preferred_element_type=F32)
        return tuple(out)

    sums = _fori_groups(i + 1, bpt, pv_body, (zero,) * N_HEADS_A)
    for p in range(n_pairs):
        denom = jnp.where(row < HEAD_DIM, sums[2 * p], sums[2 * p + 1])
        o_ref[0, p * LANES:(p + 1) * LANES, :] = (acc_scr[p] / denom).astype(o_ref.dtype)


def _dsa_prompt(bias_table, qi, wit, kid, qa, kad, vadt, n_top):
    bsz, seq, _ = qi.shape
    assert seq % Q_TILE == 0 and Q_TILE % Q_BLOCK == 0
    nkb = seq // Q_BLOCK
    n_bias = Q_TILE // Q_BLOCK + 2
    tril = (np.arange(Q_BLOCK)[:, None] >= np.arange(Q_BLOCK)[None, :]).astype(np.float32)
    qblk = lambda w: pl.BlockSpec((1, Q_TILE, w), lambda b, i: (b, i, 0))
    full = lambda w: pl.BlockSpec((1, seq, w), lambda b, i: (b, 0, 0))
    return pl.pallas_call(
        functools.partial(_dsa_prompt_kernel, n_top=n_top),
        grid=(bsz, seq // Q_TILE),
        in_specs=[pl.BlockSpec(memory_space=pltpu.SMEM), qblk(N_IDX_HEADS * IDX_DIM),
                  pl.BlockSpec((1, N_IDX_HEADS, Q_TILE), lambda b, i: (b, 0, i)), full(LANES),
                  qblk(WIDTH_A), full(2 * LANES),
                  pl.BlockSpec((1, nkb, 2 * LANES, Q_BLOCK), lambda b, i: (b, 0, 0, 0)),
                  _const_spec((Q_BLOCK, Q_BLOCK))],
        out_specs=pl.BlockSpec((1, WIDTH_A, Q_TILE), lambda b, i: (b, 0, i)),
        out_shape=jax.ShapeDtypeStruct((bsz, WIDTH_A, seq), BF16),
        scratch_shapes=[pltpu.VMEM((nkb, Q_BLOCK, Q_TILE), I32),
                        pltpu.VMEM((nkb, N_HEADS_A // 2, 2 * Q_BLOCK, Q_TILE), F32),
                        pltpu.VMEM((n_bias, N_HEADS_A // 2, 2 * Q_BLOCK, Q_TILE), F32),
                        pltpu.VMEM((N_HEADS_A // 2, LANES, Q_TILE), F32)],
        compiler_params=_cparams(2),
        name="dsa_prompt",
    )(bias_table, qi, wit, kid, qa, kad, vadt, jnp.asarray(tril, BF16))


def _suffix_matrix(n_heads):
    n = n_heads * Q_BLOCK
    r = np.arange(n)
    same = (r[:, None] // Q_BLOCK) == (r[None, :] // Q_BLOCK)
    u = same & (r[:, None] > r[None, :])
    w = np.concatenate([u, same], axis=1).astype(np.float32)
    return jnp.asarray(np.concatenate([w, w], axis=0), BF16)


def _log_keep(z):
    return -(jnp.maximum(z, 0.0) + jnp.log(1.0 + jnp.exp(-jnp.abs(z))))


def _split_hi_lo(x):
    hi = x.astype(BF16)
    lo = (x - hi.astype(F32)).astype(BF16)
    return jnp.concatenate([hi, lo], axis=1)


def _sb_prompt_kernel(qb_ref, kb_ref, vb_ref, w_ref, o_ref, run_scr, acc_scr):
    i = pl.program_id(1)
    kb, qt = Q_BLOCK, Q_TILE
    bpt = qt // kb
    n_pairs = N_HEADS_B // 2
    row = lax.broadcasted_iota(I32, (qt, 2 * kb), 0)
    col = lax.broadcasted_iota(I32, (qt, 2 * kb), 1) & (kb - 1)
    lo, hi = _half_masks(BF16)
    q_ps = [(qb_ref[0, :, p * LANES:(p + 1) * LANES].astype(F32) * (HEAD_DIM ** -0.5)).astype(BF16)
            for p in range(n_pairs)]

    def block(j, edge, first):
        rows = pl.ds(pl.multiple_of(j * kb, kb), kb)
        if edge:
            strict = j * kb + col < i * qt + row
        for p in range(n_pairs):
            kbd = _pair_block_diag(kb_ref[0, rows, p * LANES:(p + 1) * LANES], lo, hi)
            vbd = _pair_block_diag(vb_ref[0, rows, p * LANES:(p + 1) * LANES], lo, hi)
            z = lax.dot_general(q_ps[p], kbd, _NT, preferred_element_type=F32)
            lk = _log_keep(z)
            if edge:
                lk = jnp.where(strict, lk, 0.0)
            res = jnp.dot(_split_hi_lo(lk), w_ref[...], preferred_element_type=F32)
            x = z + lk + res[:, :2 * kb]
            if not first:
                x = x + run_scr[p]
            a = jnp.exp(x)
            if edge:
                a = jnp.where(strict, a, 0.0)
            pv = jnp.dot(a.astype(BF16), vbd, preferred_element_type=F32)
            if first:
                run_scr[p] = res[:, 2 * kb:]
                acc_scr[p] = pv
            else:
                run_scr[p] += res[:, 2 * kb:]
                acc_scr[p] += pv

    for e in range(bpt):
        block((i + 1) * bpt - 1 - e, True, e == 0)

    def body(t, carry):
        block(i * bpt - 1 - t, False, False)
        return carry

    _fori_groups(i, bpt, body, 0)
    for p in range(n_pairs):
        o_ref[0, :, p * LANES:(p + 1) * LANES] = acc_scr[p].astype(o_ref.dtype)


def _sb_prompt(qb, kb16, vb16):
    bsz, seq, width = qb.shape
    assert seq % Q_TILE == 0 and Q_TILE % Q_BLOCK == 0
    wmat = _suffix_matrix(2)
    qblk = pl.BlockSpec((1, Q_TILE, width), lambda b, i: (b, i, 0))
    full = pl.BlockSpec((1, seq, width), lambda b, i: (b, 0, 0))
    return pl.pallas_call(
        _sb_prompt_kernel,
        grid=(bsz, seq // Q_TILE),
        in_specs=[qblk, full, full, _const_spec(wmat.shape)],
        out_specs=qblk,
        out_shape=jax.ShapeDtypeStruct((bsz, seq, width), BF16),
        scratch_shapes=[pltpu.VMEM((N_HEADS_B // 2, Q_TILE, 2 * Q_BLOCK), F32),
                        pltpu.VMEM((N_HEADS_B // 2, Q_TILE, LANES), F32)],
        compiler_params=_cparams(2),
        name="sb_prompt",
    )(qb, kb16, vb16, wmat)


def _page_specs(block_tail, n_pages, reverse, pages=None):
    pages = pages or SMALL_PAGES

    def spec(p):
        def index_map(b, g, pt):
            page = g * pages + p
            if reverse:
                page = n_pages - 1 - page
            return (pt[b, page],) + (0,) * len(block_tail)
        return pl.BlockSpec((1,) + block_tail, index_map)
    return [spec(p) for p in range(pages)]


def _idx_sample_kernel(pt_ref, q_ref, w_ref, *refs):
    k_refs, o_ref = refs[:SMALL_PAGES], refs[SMALL_PAGES]
    q = q_ref[0].astype(BF16)
    w = w_ref[0] * (IDX_DIM ** -0.5)
    k_all = jnp.concatenate([k[0].astype(BF16) for k in k_refs], axis=1)
    r = jnp.dot(q, k_all, preferred_element_type=F32)
    score = jnp.sum(jnp.maximum(r, 0.0) * w, axis=0, keepdims=True)
    for p in range(SMALL_PAGES):
        o_ref[0, p] = score[:, p * LANES:(p + 1) * LANES]


def _idx_sample(page_table, qi3, wi3, cache_kidx):
    bd, n_pages = page_table.shape
    per_b = lambda shape: pl.BlockSpec((1,) + shape, lambda b, g, pt: (b,) + (0,) * len(shape))
    return pl.pallas_call(
        _idx_sample_kernel,
        grid_spec=pltpu.PrefetchScalarGridSpec(
            num_scalar_prefetch=1,
            grid=(bd, n_pages // SMALL_PAGES),
            in_specs=[per_b((N_IDX_HEADS, IDX_DIM)), per_b((N_IDX_HEADS, 1))]
            + _page_specs((IDX_DIM, Q_BLOCK), n_pages, False),
            out_specs=pl.BlockSpec((1, SMALL_PAGES, 1, LANES), lambda b, g, pt: (b, g, 0, 0)),
        ),
        out_shape=jax.ShapeDtypeStruct((bd, n_pages, 1, LANES), F32),
        compiler_params=_cparams(2),
        name="idx_sample",
    )(page_table, qi3, wi3, *([cache_kidx] * SMALL_PAGES))


def _select_sample_kernel(sc_ref, q_ref, w_ref, k_ref, tri_ref, low_ref, madd_o, maddn_o, *, n_top):
    n_seq = sc_ref.shape[0]
    seqs = range(n_seq)

    def total(x):
        return jnp.sum(jnp.sum(x, axis=1, keepdims=True), axis=0, keepdims=True)

    keys, keys_new = [], []
    for s in seqs:
        keys.append(_order_key(sc_ref[s]))
        r_new = jnp.sum(q_ref[s] * k_ref[s], axis=1, keepdims=True)
        s_new = jnp.sum(jnp.maximum(r_new * (IDX_DIM ** -0.5), 0.0) * w_ref[s], axis=0, keepdims=True)
        keys_new.append(_order_key(s_new))

    def count(s, pred):
        return total(jnp.where(pred(keys[s]), 1, 0)) + jnp.where(pred(keys_new[s]), 1, 0)

    thrs = tuple(jnp.where(count(s, lambda k: k >= 0) >= n_top, 0, INT_MIN).astype(I32) for s in seqs)

    def bit_body(t, thrs):
        bit = lax.shift_left(jnp.int32(1), jnp.int32(30) - t)
        cands = [thr | bit for thr in thrs]
        return tuple(jnp.where(count(s, lambda k: k >= cands[s]) >= n_top, cands[s], thrs[s]) for s in seqs)

    thrs = lax.fori_loop(0, 31, bit_body, thrs)
    for s in seqs:
        key, key_new, thr = keys[s], keys_new[s], thrs[s]
        need = (n_top - count(s, lambda k: k > thr)).astype(F32)
        eq = jnp.where(key == thr, 1.0, 0.0).astype(BF16)
        pre = jnp.dot(eq, tri_ref[...], preferred_element_type=F32)
        tot = jnp.broadcast_to(pre[:, LANES - 1:LANES], pre.shape).astype(BF16)
        before = jnp.dot(low_ref[...], tot, preferred_element_type=F32)
        take = (key > thr) | ((key == thr) & (before + pre <= need))
        madd_o[s] = jnp.where(take, 0.0, NEG)
        ties_past = total(jnp.where(key == thr, 1.0, 0.0))
        take_new = (key_new > thr) | ((key_new == thr) & (ties_past + 1.0 <= need))
        maddn_o[s] = jnp.where(take_new, 0.0, NEG)


def _select_sample(scores, qi3, wi3, ki3, n_top):
    bd, n_pages, _ = scores.shape
    n_seq = math.gcd(bd, SELECT_SEQS)
    tri = jnp.asarray((np.arange(LANES)[:, None] <= np.arange(LANES)[None, :]).astype(np.float32), BF16)
    low = jnp.asarray((np.arange(n_pages)[:, None] > np.arange(n_pages)[None, :]).astype(np.float32), BF16)
    per_b = lambda shape: pl.BlockSpec((n_seq,) + shape, lambda b: (b,) + (0,) * len(shape))
    return pl.pallas_call(
        functools.partial(_select_sample_kernel, n_top=n_top),
        grid=(bd // n_seq,),
        in_specs=[per_b((n_pages, LANES)), per_b((N_IDX_HEADS, IDX_DIM)), per_b((N_IDX_HEADS, 1)),
                  per_b((1, IDX_DIM)), _const_spec(tri.shape), _const_spec(low.shape)],
        out_specs=[per_b((n_pages, LANES)), per_b((1, 1))],
        out_shape=[jax.ShapeDtypeStruct((bd, n_pages, LANES), F32), jax.ShapeDtypeStruct((bd, 1, 1), F32)],
        compiler_params=_cparams(1),
        name="select_sample",
    )(scores, qi3, wi3, ki3, tri, low)


def _sample_bias(tabt, dist):
    bucket = _t5_bucket(dist)
    out = jnp.zeros((tabt.shape[0], dist.shape[1]), F32)
    for bk in range(N_BUCKETS):
        out = jnp.where(bucket == bk, tabt[:, bk:bk + 1], out)
    return out


def _dsa_logits_kernel(pt_ref, tabt_ref, qz_ref, madd_ref, *refs, past_len):
    k_refs, o_ref = refs[:SMALL_PAGES], refs[SMALL_PAGES]
    g = pl.program_id(1)
    qz16 = qz_ref[0].astype(BF16)
    k_all = jnp.concatenate([k[0].astype(BF16) for k in k_refs], axis=1)
    lg = jnp.dot(qz16, k_all, preferred_element_type=F32)
    key_pos = g * (SMALL_PAGES * Q_BLOCK) + lax.broadcasted_iota(I32, (1, SMALL_PAGES * Q_BLOCK), 1)
    lg = lg + _sample_bias(tabt_ref[...], past_len - key_pos)
    for p in range(SMALL_PAGES):
        o_ref[0, p] = lg[:, p * LANES:(p + 1) * LANES] + madd_ref[0, p]


def _dsa_logits(page_table, bias_table, qz, madd, cache_kt, past_len):
    bd, n_pages = page_table.shape
    per_b = lambda shape: pl.BlockSpec((1,) + shape, lambda b, g, pt: (b,) + (0,) * len(shape))
    return pl.pallas_call(
        functools.partial(_dsa_logits_kernel, past_len=past_len),
        grid_spec=pltpu.PrefetchScalarGridSpec(
            num_scalar_prefetch=1,
            grid=(bd, n_pages // SMALL_PAGES),
            in_specs=[pl.BlockSpec((N_HEADS_A, N_BUCKETS), lambda b, g, pt: (0, 0)),
                      per_b((N_HEADS_A, LANES)),
                      pl.BlockSpec((1, SMALL_PAGES, 1, LANES), lambda b, g, pt: (b, g, 0, 0))]
            + _page_specs((N_KV_A * HEAD_DIM, Q_BLOCK), n_pages, False),
            out_specs=pl.BlockSpec((1, SMALL_PAGES, N_HEADS_A, LANES), lambda b, g, pt: (b, g, 0, 0)),
        ),
        out_shape=jax.ShapeDtypeStruct((bd, n_pages, N_HEADS_A, LANES), F32),
        compiler_params=_cparams(2),
        name="dsa_logits",
    )(page_table, bias_table.T, qz, madd, *([cache_kt] * SMALL_PAGES))


def _dsa_pv_kernel(pt_ref, tabt_ref, lg_ref, qz_ref, maddn_ref, kn_ref, vnc_ref, *refs):
    v_refs = refs[:SMALL_PAGES]
    o_ref, m_scr, lsum_scr, acc_scr = refs[SMALL_PAGES:]
    g = pl.program_id(1)

    def new_logit():
        lg = jnp.sum(qz_ref[0] * kn_ref[0], axis=1, keepdims=True)
        return lg + _sample_bias(tabt_ref[...], jnp.zeros((1, 1), I32)) + maddn_ref[0]

    @pl.when(g == 0)
    def _():
        m_past = jnp.max(jnp.max(lg_ref[0], axis=0), axis=1, keepdims=True)
        m_scr[...] = jnp.maximum(m_past, new_logit())
        lsum_scr[...] = jnp.zeros(lsum_scr.shape, F32)
        acc_scr[...] = jnp.zeros(acc_scr.shape, F32)

    m = m_scr[...]
    prs = [jnp.exp(lg_ref[0, g * SMALL_PAGES + p] - m) for p in range(SMALL_PAGES)]
    lsum_scr[...] += functools.reduce(lambda a, b: a + b, prs)
    v_all = jnp.concatenate([v[0].astype(BF16) for v in v_refs], axis=1)
    acc_scr[...] += lax.dot_general(v_all, jnp.concatenate(prs, axis=1).astype(BF16), _NT,
                                    preferred_element_type=F32)

    @pl.when(g == pl.num_programs(1) - 1)
    def _():
        p_new = jnp.exp(new_logit() - m)
        denom = jnp.sum(lsum_scr[...], axis=1, keepdims=True) + p_new
        out = (acc_scr[...] + vnc_ref[0] * _col_to_row(p_new)) / _col_to_row(denom)
        r = lax.broadcasted_iota(I32, out.shape, 0)
        h = lax.broadcasted_iota(I32, out.shape, 1)
        out = jnp.where(_div_pow2(r, HEAD_DIM) == _div_pow2(h, GROUP_A), out, 0.0)
        o_ref[0] = out[:HEAD_DIM] + out[HEAD_DIM:]


def _dsa_pv(page_table, bias_table, logits, qz, maddn, k_new, v_new_col, cache_vt):
    bd, n_pages = page_table.shape
    per_b = lambda shape: pl.BlockSpec((1,) + shape, lambda b, g, pt: (b,) + (0,) * len(shape))
    kv = N_KV_A * HEAD_DIM
    return pl.pallas_call(
        _dsa_pv_kernel,
        grid_spec=pltpu.PrefetchScalarGridSpec(
            num_scalar_prefetch=1,
            grid=(bd, n_pages // SMALL_PAGES),
            in_specs=[pl.BlockSpec((N_HEADS_A, N_BUCKETS), lambda b, g, pt: (0, 0)),
                      per_b((n_pages, N_HEADS_A, LANES)), per_b((N_HEADS_A, kv)), per_b((1, 1)),
                      per_b((1, kv)), per_b((kv, 1))] + _page_specs((kv, Q_BLOCK), n_pages, False),
            out_specs=per_b((HEAD_DIM, N_HEADS_A)),
            scratch_shapes=[pltpu.VMEM((N_HEADS_A, 1), F32), pltpu.VMEM((N_HEADS_A, LANES), F32),
                            pltpu.VMEM((kv, N_HEADS_A), F32)],
        ),
        out_shape=jax.ShapeDtypeStruct((bd, HEAD_DIM, N_HEADS_A), F32),
        compiler_params=_cparams(2),
        name="dsa_pv",
    )(page_table, bias_table.T, logits, qz, maddn, k_new, v_new_col, *([cache_vt] * SMALL_PAGES))


def _sb_sample_kernel(pt_ref, qbd_ref, kn_ref, vnc_ref, w_ref, *refs, past_len):
    k_refs = refs[:SB_PAGES]
    v_refs = refs[SB_PAGES:2 * SB_PAGES]
    o_ref, run_scr, acc_scr = refs[2 * SB_PAGES:]
    g = pl.program_id(1)
    n_steps = pl.num_programs(1)
    n_pages = n_steps * SB_PAGES
    qbd = qbd_ref[0]
    qbd16 = qbd.astype(BF16)
    lane = lax.broadcasted_iota(I32, (1, LANES), 1)
    q_pos = past_len

    @pl.when(g == 0)
    def _():
        pos = past_len + lax.broadcasted_iota(I32, (N_HEADS_B, 1), 1)
        keep = pos < q_pos
        z = jnp.sum(qbd * kn_ref[0], axis=1, keepdims=True)
        lk = jnp.where(keep, _log_keep(z), 0.0)
        a = jnp.where(keep, jnp.exp(z + lk), 0.0)
        run_scr[...] = jnp.broadcast_to(lk, run_scr.shape)
        acc_scr[...] = vnc_ref[0] * _col_to_row(a)

    pages = range(SB_PAGES)
    nh = N_HEADS_B
    k_all = jnp.concatenate([k_refs[p][0].astype(BF16) for p in pages], axis=1)
    z_all = jnp.dot(qbd16, k_all, preferred_element_type=F32)
    z = jnp.concatenate([z_all[:, p * LANES:(p + 1) * LANES] for p in pages], axis=0)
    visit = g * SB_PAGES + _div_pow2(lax.broadcasted_iota(I32, z.shape, 0), nh)
    key_pos = (n_pages - 1 - visit) * Q_BLOCK + lax.broadcasted_iota(I32, z.shape, 1)
    keep = key_pos < q_pos
    lk = jnp.where(keep, _log_keep(z), 0.0)
    res = jnp.dot(_split_hi_lo(lk), w_ref[...], preferred_element_type=F32)
    run = run_scr[...]
    runs = []
    for p in pages:
        runs.append(run)
        run = run + res[p * nh:(p + 1) * nh, LANES:]
    run_scr[...] = run
    a = jnp.where(keep, jnp.exp(z + lk + res[:, :LANES] + jnp.concatenate(runs, axis=0)), 0.0)
    a_all = jnp.concatenate([a[p * nh:(p + 1) * nh] for p in pages], axis=1).astype(BF16)
    v_all = jnp.concatenate([v_refs[p][0].astype(BF16) for p in pages], axis=1)
    acc_scr[...] += lax.dot_general(v_all, a_all, _NT, preferred_element_type=F32)

    @pl.when(g == n_steps - 1)
    def _():
        r = lax.broadcasted_iota(I32, acc_scr.shape, 0)
        h = lax.broadcasted_iota(I32, acc_scr.shape, 1)
        o_ref[0] = jnp.sum(jnp.where(_div_pow2(r, HEAD_DIM) == h, acc_scr[...], 0.0), axis=1, keepdims=True)


def _sb_sample(page_table, qbd, k_new, v_new_col, cache_kt, cache_vt, past_len):
    bd, n_pages = page_table.shape
    width = cache_kt.shape[1]
    wmat = _suffix_matrix(1)
    per_b = lambda shape: pl.BlockSpec((1,) + shape, lambda b, g, pt: (b,) + (0,) * len(shape))
    kv_specs = _page_specs((width, Q_BLOCK), n_pages, True, SB_PAGES)
    return pl.pallas_call(
        functools.partial(_sb_sample_kernel, past_len=past_len),
        grid_spec=pltpu.PrefetchScalarGridSpec(
            num_scalar_prefetch=1,
            grid=(bd, n_pages // SB_PAGES),
            in_specs=[per_b((N_HEADS_B, width)), per_b((1, width)), per_b((width, 1)),
                      pl.BlockSpec(wmat.shape, lambda b, g, pt: (0, 0))] + kv_specs + kv_specs,
            out_specs=per_b((width, 1)),
            scratch_shapes=[pltpu.VMEM((N_HEADS_B, LANES), F32), pltpu.VMEM((width, N_HEADS_B), F32)],
        ),
        out_shape=jax.ShapeDtypeStruct((bd, width, 1), F32),
        compiler_params=_cparams(2),
        name="sb_sample",
    )(page_table, qbd, k_new, v_new_col, wmat, *([cache_kt] * SB_PAGES), *([cache_vt] * SB_PAGES))


def _route(logits, axis):
    slot = lax.broadcasted_iota(I32, logits.shape, axis)
    big = jnp.int32(ROUTER_LANES)

    def softmax_over(mask):
        m = jnp.max(jnp.where(mask, logits, NEG), axis=axis, keepdims=True)
        e = jnp.where(mask, jnp.exp(logits - m), 0.0)
        return e / jnp.sum(e, axis=axis, keepdims=True)

    def top1(prob, mask):
        best = jnp.max(jnp.where(mask, prob, -1.0), axis=axis, keepdims=True)
        idx = jnp.min(jnp.where(mask & (prob == best), slot, big), axis=axis, keepdims=True)
        return best, idx

    is_group = slot < N_GROUPS
    g_w, g_idx = top1(softmax_over(is_group), is_group)
    e_slot = slot - N_GROUPS
    in_group = (e_slot >= 0) & (e_slot < N_EXPERTS) & (_div_pow2(e_slot, EXPERTS_PER_GROUP) == g_idx)
    p_e = softmax_over(in_group)
    p1, i1 = top1(p_e, in_group)
    p2, i2 = top1(p_e, in_group & (slot != i1))
    denom = p1 + p2
    return slot, i1, p1 / denom * g_w, i2, p2 / denom * g_w


def _merge_kernel(x_ref, oa_ref, ob_ref, ga_ref, gb_ref, wpa, wpb, wout, ln2_ref, wr, br,
                  h_o, hn_o, route_o, *, grouped):
    mixed = (ga_ref[...] * jnp.dot(oa_ref[...], wpa[...], preferred_element_type=F32)
             + gb_ref[...] * jnp.dot(ob_ref[...], wpb[...], preferred_element_type=F32))
    h = x_ref[...] + jnp.dot(mixed.astype(BF16), wout[...], preferred_element_type=F32)
    h_o[...] = h
    ms = jnp.mean(h * h, axis=-1, keepdims=True)
    hn = (h * lax.rsqrt(ms + EPS) * ln2_ref[...]).astype(BF16)

    if grouped:
        half = hn.shape[1] // 2
        hi = lax.bitcast_convert_type(hn[:, :half].astype(F32), jnp.uint32)
        lo = lax.bitcast_convert_type(hn[:, half:].astype(F32), jnp.uint32)
        hn_o[...] = hi | (lo >> 16)
        logits_t = lax.dot_general(wr[...], hn, _NT, preferred_element_type=F32) + br[...]
        _, i1, w1, i2, w2 = _route(logits_t, 0)
        rows = [(i1 - N_GROUPS).astype(F32), (i2 - N_GROUPS).astype(F32), w1, w2]
        route_o[...] = jnp.concatenate(rows + [jnp.zeros_like(w1)] * (route_o.shape[0] - len(rows)), axis=0)
    else:
        hn_o[...] = hn
        logits = jnp.dot(hn, wr[...], preferred_element_type=F32) + br[...]
        slot, i1, w1, i2, w2 = _route(logits, 1)
        route_o[...] = jnp.where(slot == i1, w1, 0.0) + jnp.where(slot == i2, w2, 0.0)


def _merge(x2d, o_a, o_b, g_a, g_b, wpa, wpb, wout, ln2, wr, br, tm, grouped):
    n, d = x2d.shape
    row = lambda w: pl.BlockSpec((tm, w), lambda i: (i, 0))
    if grouped:
        consts = (wpa, wpb, wout, ln2, wr.T, br.T)
        hn_spec, hn_shape = row(d // 2), jax.ShapeDtypeStruct((n, d // 2), jnp.uint32)
        rt_spec = pl.BlockSpec((ROUTE_ROWS, tm), lambda i: (0, i))
        rt_shape = jax.ShapeDtypeStruct((ROUTE_ROWS, n), F32)
    else:
        consts = (wpa, wpb, wout, ln2, wr, br)
        hn_spec, hn_shape = row(d), jax.ShapeDtypeStruct((n, d), BF16)
        rt_spec, rt_shape = row(ROUTER_LANES), jax.ShapeDtypeStruct((n, ROUTER_LANES), F32)
    return pl.pallas_call(
        functools.partial(_merge_kernel, grouped=grouped),
        grid=(n // tm,),
        in_specs=[row(d), row(WIDTH_A), row(WIDTH_B), row(d), row(d)] + [_const_spec(c.shape) for c in consts],
        out_specs=[row(d), hn_spec, rt_spec],
        out_shape=[jax.ShapeDtypeStruct((n, d), F32), hn_shape, rt_shape],
        compiler_params=_cparams(1),
        name="merge",
    )(x2d, o_a, o_b, g_a, g_b, *consts)


def _moe_kernel(hn_ref, comb_ref, h_ref, wgu_ref, wd_ref, lnf_ref, y_ref, acc_scr, *, final_norm):
    e = pl.program_id(1)

    @pl.when(e == 0)
    def _():
        acc_scr[...] = jnp.zeros(acc_scr.shape, F32)

    d_expert = wd_ref.shape[1]
    gu = jnp.dot(hn_ref[...], wgu_ref[0], preferred_element_type=F32)
    gate, up = gu[:, :d_expert], gu[:, d_expert:]
    comb = comb_ref[...]
    lane = lax.broadcasted_iota(I32, comb.shape, 1)
    c_e = jnp.sum(jnp.where(lane == e + N_GROUPS, comb, 0.0), axis=1, keepdims=True)
    hid = gate * _sigmoid(gate) * up * c_e
    acc_scr[...] += jnp.dot(hid.astype(BF16), wd_ref[0], preferred_element_type=F32)

    @pl.when(e == pl.num_programs(1) - 1)
    def _():
        xo = h_ref[...] + acc_scr[...]
        if final_norm:
            ms = jnp.mean(xo * xo, axis=-1, keepdims=True)
            xo = xo * lax.rsqrt(ms + EPS) * lnf_ref[...]
        y_ref[...] = xo


def _moe(hn, comb, h, wgu, wd, lnf, tm, final_norm):
    n, d = h.shape
    n_exp, _, two_f = wgu.shape
    row = lambda w: pl.BlockSpec((tm, w), lambda i, e: (i, 0))
    return pl.pallas_call(
        functools.partial(_moe_kernel, final_norm=final_norm),
        grid=(n // tm, n_exp),
        in_specs=[row(d), row(ROUTER_LANES), row(d),
                  pl.BlockSpec((1, d, two_f), lambda i, e: (e, 0, 0)),
                  pl.BlockSpec((1, two_f // 2, d), lambda i, e: (e, 0, 0)),
                  pl.BlockSpec((1, d), lambda i, e: (0, 0))],
        out_specs=row(d),
        out_shape=jax.ShapeDtypeStruct((n, d), F32),
        scratch_shapes=[pltpu.VMEM((tm, d), F32)],
        compiler_params=_cparams(2),
        name="moe",
    )(hn, comb, h, wgu, wd, lnf)


def _moe_plan_kernel(rt_ref, before_ref, lower_ref, pos_o, tt_o):
    ch = rt_ref.shape[1]
    e1 = rt_ref[0:1, :].astype(I32)
    e2 = rt_ref[1:2, :].astype(I32)
    sub = lax.broadcasted_iota(I32, (ROUTER_LANES, ch), 0)
    a1, a2 = sub == e1, sub == e2
    chosen = (jnp.where(a1, 1.0, 0.0) + jnp.where(a2, 1.0, 0.0)).astype(BF16)
    before = jnp.dot(chosen, before_ref[...], preferred_element_type=F32)
    cnt = before[:, ch - 1:ch] + chosen[:, ch - 1:ch].astype(F32)
    ntile = jnp.floor((cnt + (MOE_ROWS - 1)) * (1.0 / MOE_ROWS))
    base_t = jnp.dot(lower_ref[...], jnp.broadcast_to(ntile, (ROUTER_LANES, LANES)).astype(BF16),
                     preferred_element_type=F32)[:, 0:1]
    slot = before + base_t * MOE_ROWS
    pos = [jnp.sum(jnp.where(a, slot, 0.0), axis=0, keepdims=True) for a in (a1, a2)]
    pos_o[0] = jnp.concatenate(pos + [jnp.zeros_like(pos[0])] * (pos_o.shape[1] - 2), axis=0).astype(I32)

    tile = lax.broadcasted_iota(I32, (ROUTER_LANES, LANES), 1).astype(F32)
    exp = lax.broadcasted_iota(I32, (ROUTER_LANES, LANES), 0)
    is_exp = exp < N_EXPERTS
    te = jnp.sum(jnp.where(is_exp & (base_t + ntile <= tile), 1, 0), axis=0, keepdims=True)
    total = jnp.sum(jnp.where(is_exp, ntile, 0.0), axis=0, keepdims=True).astype(I32) + jnp.zeros_like(te)
    rows = [jnp.minimum(te, N_EXPERTS - 1), total]
    tt_o[0] = jnp.concatenate(rows + [jnp.zeros_like(te)] * (tt_o.shape[1] - len(rows)), axis=0)


def _moe_routed_kernel(tt_ref, pos_ref, rt_ref, hnp_ref, h_ref, wgu_ref, wd_ref, lnf_ref, y_ref,
                       rows_scr, *, max_tiles, final_norm):
    c = pl.program_id(0)
    i = pl.program_id(1)
    ch, half = hnp_ref.shape
    n_tiles = tt_ref[(c * 2 + 1) * max_tiles]

    @pl.when(i == 0)
    def _():
        rows_scr[:, :half] = jnp.zeros((rows_scr.shape[0], half), F32)

        def place(t, carry):
            row = lax.bitcast_convert_type(hnp_ref[pl.ds(t, 1), :], F32)
            for k in range(TOP_E):
                rows_scr[pl.ds(pos_ref[0, k, t], 1), :half] = row
            return carry

        lax.fori_loop(0, ch, place, 0, unroll=MOE_UNROLL)

    @pl.when(i < n_tiles)
    def _():
        tile = pl.ds(pl.multiple_of(i * MOE_ROWS, MOE_ROWS), MOE_ROWS)
        packed = lax.bitcast_convert_type(rows_scr[tile, :half], jnp.uint32)
        first = lax.bitcast_convert_type(packed & jnp.uint32(0xFFFF0000), F32).astype(BF16)
        second = lax.bitcast_convert_type(packed << 16, F32).astype(BF16)
        x = jnp.concatenate([first, second], axis=1)
        d_expert = wd_ref.shape[1]
        gu = jnp.dot(x, wgu_ref[0], preferred_element_type=F32)
        gate, up = gu[:, :d_expert], gu[:, d_expert:]
        hid = gate * _sigmoid(gate) * up
        rows_scr[tile, :] = jnp.dot(hid.astype(BF16), wd_ref[0], preferred_element_type=F32)

    @pl.when(i == max_tiles - 1)
    def _():
        def combine(t, carry):
            acc = h_ref[pl.ds(t, 1), :]
            for k in range(TOP_E):
                acc = acc + rt_ref[TOP_E + k, t] * rows_scr[pl.ds(pos_ref[0, k, t], 1), :]
            y_ref[pl.ds(t, 1), :] = acc
            return carry

        lax.fori_loop(0, ch, combine, 0, unroll=MOE_UNROLL)
        if final_norm:
            xo = y_ref[...]
            ms = jnp.mean(xo * xo, axis=-1, keepdims=True)
            y_ref[...] = xo * lax.rsqrt(ms + EPS) * lnf_ref[...]


def _moe_routed(hnp, rt, h, wgu, wd, lnf, final_norm):
    n, d = h.shape
    ch = MOE_CHUNK
    assert n % ch == 0
    n_chunks = n // ch
    max_tiles = TOP_E * ch // MOE_ROWS + N_EXPERTS
    assert max_tiles <= LANES
    before = jnp.asarray((np.arange(ch)[:, None] < np.arange(ch)[None, :]).astype(np.float32), BF16)
    lower = jnp.asarray((np.arange(ROUTER_LANES)[:, None] > np.arange(ROUTER_LANES)[None, :]).astype(np.float32),
                        BF16)
    pos, tt = pl.pallas_call(
        _moe_plan_kernel,
        grid=(n_chunks,),
        in_specs=[pl.BlockSpec((ROUTE_ROWS, ch), lambda c: (0, c)), _const_spec(before.shape),
                  _const_spec(lower.shape)],
        out_specs=[pl.BlockSpec((1, ROUTE_ROWS, ch), lambda c: (c, 0, 0)),
                   pl.BlockSpec((1, ROUTE_ROWS, LANES), lambda c: (c, 0, 0))],
        out_shape=[jax.ShapeDtypeStruct((n_chunks, ROUTE_ROWS, ch), I32),
                   jax.ShapeDtypeStruct((n_chunks, ROUTE_ROWS, LANES), I32)],
        compiler_params=_cparams(1),
        name="moe_plan",
    )(rt, before, lower)
    tt_flat = tt[:, :2, :max_tiles].reshape(-1)
    n_exp, _, two_f = wgu.shape
    expert_of = lambda c, i, tt: (tt[c * 2 * max_tiles + i], 0, 0)
    return pl.pallas_call(
        functools.partial(_moe_routed_kernel, max_tiles=max_tiles, final_norm=final_norm),
        grid_spec=pltpu.PrefetchScalarGridSpec(
            num_scalar_prefetch=1,
            grid=(n_chunks, max_tiles),
            in_specs=[pl.BlockSpec((1, ROUTE_ROWS, ch), lambda c, i, tt: (c, 0, 0), memory_space=pltpu.SMEM),
                      pl.BlockSpec((ROUTE_ROWS, ch), lambda c, i, tt: (0, c), memory_space=pltpu.SMEM),
                      pl.BlockSpec((ch, d // 2), lambda c, i, tt: (c, 0)),
                      pl.BlockSpec((ch, d), lambda c, i, tt: (c, 0)),
                      pl.BlockSpec((1, d, two_f), expert_of),
                      pl.BlockSpec((1, two_f // 2, d), expert_of),
                      pl.BlockSpec((1, d), lambda c, i, tt: (0, 0))],
            out_specs=pl.BlockSpec((ch, d), lambda c, i, tt: (c, 0)),
            scratch_shapes=[pltpu.VMEM((max_tiles * MOE_ROWS, d), F32)],
        ),
        out_shape=jax.ShapeDtypeStruct((n, d), F32),
        compiler_params=_cparams(2),
        name="moe_routed",
    )(tt_flat, pos, rt, hnp, h, wgu, wd, lnf)


def _tile(n, pref):
    return pref if n % pref == 0 else n


def kernel(x_prompt, x_sample, cache_k_a, cache_v_a, cache_kidx_a, cache_k_b, cache_v_b, page_table,
           rel_bias_table, ln1_g, w_in, w_proj_a, w_proj_b, w_out, ln2_g, w_router_group, b_router_group,
           w_router_expert, b_router_expert, w_gate, w_up, w_down, ln_f_g):
    bsz, seq, d = x_prompt.shape
    bd, t_new, _ = x_sample.shape
    assert t_new == 1, "the sample group decodes one token per sequence"
    depth = w_in.shape[0]
    pool = cache_k_a.shape[1]
    n_pages = page_table.shape[1]
    past_len = n_pages * Q_BLOCK
    assert n_pages % SMALL_PAGES == 0 and n_pages % SB_PAGES == 0
    n_p, n_s = bsz * seq, bd * t_new
    d_expert = w_gate.shape[-1]

    xp, xs = x_prompt.reshape(n_p, d), x_sample.reshape(n_s, d)
    new_p, new_s = [], []
    for l in range(depth):
        w_pieces = _split_w_in(w_in[l])
        wpa, wpb, wout = (w.astype(BF16) for w in (w_proj_a[l], w_proj_b[l], w_out[l]))
        ln2 = ln2_g[l].reshape(1, d)
        wr = jnp.concatenate([w_router_group[l], jnp.moveaxis(w_router_expert[l], 0, 1).reshape(d, N_EXPERTS),
                              jnp.zeros((d, ROUTER_LANES - N_GROUPS - N_EXPERTS), F32)], axis=1).astype(BF16)
        br = jnp.concatenate([b_router_group[l], b_router_expert[l].reshape(N_EXPERTS),
                              jnp.zeros((ROUTER_LANES - N_GROUPS - N_EXPERTS,), F32)]).reshape(1, ROUTER_LANES)
        wgu = jnp.concatenate([w_gate[l], w_up[l]], axis=-1).reshape(N_EXPERTS, d, 2 * d_expert).astype(BF16)
        wd = w_down[l].reshape(N_EXPERTS, d_expert, d).astype(BF16)
        lnf = ln_f_g.reshape(1, d)

        def ffn(x2d, o_a, o_b, g_a, g_b, tm_merge, tm_moe):
            final_norm = l == depth - 1
            routed = x2d.shape[0] % MOE_CHUNK == 0
            h, hn, route = _merge(x2d, o_a, o_b, g_a, g_b, wpa, wpb, wout, ln2, wr, br, tm_merge, grouped=routed)
            if routed:
                return _moe_routed(hn, route, h, wgu, wd, lnf, final_norm)
            return _moe(hn, route, h, wgu, wd, lnf, tm_moe, final_norm)

        (qa, ka, va, kad, vad, qi, ki, kid, wi, qb, kb, vb, kb16, vb16, ga, gb) = _project(
            xp, ln1_g[l], w_pieces, _tile(seq, 512), seq_len=seq)
        r3 = lambda a: a.reshape(bsz, seq, a.shape[-1])
        n_top = min(TOPK_MAX, seq // 4)
        wit = jnp.swapaxes(r3(wi), 1, 2)
        vadt = jnp.swapaxes(vad.reshape(bsz, seq // Q_BLOCK, Q_BLOCK, 2 * LANES), 2, 3)
        o_at = _dsa_prompt(rel_bias_table, r3(qi), wit, r3(kid), r3(qa), r3(kad), vadt, n_top)
        o_a = jnp.swapaxes(o_at, 1, 2)
        o_b = _sb_prompt(r3(qb), r3(kb16), r3(vb16))
        xp = ffn(xp, o_a.reshape(n_p, WIDTH_A), o_b.reshape(n_p, WIDTH_B), ga, gb, _tile(n_p, 512), _tile(n_p, 1024))
        new_p.append((ka, va, ki, kb, vb))

        (qa, ka, va, _, _, qi, ki, _, wi, qb, kb, vb, _, _, ga, gb) = _project(xs, ln1_g[l], w_pieces, n_s)
        n_top = min(TOPK_MAX, (past_len + t_new) // 4)
        qi3 = qi.astype(F32).reshape(bd, N_IDX_HEADS, IDX_DIM)
        wi3 = wi.reshape(bd, N_IDX_HEADS, 1)
        paged_t = lambda c: jnp.moveaxis(c, 1, -1).reshape(pool, -1, Q_BLOCK)
        scores = _idx_sample(page_table, qi3, wi3, paged_t(cache_kidx_a[l]))
        madd, maddn = _select_sample(scores.reshape(bd, n_pages, LANES), qi3, wi3, ki.reshape(bd, 1, IDX_DIM), n_top)
        qa4 = qa.astype(F32).reshape(bd, N_HEADS_A, 1, HEAD_DIM) * (HEAD_DIM ** -0.5)
        kv_of_head = (np.arange(N_HEADS_A)[:, None] // GROUP_A) == np.arange(N_KV_A)[None, :]
        qz = jnp.where(kv_of_head[None, :, :, None], qa4, 0.0).reshape(bd, N_HEADS_A, N_KV_A * HEAD_DIM)
        logits = _dsa_logits(page_table, rel_bias_table, qz, madd.reshape(bd, n_pages, 1, LANES),
                             paged_t(cache_k_a[l]), past_len)
        o_a = _dsa_pv(page_table, rel_bias_table, logits, qz, maddn, ka.reshape(bd, 1, -1), va.reshape(bd, -1, 1),
                      paged_t(cache_v_a[l]))
        o_a = jnp.swapaxes(o_a, 1, 2)
        qb4 = qb.astype(F32).reshape(bd, N_HEADS_B, 1, HEAD_DIM) * (HEAD_DIM ** -0.5)
        own = np.eye(N_HEADS_B, dtype=bool)
        qbd = jnp.where(own[None, :, :, None], qb4, 0.0).reshape(bd, N_HEADS_B, WIDTH_B)
        o_b = _sb_sample(page_table, qbd, kb.reshape(bd, 1, -1), vb.reshape(bd, -1, 1),
                         paged_t(cache_k_b[l]), paged_t(cache_v_b[l]), past_len)
        xs = ffn(xs, o_a.reshape(n_s, WIDTH_A).astype(BF16), o_b.reshape(n_s, WIDTH_B).astype(BF16), ga, gb, n_s, n_s)
        new_s.append((ka, va, ki, kb, vb))

    outs = [xp.reshape(bsz, seq, d), xs.reshape(bd, t_new, d)]
    tails = ((N_KV_A, HEAD_DIM), (N_KV_A, HEAD_DIM), (IDX_DIM,), (N_HEADS_B, HEAD_DIM), (N_HEADS_B, HEAD_DIM))
    for idx, tail in enumerate(tails):
        outs.append(jnp.stack([jnp.moveaxis(r[idx].reshape((bsz,) + tail + (seq,)), -1, 1) for r in new_p]))
    for idx, tail in enumerate(tails):
        outs.append(jnp.stack([r[idx].reshape((bd, t_new) + tail) for r in new_s]))
    return tuple(outs)
```

```python
import functools
import math

import jax
import jax.numpy as jnp
import numpy as np
from jax import lax
from jax.experimental import pallas as pl
from jax.experimental.pallas import tpu as pltpu

F32 = jnp.float32
BF16 = jnp.bfloat16
I32 = jnp.int32

HEAD_DIM = 64
N_HEADS_A = 8
N_KV_A = 2
GROUP_A = N_HEADS_A // N_KV_A
N_IDX_HEADS = 8
IDX_DIM = 64
N_HEADS_B = 8
WIDTH_A = N_HEADS_A * HEAD_DIM
WIDTH_B = N_HEADS_B * HEAD_DIM
TOPK_MAX = 256
N_BUCKETS = 32
MAX_DISTANCE = 128
N_GROUPS = 4
EXPERTS_PER_GROUP = 8
N_EXPERTS = N_GROUPS * EXPERTS_PER_GROUP
TOP_E = 2
Q_BLOCK = 128
EPS = 1e-6

Q_TILE = 256

LANES = 128
VMEM_LIMIT_BYTES = 56 * 1024 * 1024

SMALL_PAGES = 32
SB_PAGES = 16
SELECT_SEQS = 4
MOE_CHUNK = 1024
MOE_ROWS = 128
MOE_TILES_PER_STEP = 2
MOE_UNROLL = 8
ROUTE_ROWS = 8
NEG = -1e30
INT_MIN = -(2 ** 31)
ROUTER_LANES = LANES

_NT = (((1,), (1,)), ((), ()))


def _cparams(n_axes):
    return pltpu.CompilerParams(dimension_semantics=("arbitrary",) * n_axes,
                                vmem_limit_bytes=VMEM_LIMIT_BYTES)


def _const_spec(shape):
    nd = len(shape)
    return pl.BlockSpec(shape, lambda *_: (0,) * nd)


def _div_pow2(x, n):
    assert n & (n - 1) == 0
    return x >> (n.bit_length() - 1)


def _sigmoid(x):
    return 1.0 / (1.0 + jnp.exp(-x))


def _half_masks(dtype):
    lane = lax.broadcasted_iota(I32, (LANES, LANES), 1)
    lo = jnp.where(lane < HEAD_DIM, 1.0, 0.0).astype(dtype)
    hi = jnp.where(lane < HEAD_DIM, 0.0, 1.0).astype(dtype)
    return lo, hi


def _pair_block_diag(blk, lo, hi):
    return jnp.concatenate([blk * lo, blk * hi], axis=0)


def _row_half_masks(dtype):
    r = lax.broadcasted_iota(I32, (LANES, LANES), 0)
    top = jnp.where(r < HEAD_DIM, 1.0, 0.0).astype(dtype)
    bottom = jnp.where(r < HEAD_DIM, 0.0, 1.0).astype(dtype)
    return top, bottom


def _pair_block_diag_t(blk_t, top, bottom):
    return jnp.concatenate([blk_t * top, blk_t * bottom], axis=1)


def _fori_groups(n_groups, group, body, init):
    def trip(o, carry):
        for u in range(group):
            carry = body(o * group + u, carry)
        return carry
    return lax.fori_loop(0, n_groups, trip, init)


def _col_to_row(x):
    n = x.shape[0]
    eye = lax.broadcasted_iota(I32, (n, n), 0) == lax.broadcasted_iota(I32, (n, n), 1)
    return jnp.sum(jnp.where(eye, x, 0.0), axis=0, keepdims=True)


def _order_key(score):
    score = jnp.where(score == 0.0, 0.0, score)
    b = lax.bitcast_convert_type(score, I32)
    return b ^ ((b >> 31) & jnp.int32(0x7FFFFFFF))


def _t5_bucket(dist):
    n = jnp.maximum(dist, 0)
    max_exact = N_BUCKETS // 2
    nf = jnp.maximum(n, max_exact).astype(F32)
    large = max_exact + (jnp.log(nf / max_exact) / math.log(MAX_DISTANCE / max_exact)
                         * (N_BUCKETS - max_exact)).astype(I32)
    return jnp.where(n < max_exact, n, jnp.minimum(large, N_BUCKETS - 1))


def _proj_kernel(x_ref, g_ref, wqa, wkv, wkvd, wqi, wkiw, wqb, wkb, wvb, wga, wgb,
                 qa_o, ka_o, va_o, kad_o, vad_o, qi_o, ki_o, kid_o, wi_o,
                 qb_o, kb_o, vb_o, kb16_o, vb16_o, ga_o, gb_o, *, feature_major):
    x = x_ref[...]
    ms = jnp.mean(x * x, axis=-1, keepdims=True)
    xn = (x * lax.rsqrt(ms + EPS) * g_ref[...]).astype(BF16)

    def mm(w):
        return jnp.dot(xn, w[...], preferred_element_type=F32)

    def put_new_rows(o_ref, val):
        if feature_major:
            o_ref[0] = val.T
        else:
            o_ref[...] = val

    qa_o[...] = mm(wqa).astype(BF16)
    kv = mm(wkv)
    put_new_rows(ka_o, kv[:, :N_KV_A * HEAD_DIM])
    put_new_rows(va_o, kv[:, N_KV_A * HEAD_DIM:])
    kvd = mm(wkvd).astype(BF16)
    kad_o[...] = kvd[:, :2 * LANES]
    vad_o[...] = kvd[:, 2 * LANES:]
    qi_o[...] = mm(wqi).astype(BF16)
    kiw = mm(wkiw)
    put_new_rows(ki_o, kiw[:, :IDX_DIM])
    kid_o[...] = kiw[:, :LANES].astype(BF16)
    wi_o[...] = kiw[:, LANES:LANES + N_IDX_HEADS] * (N_IDX_HEADS ** -0.5)
    qb_o[...] = mm(wqb).astype(BF16)
    kb = mm(wkb)
    put_new_rows(kb_o, kb)
    kb16_o[...] = kb.astype(BF16)
    vb = mm(wvb)
    put_new_rows(vb_o, vb)
    vb16_o[...] = vb.astype(BF16)
    ga_o[...] = _sigmoid(mm(wga))
    gb_o[...] = _sigmoid(mm(wgb))


def _split_w_in(w_in):
    d_model = w_in.shape[0]
    widths = (WIDTH_A, N_KV_A * HEAD_DIM, N_KV_A * HEAD_DIM, N_IDX_HEADS * IDX_DIM, N_IDX_HEADS, IDX_DIM,
              WIDTH_B, WIDTH_B, WIDTH_B, d_model, d_model)
    points = [int(p) for p in np.cumsum(widths)[:-1]]
    qa, ka, va, qi, wi, ki, qb, kb, vb, ga, gb = jnp.split(w_in, points, axis=1)
    dup = lambda a: jnp.concatenate([a[:, :HEAD_DIM], a[:, :HEAD_DIM], a[:, HEAD_DIM:], a[:, HEAD_DIM:]], axis=1)
    wkv = jnp.concatenate([ka, va], axis=1)
    wkvd = jnp.concatenate([dup(ka), dup(va)], axis=1)
    wkiw = jnp.concatenate([ki, ki, wi, jnp.zeros((d_model, LANES - N_IDX_HEADS), w_in.dtype)], axis=1)
    return tuple(a.astype(BF16) for a in (qa, wkv, wkvd, qi, wkiw, qb, kb, vb, ga, gb))


_NEW_ROW_OUTPUTS = (1, 2, 6, 10, 11)


def _project(x2d, ln_g, w_pieces, tm, seq_len=None):
    n, d = x2d.shape
    assert n % tm == 0
    out_defs = [
        (WIDTH_A, BF16), (N_KV_A * HEAD_DIM, F32), (N_KV_A * HEAD_DIM, F32), (2 * LANES, BF16), (2 * LANES, BF16),
        (N_IDX_HEADS * IDX_DIM, BF16), (IDX_DIM, F32), (LANES, BF16), (N_IDX_HEADS, F32),
        (WIDTH_B, BF16), (WIDTH_B, F32), (WIDTH_B, F32), (WIDTH_B, BF16), (WIDTH_B, BF16), (d, F32), (d, F32)]
    row = lambda i: (i, 0)
    out_specs = [pl.BlockSpec((tm, w), row) for w, _ in out_defs]
    out_shape = [jax.ShapeDtypeStruct((n, w), dt) for w, dt in out_defs]
    if seq_len is not None:
        assert seq_len % tm == 0
        tiles = seq_len // tm
        for k in _NEW_ROW_OUTPUTS:
            w, dt = out_defs[k]
            out_specs[k] = pl.BlockSpec((1, w, tm), lambda i: (i // tiles, 0, i % tiles))
            out_shape[k] = jax.ShapeDtypeStruct((n // seq_len, w, seq_len), dt)
    return pl.pallas_call(
        functools.partial(_proj_kernel, feature_major=seq_len is not None),
        grid=(n // tm,),
        in_specs=[pl.BlockSpec((tm, d), row), _const_spec((1, d))] + [_const_spec(w.shape) for w in w_pieces],
        out_specs=out_specs,
        out_shape=out_shape,
        compiler_params=_cparams(1),
        name="proj",
    )(x2d, ln_g.reshape(1, d), *w_pieces)


def _dsa_prompt_kernel(tab_ref, qi_ref, wit_ref, kid_ref, qa_ref, kad_ref, vadt_ref, tril_ref, o_ref,
                       key_scr, lg_scr, bias_scr, acc_scr, *, n_top):
    b = pl.program_id(0)
    i = pl.program_id(1)
    kb, qt = Q_BLOCK, Q_TILE
    bpt = qt // kb
    row = lax.broadcasted_iota(I32, (kb, qt), 0)
    col = lax.broadcasted_iota(I32, (kb, qt), 1)
    lo, hi = _half_masks(BF16)
    top, bottom = _row_half_masks(BF16)
    d_min = 1 - bpt

    def block_dist(j):
        return i * bpt - j

    @pl.when((b == 0) & (i == 0))
    def _():
        for d in range(d_min, 3):
            bucket = _t5_bucket(d * kb + col - row)
            for h in range(N_HEADS_A):
                def sel(bk, acc):
                    return jnp.where(bucket == bk, tab_ref[bk, h], acc)
                tile = lax.fori_loop(0, N_BUCKETS, sel, jnp.zeros((kb, qt), F32))
                bias_scr[d - d_min, h // 2, (h % 2) * kb:(h % 2 + 1) * kb, :] = tile

    def causal(j):
        return block_dist(j) * kb + col - row >= 0

    w = wit_ref[0] * (IDX_DIM ** -0.5)
    wrows = [w[h:h + 1, :] for h in range(N_IDX_HEADS)]

    def score_body(j, carry):
        kblk = kid_ref[0, pl.ds(pl.multiple_of(j * kb, kb), kb), :]
        kbd = _pair_block_diag(kblk, lo, hi)
        acc = jnp.zeros((kb, qt), F32)
        for p in range(N_IDX_HEADS // 2):
            r = lax.dot_general(kbd, qi_ref[0, :, p * LANES:(p + 1) * LANES], _NT,
                                preferred_element_type=F32)
            acc = acc + jnp.maximum(r[:kb], 0.0) * wrows[2 * p] + jnp.maximum(r[kb:], 0.0) * wrows[2 * p + 1]
        key_scr[j] = _order_key(jnp.where(causal(j), acc, -jnp.inf))
        return carry

    _fori_groups(i + 1, bpt, score_body, 0)

    def count(pred):
        def body(j, acc):
            return acc + jnp.where(pred(key_scr[j]), 1, 0)
        acc = _fori_groups(i + 1, bpt, body, jnp.zeros((kb, qt), I32))
        return jnp.sum(acc, axis=0, keepdims=True)

    thr = jnp.where(count(lambda k: k >= 0) >= n_top, 0, INT_MIN).astype(I32)

    def bit_body(t, thr):
        cand = thr | lax.shift_left(jnp.int32(1), jnp.int32(30) - t)
        return jnp.where(count(lambda k: k >= cand) >= n_top, cand, thr)

    thr = lax.fori_loop(0, 31, bit_body, thr)
    need = (n_top - count(lambda k: k > thr)).astype(F32)

    tril = tril_ref[...]

    n_pairs = N_HEADS_A // 2
    q_ps = [(qa_ref[0, :, p * LANES:(p + 1) * LANES].astype(F32) * (HEAD_DIM ** -0.5)).astype(BF16)
            for p in range(n_pairs)]

    def logits_body(j, carry):
        ties_before, maxes = carry[0], carry[1:]
        key = key_scr[j]
        eq = key == thr
        pre = jnp.dot(tril, jnp.where(eq, 1.0, 0.0).astype(BF16), preferred_element_type=F32)
        take = (key > thr) | (eq & (ties_before + pre <= need))
        madd = jnp.where(take & causal(j), 0.0, NEG)
        rows = pl.ds(pl.multiple_of(j * kb, kb), kb)
        dsel = jnp.minimum(block_dist(j), 2) - d_min
        out = [ties_before + pre[kb - 1:kb, :]]
        for p in range(n_pairs):
            c = (2 * p) // GROUP_A
            kbd = _pair_block_diag(kad_ref[0, rows, c * LANES:(c + 1) * LANES], lo, hi)
            lg = lax.dot_general(kbd, q_ps[p], _NT, preferred_element_type=F32) + bias_scr[dsel, p]
            for half in range(2):
                lgh = lg[half * kb:(half + 1) * kb] + madd
                lg_scr[j, p, half * kb:(half + 1) * kb, :] = lgh
                out.append(jnp.maximum(maxes[2 * p + half], jnp.max(lgh, axis=0, keepdims=True)))
        return tuple(out)

    neg = jnp.full((1, qt), NEG, F32)
    zero = jnp.zeros((1, qt), F32)
    maxes = _fori_groups(i + 1, bpt, logits_body, (zero,) + (neg,) * N_HEADS_A)[1:]
    acc_scr[...] = jnp.zeros(acc_scr.shape, F32)

    def pv_body(j, sums):
        out = []
        for p in range(n_pairs):
            c = (2 * p) // GROUP_A
            vbd = _pair_block_diag_t(vadt_ref[0, j, c * LANES:(c + 1) * LANES, :], top, bottom)
            p0 = jnp.exp(lg_scr[j, p, :kb, :] - maxes[2 * p])
            p1 = jnp.exp(lg_scr[j, p, kb:, :] - maxes[2 * p + 1])
            out += [sums[2 * p] + jnp.sum(p0, axis=0, keepdims=True),
                    sums[2 * p + 1] + jnp.sum(p1, axis=0, keepdims=True)]
            acc_scr[p] += jnp.dot(vbd, jnp.concatenate([p0, p1], axis=0).astype(BF16),
                                  preferred_element_type=F32)
        return tuple(out)

    sums = _fori_groups(i + 1, bpt, pv_body, (zero,) * N_HEADS_A)
    for p in range(n_pairs):
        denom = jnp.where(row < HEAD_DIM, sums[2 * p], sums[2 * p + 1])
        o_ref[0, p * LANES:(p + 1) * LANES, :] = (acc_scr[p] / denom).astype(o_ref.dtype)


def _dsa_prompt(bias_table, qi, wit, kid, qa, kad, vadt, n_top):
    bsz, seq, _ = qi.shape
    assert seq % Q_TILE == 0 and Q_TILE % Q_BLOCK == 0
    nkb = seq // Q_BLOCK
    n_bias = Q_TILE // Q_BLOCK + 2
    tril = (np.arange(Q_BLOCK)[:, None] >= np.arange(Q_BLOCK)[None, :]).astype(np.float32)
    qblk = lambda w: pl.BlockSpec((1, Q_TILE, w), lambda b, i: (b, i, 0))
    full = lambda w: pl.BlockSpec((1, seq, w), lambda b, i: (b, 0, 0))
    return pl.pallas_call(
        functools.partial(_dsa_prompt_kernel, n_top=n_top),
        grid=(bsz, seq // Q_TILE),
        in_specs=[pl.BlockSpec(memory_space=pltpu.SMEM), qblk(N_IDX_HEADS * IDX_DIM),
                  pl.BlockSpec((1, N_IDX_HEADS, Q_TILE), lambda b, i: (b, 0, i)), full(LANES),
                  qblk(WIDTH_A), full(2 * LANES),
                  pl.BlockSpec((1, nkb, 2 * LANES, Q_BLOCK), lambda b, i: (b, 0, 0, 0)),
                  _const_spec((Q_BLOCK, Q_BLOCK))],
        out_specs=pl.BlockSpec((1, WIDTH_A, Q_TILE), lambda b, i: (b, 0, i)),
        out_shape=jax.ShapeDtypeStruct((bsz, WIDTH_A, seq), BF16),
        scratch_shapes=[pltpu.VMEM((nkb, Q_BLOCK, Q_TILE), I32),
                        pltpu.VMEM((nkb, N_HEADS_A // 2, 2 * Q_BLOCK, Q_TILE), F32),
                        pltpu.VMEM((n_bias, N_HEADS_A // 2, 2 * Q_BLOCK, Q_TILE), F32),
                        pltpu.VMEM((N_HEADS_A // 2, LANES, Q_TILE), F32)],
        compiler_params=_cparams(2),
        name="dsa_prompt",
    )(bias_table, qi, wit, kid, qa, kad, vadt, jnp.asarray(tril, BF16))


def _suffix_matrix(n_heads):
    n = n_heads * Q_BLOCK
    r = np.arange(n)
    same = (r[:, None] // Q_BLOCK) == (r[None, :] // Q_BLOCK)
    u = same & (r[:, None] > r[None, :])
    w = np.concatenate([u, same], axis=1).astype(np.float32)
    return jnp.asarray(np.concatenate([w, w], axis=0), BF16)


def _log_keep(z):
    return -(jnp.maximum(z, 0.0) + jnp.log(1.0 + jnp.exp(-jnp.abs(z))))


def _split_hi_lo(x):
    hi = x.astype(BF16)
    lo = (x - hi.astype(F32)).astype(BF16)
    return jnp.concatenate([hi, lo], axis=1)


def _sb_prompt_kernel(qb_ref, kb_ref, vb_ref, w_ref, o_ref, run_scr, acc_scr):
    i = pl.program_id(1)
    kb, qt = Q_BLOCK, Q_TILE
    bpt = qt // kb
    n_pairs = N_HEADS_B // 2
    row = lax.broadcasted_iota(I32, (qt, 2 * kb), 0)
    col = lax.broadcasted_iota(I32, (qt, 2 * kb), 1) & (kb - 1)
    lo, hi = _half_masks(BF16)
    q_ps = [(qb_ref[0, :, p * LANES:(p + 1) * LANES].astype(F32) * (HEAD_DIM ** -0.5)).astype(BF16)
            for p in range(n_pairs)]

    def block(j, edge, first):
        rows = pl.ds(pl.multiple_of(j * kb, kb), kb)
        if edge:
            strict = j * kb + col < i * qt + row
        for p in range(n_pairs):
            kbd = _pair_block_diag(kb_ref[0, rows, p * LANES:(p + 1) * LANES], lo, hi)
            vbd = _pair_block_diag(vb_ref[0, rows, p * LANES:(p + 1) * LANES], lo, hi)
            z = lax.dot_general(q_ps[p], kbd, _NT, preferred_element_type=F32)
            lk = _log_keep(z)
            if edge:
                lk = jnp.where(strict, lk, 0.0)
            res = jnp.dot(_split_hi_lo(lk), w_ref[...], preferred_element_type=F32)
            x = z + lk + res[:, :2 * kb]
            if not first:
                x = x + run_scr[p]
            a = jnp.exp(x)
            if edge:
                a = jnp.where(strict, a, 0.0)
            pv = jnp.dot(a.astype(BF16), vbd, preferred_element_type=F32)
            if first:
                run_scr[p] = res[:, 2 * kb:]
                acc_scr[p] = pv
            else:
                run_scr[p] += res[:, 2 * kb:]
                acc_scr[p] += pv

    for e in range(bpt):
        block((i + 1) * bpt - 1 - e, True, e == 0)

    def body(t, carry):
        block(i * bpt - 1 - t, False, False)
        return carry

    _fori_groups(i, bpt, body, 0)
    for p in range(n_pairs):
        o_ref[0, :, p * LANES:(p + 1) * LANES] = acc_scr[p].astype(o_ref.dtype)


def _sb_prompt(qb, kb16, vb16):
    bsz, seq, width = qb.shape
    assert seq % Q_TILE == 0 and Q_TILE % Q_BLOCK == 0
    wmat = _suffix_matrix(2)
    qblk = pl.BlockSpec((1, Q_TILE, width), lambda b, i: (b, i, 0))
    full = pl.BlockSpec((1, seq, width), lambda b, i: (b, 0, 0))
    return pl.pallas_call(
        _sb_prompt_kernel,
        grid=(bsz, seq // Q_TILE),
        in_specs=[qblk, full, full, _const_spec(wmat.shape)],
        out_specs=qblk,
        out_shape=jax.ShapeDtypeStruct((bsz, seq, width), BF16),
        scratch_shapes=[pltpu.VMEM((N_HEADS_B // 2, Q_TILE, 2 * Q_BLOCK), F32),
                        pltpu.VMEM((N_HEADS_B // 2, Q_TILE, LANES), F32)],
        compiler_params=_cparams(2),
        name="sb_prompt",
    )(qb, kb16, vb16, wmat)


def _page_specs(block_tail, n_pages, reverse, pages=None):
    pages = pages or SMALL_PAGES

    def spec(p):
        def index_map(b, g, pt):
            page = g * pages + p
            if reverse:
                page = n_pages - 1 - page
            return (pt[b, page],) + (0,) * len(block_tail)
        return pl.BlockSpec((1,) + block_tail, index_map)
    return [spec(p) for p in range(pages)]


def _idx_sample_kernel(pt_ref, q_ref, w_ref, *refs):
    k_refs, o_ref = refs[:SMALL_PAGES], refs[SMALL_PAGES]
    q = q_ref[0].astype(BF16)
    w = w_ref[0] * (IDX_DIM ** -0.5)
    k_all = jnp.concatenate([k[0].astype(BF16) for k in k_refs], axis=1)
    r = jnp.dot(q, k_all, preferred_element_type=F32)
    score = jnp.sum(jnp.maximum(r, 0.0) * w, axis=0, keepdims=True)
    for p in range(SMALL_PAGES):
        o_ref[0, p] = score[:, p * LANES:(p + 1) * LANES]


def _idx_sample(page_table, qi3, wi3, cache_kidx):
    bd, n_pages = page_table.shape
    per_b = lambda shape: pl.BlockSpec((1,) + shape, lambda b, g, pt: (b,) + (0,) * len(shape))
    return pl.pallas_call(
        _idx_sample_kernel,
        grid_spec=pltpu.PrefetchScalarGridSpec(
            num_scalar_prefetch=1,
            grid=(bd, n_pages // SMALL_PAGES),
            in_specs=[per_b((N_IDX_HEADS, IDX_DIM)), per_b((N_IDX_HEADS, 1))]
            + _page_specs((IDX_DIM, Q_BLOCK), n_pages, False),
            out_specs=pl.BlockSpec((1, SMALL_PAGES, 1, LANES), lambda b, g, pt: (b, g, 0, 0)),
        ),
        out_shape=jax.ShapeDtypeStruct((bd, n_pages, 1, LANES), F32),
        compiler_params=_cparams(2),
        name="idx_sample",
    )(page_table, qi3, wi3, *([cache_kidx] * SMALL_PAGES))


def _select_sample_kernel(sc_ref, q_ref, w_ref, k_ref, tri_ref, low_ref, madd_o, maddn_o, *, n_top):
    n_seq = sc_ref.shape[0]
    seqs = range(n_seq)

    def total(x):
        return jnp.sum(jnp.sum(x, axis=1, keepdims=True), axis=0, keepdims=True)

    keys, keys_new = [], []
    for s in seqs:
        keys.append(_order_key(sc_ref[s]))
        r_new = jnp.sum(q_ref[s] * k_ref[s], axis=1, keepdims=True)
        s_new = jnp.sum(jnp.maximum(r_new * (IDX_DIM ** -0.5), 0.0) * w_ref[s], axis=0, keepdims=True)
        keys_new.append(_order_key(s_new))

    def count(s, pred):
        return total(jnp.where(pred(keys[s]), 1, 0)) + jnp.where(pred(keys_new[s]), 1, 0)

    thrs = tuple(jnp.where(count(s, lambda k: k >= 0) >= n_top, 0, INT_MIN).astype(I32) for s in seqs)

    def bit_body(t, thrs):
        bit = lax.shift_left(jnp.int32(1), jnp.int32(30) - t)
        cands = [thr | bit for thr in thrs]
        return tuple(jnp.where(count(s, lambda k: k >= cands[s]) >= n_top, cands[s], thrs[s]) for s in seqs)

    thrs = lax.fori_loop(0, 31, bit_body, thrs)
    for s in seqs:
        key, key_new, thr = keys[s], keys_new[s], thrs[s]
        need = (n_top - count(s, lambda k: k > thr)).astype(F32)
        eq = jnp.where(key == thr, 1.0, 0.0).astype(BF16)
        pre = jnp.dot(eq, tri_ref[...], preferred_element_type=F32)
        tot = jnp.broadcast_to(pre[:, LANES - 1:LANES], pre.shape).astype(BF16)
        before = jnp.dot(low_ref[...], tot, preferred_element_type=F32)
        take = (key > thr) | ((key == thr) & (before + pre <= need))
        madd_o[s] = jnp.where(take, 0.0, NEG)
        ties_past = total(jnp.where(key == thr, 1.0, 0.0))
        take_new = (key_new > thr) | ((key_new == thr) & (ties_past + 1.0 <= need))
        maddn_o[s] = jnp.where(take_new, 0.0, NEG)


def _select_sample(scores, qi3, wi3, ki3, n_top):
    bd, n_pages, _ = scores.shape
    n_seq = math.gcd(bd, SELECT_SEQS)
    tri = jnp.asarray((np.arange(LANES)[:, None] <= np.arange(LANES)[None, :]).astype(np.float32), BF16)
    low = jnp.asarray((np.arange(n_pages)[:, None] > np.arange(n_pages)[None, :]).astype(np.float32), BF16)
    per_b = lambda shape: pl.BlockSpec((n_seq,) + shape, lambda b: (b,) + (0,) * len(shape))
    return pl.pallas_call(
        functools.partial(_select_sample_kernel, n_top=n_top),
        grid=(bd // n_seq,),
        in_specs=[per_b((n_pages, LANES)), per_b((N_IDX_HEADS, IDX_DIM)), per_b((N_IDX_HEADS, 1)),
                  per_b((1, IDX_DIM)), _const_spec(tri.shape), _const_spec(low.shape)],
        out_specs=[per_b((n_pages, LANES)), per_b((1, 1))],
        out_shape=[jax.ShapeDtypeStruct((bd, n_pages, LANES), F32), jax.ShapeDtypeStruct((bd, 1, 1), F32)],
        compiler_params=_cparams(1),
        name="select_sample",
    )(scores, qi3, wi3, ki3, tri, low)


def _sample_bias(tabt, dist):
    bucket = _t5_bucket(dist)
    out = jnp.zeros((tabt.shape[0], dist.shape[1]), F32)
    for bk in range(N_BUCKETS):
        out = jnp.where(bucket == bk, tabt[:, bk:bk + 1], out)
    return out


def _dsa_logits_kernel(pt_ref, tabt_ref, qz_ref, madd_ref, *refs, past_len):
    k_refs, o_ref = refs[:SMALL_PAGES], refs[SMALL_PAGES]
    g = pl.program_id(1)
    qz16 = qz_ref[0].astype(BF16)
    k_all = jnp.concatenate([k[0].astype(BF16) for k in k_refs], axis=1)
    lg = jnp.dot(qz16, k_all, preferred_element_type=F32)
    key_pos = g * (SMALL_PAGES * Q_BLOCK) + lax.broadcasted_iota(I32, (1, SMALL_PAGES * Q_BLOCK), 1)
    lg = lg + _sample_bias(tabt_ref[...], past_len - key_pos)
    for p in range(SMALL_PAGES):
        o_ref[0, p] = lg[:, p * LANES:(p + 1) * LANES] + madd_ref[0, p]


def _dsa_logits(page_table, bias_table, qz, madd, cache_kt, past_len):
    bd, n_pages = page_table.shape
    per_b = lambda shape: pl.BlockSpec((1,) + shape, lambda b, g, pt: (b,) + (0,) * len(shape))
    return pl.pallas_call(
        functools.partial(_dsa_logits_kernel, past_len=past_len),
        grid_spec=pltpu.PrefetchScalarGridSpec(
            num_scalar_prefetch=1,
            grid=(bd, n_pages // SMALL_PAGES),
            in_specs=[pl.BlockSpec((N_HEADS_A, N_BUCKETS), lambda b, g, pt: (0, 0)),
                      per_b((N_HEADS_A, LANES)),
                      pl.BlockSpec((1, SMALL_PAGES, 1, LANES), lambda b, g, pt: (b, g, 0, 0))]
            + _page_specs((N_KV_A * HEAD_DIM, Q_BLOCK), n_pages, False),
            out_specs=pl.BlockSpec((1, SMALL_PAGES, N_HEADS_A, LANES), lambda b, g, pt: (b, g, 0, 0)),
        ),
        out_shape=jax.ShapeDtypeStruct((bd, n_pages, N_HEADS_A, LANES), F32),
        compiler_params=_cparams(2),
        name="dsa_logits",
    )(page_table, bias_table.T, qz, madd, *([cache_kt] * SMALL_PAGES))


def _dsa_pv_kernel(pt_ref, tabt_ref, lg_ref, qz_ref, maddn_ref, kn_ref, vnc_ref, *refs):
    v_refs = refs[:SMALL_PAGES]
    o_ref, m_scr, lsum_scr, acc_scr = refs[SMALL_PAGES:]
    g = pl.program_id(1)

    def new_logit():
        lg = jnp.sum(qz_ref[0] * kn_ref[0], axis=1, keepdims=True)
        return lg + _sample_bias(tabt_ref[...], jnp.zeros((1, 1), I32)) + maddn_ref[0]

    @pl.when(g == 0)
    def _():
        m_past = jnp.max(jnp.max(lg_ref[0], axis=0), axis=1, keepdims=True)
        m_scr[...] = jnp.maximum(m_past, new_logit())
        lsum_scr[...] = jnp.zeros(lsum_scr.shape, F32)
        acc_scr[...] = jnp.zeros(acc_scr.shape, F32)

    m = m_scr[...]
    prs = [jnp.exp(lg_ref[0, g * SMALL_PAGES + p] - m) for p in range(SMALL_PAGES)]
    lsum_scr[...] += functools.reduce(lambda a, b: a + b, prs)
    v_all = jnp.concatenate([v[0].astype(BF16) for v in v_refs], axis=1)
    acc_scr[...] += lax.dot_general(v_all, jnp.concatenate(prs, axis=1).astype(BF16), _NT,
                                    preferred_element_type=F32)

    @pl.when(g == pl.num_programs(1) - 1)
    def _():
        p_new = jnp.exp(new_logit() - m)
        denom = jnp.sum(lsum_scr[...], axis=1, keepdims=True) + p_new
        out = (acc_scr[...] + vnc_ref[0] * _col_to_row(p_new)) / _col_to_row(denom)
        r = lax.broadcasted_iota(I32, out.shape, 0)
        h = lax.broadcasted_iota(I32, out.shape, 1)
        out = jnp.where(_div_pow2(r, HEAD_DIM) == _div_pow2(h, GROUP_A), out, 0.0)
        o_ref[0] = out[:HEAD_DIM] + out[HEAD_DIM:]


def _dsa_pv(page_table, bias_table, logits, qz, maddn, k_new, v_new_col, cache_vt):
    bd, n_pages = page_table.shape
    per_b = lambda shape: pl.BlockSpec((1,) + shape, lambda b, g, pt: (b,) + (0,) * len(shape))
    kv = N_KV_A * HEAD_DIM
    return pl.pallas_call(
        _dsa_pv_kernel,
        grid_spec=pltpu.PrefetchScalarGridSpec(
            num_scalar_prefetch=1,
            grid=(bd, n_pages // SMALL_PAGES),
            in_specs=[pl.BlockSpec((N_HEADS_A, N_BUCKETS), lambda b, g, pt: (0, 0)),
                      per_b((n_pages, N_HEADS_A, LANES)), per_b((N_HEADS_A, kv)), per_b((1, 1)),
                      per_b((1, kv)), per_b((kv, 1))] + _page_specs((kv, Q_BLOCK), n_pages, False),
            out_specs=per_b((HEAD_DIM, N_HEADS_A)),
            scratch_shapes=[pltpu.VMEM((N_HEADS_A, 1), F32), pltpu.VMEM((N_HEADS_A, LANES), F32),
                            pltpu.VMEM((kv, N_HEADS_A), F32)],
        ),
        out_shape=jax.ShapeDtypeStruct((bd, HEAD_DIM, N_HEADS_A), F32),
        compiler_params=_cparams(2),
        name="dsa_pv",
    )(page_table, bias_table.T, logits, qz, maddn, k_new, v_new_col, *([cache_vt] * SMALL_PAGES))


def _sb_sample_kernel(pt_ref, qbd_ref, kn_ref, vnc_ref, w_ref, *refs, past_len):
    k_refs = refs[:SB_PAGES]
    v_refs = refs[SB_PAGES:2 * SB_PAGES]
    o_ref, run_scr, acc_scr = refs[2 * SB_PAGES:]
    g = pl.program_id(1)
    n_steps = pl.num_programs(1)
    n_pages = n_steps * SB_PAGES
    qbd = qbd_ref[0]
    qbd16 = qbd.astype(BF16)
    lane = lax.broadcasted_iota(I32, (1, LANES), 1)
    q_pos = past_len

    @pl.when(g == 0)
    def _():
        pos = past_len + lax.broadcasted_iota(I32, (N_HEADS_B, 1), 1)
        keep = pos < q_pos
        z = jnp.sum(qbd * kn_ref[0], axis=1, keepdims=True)
        lk = jnp.where(keep, _log_keep(z), 0.0)
        a = jnp.where(keep, jnp.exp(z + lk), 0.0)
        run_scr[...] = jnp.broadcast_to(lk, run_scr.shape)
        acc_scr[...] = vnc_ref[0] * _col_to_row(a)

    pages = range(SB_PAGES)
    nh = N_HEADS_B
    k_all = jnp.concatenate([k_refs[p][0].astype(BF16) for p in pages], axis=1)
    z_all = jnp.dot(qbd16, k_all, preferred_element_type=F32)
    z = jnp.concatenate([z_all[:, p * LANES:(p + 1) * LANES] for p in pages], axis=0)
    visit = g * SB_PAGES + _div_pow2(lax.broadcasted_iota(I32, z.shape, 0), nh)
    key_pos = (n_pages - 1 - visit) * Q_BLOCK + lax.broadcasted_iota(I32, z.shape, 1)
    keep = key_pos < q_pos
    lk = jnp.where(keep, _log_keep(z), 0.0)
    res = jnp.dot(_split_hi_lo(lk), w_ref[...], preferred_element_type=F32)
    run = run_scr[...]
    runs = []
    for p in pages:
        runs.append(run)
        run = run + res[p * nh:(p + 1) * nh, LANES:]
    run_scr[...] = run
    a = jnp.where(keep, jnp.exp(z + lk + res[:, :LANES] + jnp.concatenate(runs, axis=0)), 0.0)
    a_all = jnp.concatenate([a[p * nh:(p + 1) * nh] for p in pages], axis=1).astype(BF16)
    v_all = jnp.concatenate([v_refs[p][0].astype(BF16) for p in pages], axis=1)
    acc_scr[...] += lax.dot_general(v_all, a_all, _NT, preferred_element_type=F32)

    @pl.when(g == n_steps - 1)
    def _():
        r = lax.broadcasted_iota(I32, acc_scr.shape, 0)
        h = lax.broadcasted_iota(I32, acc_scr.shape, 1)
        o_ref[0] = jnp.sum(jnp.where(_div_pow2(r, HEAD_DIM) == h, acc_scr[...], 0.0), axis=1, keepdims=True)


def _sb_sample(page_table, qbd, k_new, v_new_col, cache_kt, cache_vt, past_len):
    bd, n_pages = page_table.shape
    width = cache_kt.shape[1]
    wmat = _suffix_matrix(1)
    per_b = lambda shape: pl.BlockSpec((1,) + shape, lambda b, g, pt: (b,) + (0,) * len(shape))
    kv_specs = _page_specs((width, Q_BLOCK), n_pages, True, SB_PAGES)
    return pl.pallas_call(
        functools.partial(_sb_sample_kernel, past_len=past_len),
        grid_spec=pltpu.PrefetchScalarGridSpec(
            num_scalar_prefetch=1,
            grid=(bd, n_pages // SB_PAGES),
            in_specs=[per_b((N_HEADS_B, width)), per_b((1, width)), per_b((width, 1)),
                      pl.BlockSpec(wmat.shape, lambda b, g, pt: (0, 0))] + kv_specs + kv_specs,
            out_specs=per_b((width, 1)),
            scratch_shapes=[pltpu.VMEM((N_HEADS_B, LANES), F32), pltpu.VMEM((width, N_HEADS_B), F32)],
        ),
        out_shape=jax.ShapeDtypeStruct((bd, width, 1), F32),
        compiler_params=_cparams(2),
        name="sb_sample",
    )(page_table, qbd, k_new, v_new_col, wmat, *([cache_kt] * SB_PAGES), *([cache_vt] * SB_PAGES))


def _route(logits, axis):
    slot = lax.broadcasted_iota(I32, logits.shape, axis)
    big = jnp.int32(ROUTER_LANES)

    def softmax_over(mask):
        m = jnp.max(jnp.where(mask, logits, NEG), axis=axis, keepdims=True)
        e = jnp.where(mask, jnp.exp(logits - m), 0.0)
        return e / jnp.sum(e, axis=axis, keepdims=True)

    def top1(prob, mask):
        best = jnp.max(jnp.where(mask, prob, -1.0), axis=axis, keepdims=True)
        idx = jnp.min(jnp.where(mask & (prob == best), slot, big), axis=axis, keepdims=True)
        return best, idx

    is_group = slot < N_GROUPS
    g_w, g_idx = top1(softmax_over(is_group), is_group)
    e_slot = slot - N_GROUPS
    in_group = (e_slot >= 0) & (e_slot < N_EXPERTS) & (_div_pow2(e_slot, EXPERTS_PER_GROUP) == g_idx)
    p_e = softmax_over(in_group)
    p1, i1 = top1(p_e, in_group)
    p2, i2 = top1(p_e, in_group & (slot != i1))
    denom = p1 + p2
    return slot, i1, p1 / denom * g_w, i2, p2 / denom * g_w


def _merge_kernel(x_ref, oa_ref, ob_ref, ga_ref, gb_ref, wpa, wpb, wout, ln2_ref, wr, br,
                  h_o, hn_o, route_o, *, grouped):
    mixed = (ga_ref[...] * jnp.dot(oa_ref[...], wpa[...], preferred_element_type=F32)
             + gb_ref[...] * jnp.dot(ob_ref[...], wpb[...], preferred_element_type=F32))
    h = x_ref[...] + jnp.dot(mixed.astype(BF16), wout[...], preferred_element_type=F32)
    h_o[...] = h
    ms = jnp.mean(h * h, axis=-1, keepdims=True)
    hn = (h * lax.rsqrt(ms + EPS) * ln2_ref[...]).astype(BF16)

    if grouped:
        half = hn.shape[1] // 2
        hi = lax.bitcast_convert_type(hn[:, :half].astype(F32), jnp.uint32)
        lo = lax.bitcast_convert_type(hn[:, half:].astype(F32), jnp.uint32)
        hn_o[...] = hi | (lo >> 16)
        logits_t = lax.dot_general(wr[...], hn, _NT, preferred_element_type=F32) + br[...]
        _, i1, w1, i2, w2 = _route(logits_t, 0)
        rows = [(i1 - N_GROUPS).astype(F32), (i2 - N_GROUPS).astype(F32), w1, w2]
        route_o[...] = jnp.concatenate(rows + [jnp.zeros_like(w1)] * (route_o.shape[0] - len(rows)), axis=0)
    else:
        hn_o[...] = hn
        logits = jnp.dot(hn, wr[...], preferred_element_type=F32) + br[...]
        slot, i1, w1, i2, w2 = _route(logits, 1)
        route_o[...] = jnp.where(slot == i1, w1, 0.0) + jnp.where(slot == i2, w2, 0.0)


def _merge(x2d, o_a, o_b, g_a, g_b, wpa, wpb, wout, ln2, wr, br, tm, grouped):
    n, d = x2d.shape
    row = lambda w: pl.BlockSpec((tm, w), lambda i: (i, 0))
    if grouped:
        consts = (wpa, wpb, wout, ln2, wr.T, br.T)
        hn_spec, hn_shape = row(d // 2), jax.ShapeDtypeStruct((n, d // 2), jnp.uint32)
        rt_spec = pl.BlockSpec((ROUTE_ROWS, tm), lambda i: (0, i))
        rt_shape = jax.ShapeDtypeStruct((ROUTE_ROWS, n), F32)
    else:
        consts = (wpa, wpb, wout, ln2, wr, br)
        hn_spec, hn_shape = row(d), jax.ShapeDtypeStruct((n, d), BF16)
        rt_spec, rt_shape = row(ROUTER_LANES), jax.ShapeDtypeStruct((n, ROUTER_LANES), F32)
    return pl.pallas_call(
        functools.partial(_merge_kernel, grouped=grouped),
        grid=(n // tm,),
        in_specs=[row(d), row(WIDTH_A), row(WIDTH_B), row(d), row(d)] + [_const_spec(c.shape) for c in consts],
        out_specs=[row(d), hn_spec, rt_spec],
        out_shape=[jax.ShapeDtypeStruct((n, d), F32), hn_shape, rt_shape],
        compiler_params=_cparams(1),
        name="merge",
    )(x2d, o_a, o_b, g_a, g_b, *consts)


def _moe_kernel(hn_ref, comb_ref, h_ref, wgu_ref, wd_ref, lnf_ref, y_ref, acc_scr, *, final_norm):
    e = pl.program_id(1)

    @pl.when(e == 0)
    def _():
        acc_scr[...] = jnp.zeros(acc_scr.shape, F32)

    d_expert = wd_ref.shape[1]
    gu = jnp.dot(hn_ref[...], wgu_ref[0], preferred_element_type=F32)
    gate, up = gu[:, :d_expert], gu[:, d_expert:]
    comb = comb_ref[...]
    lane = lax.broadcasted_iota(I32, comb.shape, 1)
    c_e = jnp.sum(jnp.where(lane == e + N_GROUPS, comb, 0.0), axis=1, keepdims=True)
    hid = gate * _sigmoid(gate) * up * c_e
    acc_scr[...] += jnp.dot(hid.astype(BF16), wd_ref[0], preferred_element_type=F32)

    @pl.when(e == pl.num_programs(1) - 1)
    def _():
        xo = h_ref[...] + acc_scr[...]
        if final_norm:
            ms = jnp.mean(xo * xo, axis=-1, keepdims=True)
            xo = xo * lax.rsqrt(ms + EPS) * lnf_ref[...]
        y_ref[...] = xo


def _moe(hn, comb, h, wgu, wd, lnf, tm, final_norm):
    n, d = h.shape
    n_exp, _, two_f = wgu.shape
    row = lambda w: pl.BlockSpec((tm, w), lambda i, e: (i, 0))
    return pl.pallas_call(
        functools.partial(_moe_kernel, final_norm=final_norm),
        grid=(n // tm, n_exp),
        in_specs=[row(d), row(ROUTER_LANES), row(d),
                  pl.BlockSpec((1, d, two_f), lambda i, e: (e, 0, 0)),
                  pl.BlockSpec((1, two_f // 2, d), lambda i, e: (e, 0, 0)),
                  pl.BlockSpec((1, d), lambda i, e: (0, 0))],
        out_specs=row(d),
        out_shape=jax.ShapeDtypeStruct((n, d), F32),
        scratch_shapes=[pltpu.VMEM((tm, d), F32)],
        compiler_params=_cparams(2),
        name="moe",
    )(hn, comb, h, wgu, wd, lnf)


def _moe_plan_kernel(rt_ref, before_ref, lower_ref, pos_o, tt_o):
    ch = rt_ref.shape[1]
    e1 = rt_ref[0:1, :].astype(I32)
    e2 = rt_ref[1:2, :].astype(I32)
    sub = lax.broadcasted_iota(I32, (ROUTER_LANES, ch), 0)
    a1, a2 = sub == e1, sub == e2
    chosen = (jnp.where(a1, 1.0, 0.0) + jnp.where(a2, 1.0, 0.0)).astype(BF16)
    before = jnp.dot(chosen, before_ref[...], preferred_element_type=F32)
    cnt = before[:, ch - 1:ch] + chosen[:, ch - 1:ch].astype(F32)
    ntile = jnp.floor((cnt + (MOE_ROWS - 1)) * (1.0 / MOE_ROWS))
    base_t = jnp.dot(lower_ref[...], jnp.broadcast_to(ntile, (ROUTER_LANES, LANES)).astype(BF16),
                     preferred_element_type=F32)[:, 0:1]
    slot = before + base_t * MOE_ROWS
    pos = [jnp.sum(jnp.where(a, slot, 0.0), axis=0, keepdims=True) for a in (a1, a2)]
    pos_o[0] = jnp.concatenate(pos + [jnp.zeros_like(pos[0])] * (pos_o.shape[1] - 2), axis=0).astype(I32)

    tile = lax.broadcasted_iota(I32, (ROUTER_LANES, LANES), 1).astype(F32)
    exp = lax.broadcasted_iota(I32, (ROUTER_LANES, LANES), 0)
    is_exp = exp < N_EXPERTS
    te = jnp.sum(jnp.where(is_exp & (base_t + ntile <= tile), 1, 0), axis=0, keepdims=True)
    total = jnp.sum(jnp.where(is_exp, ntile, 0.0), axis=0, keepdims=True).astype(I32) + jnp.zeros_like(te)
    rows = [jnp.minimum(te, N_EXPERTS - 1), total]
    tt_o[0] = jnp.concatenate(rows + [jnp.zeros_like(te)] * (tt_o.shape[1] - len(rows)), axis=0)


def _moe_routed_kernel(tt_ref, pos_ref, rt_ref, hnp_ref, h_ref, *refs, max_tiles, final_norm):
    w_refs = refs[:2 * MOE_TILES_PER_STEP]
    lnf_ref, y_ref, rows_scr = refs[2 * MOE_TILES_PER_STEP:]
    c = pl.program_id(0)
    i = pl.program_id(1)
    ch, half = hnp_ref.shape
    n_tiles = tt_ref[(c * 2 + 1) * max_tiles]

    @pl.when(i == 0)
    def _():
        rows_scr[:, :half] = jnp.zeros((rows_scr.shape[0], half), F32)

        def place(t, carry):
            row = lax.bitcast_convert_type(hnp_ref[pl.ds(t, 1), :], F32)
            for k in range(TOP_E):
                rows_scr[pl.ds(pos_ref[0, k, t], 1), :half] = row
            return carry

        lax.fori_loop(0, ch, place, 0, unroll=MOE_UNROLL)

    def expert_tile(ti, wgu_ref, wd_ref):
        tile = pl.ds(pl.multiple_of(ti * MOE_ROWS, MOE_ROWS), MOE_ROWS)
        packed = lax.bitcast_convert_type(rows_scr[tile, :half], jnp.uint32)
        first = lax.bitcast_convert_type(packed & jnp.uint32(0xFFFF0000), F32).astype(BF16)
        second = lax.bitcast_convert_type(packed << 16, F32).astype(BF16)
        x = jnp.concatenate([first, second], axis=1)
        d_expert = wd_ref.shape[1]
        gu = jnp.dot(x, wgu_ref[0], preferred_element_type=F32)
        gate, up = gu[:, :d_expert], gu[:, d_expert:]
        hid = gate * _sigmoid(gate) * up
        rows_scr[tile, :] = jnp.dot(hid.astype(BF16), wd_ref[0], preferred_element_type=F32)

    for u in range(MOE_TILES_PER_STEP):
        ti = i * MOE_TILES_PER_STEP + u
        pl.when(ti < n_tiles)(functools.partial(expert_tile, ti, w_refs[2 * u], w_refs[2 * u + 1]))

    @pl.when(i == pl.num_programs(1) - 1)
    def _():
        def combine(t, carry):
            acc = h_ref[pl.ds(t, 1), :]
            for k in range(TOP_E):
                acc = acc + rt_ref[TOP_E + k, t] * rows_scr[pl.ds(pos_ref[0, k, t], 1), :]
            y_ref[pl.ds(t, 1), :] = acc
            return carry

        lax.fori_loop(0, ch, combine, 0, unroll=MOE_UNROLL)
        if final_norm:
            xo = y_ref[...]
            ms = jnp.mean(xo * xo, axis=-1, keepdims=True)
            y_ref[...] = xo * lax.rsqrt(ms + EPS) * lnf_ref[...]


def _moe_routed(hnp, rt, h, wgu, wd, lnf, final_norm):
    n, d = h.shape
    ch = MOE_CHUNK
    assert n % ch == 0
    n_chunks = n // ch
    max_tiles = TOP_E * ch // MOE_ROWS + N_EXPERTS
    assert max_tiles <= LANES and max_tiles % MOE_TILES_PER_STEP == 0
    before = jnp.asarray((np.arange(ch)[:, None] < np.arange(ch)[None, :]).astype(np.float32), BF16)
    lower = jnp.asarray((np.arange(ROUTER_LANES)[:, None] > np.arange(ROUTER_LANES)[None, :]).astype(np.float32),
                        BF16)
    pos, tt = pl.pallas_call(
        _moe_plan_kernel,
        grid=(n_chunks,),
        in_specs=[pl.BlockSpec((ROUTE_ROWS, ch), lambda c: (0, c)), _const_spec(before.shape),
                  _const_spec(lower.shape)],
        out_specs=[pl.BlockSpec((1, ROUTE_ROWS, ch), lambda c: (c, 0, 0)),
                   pl.BlockSpec((1, ROUTE_ROWS, LANES), lambda c: (c, 0, 0))],
        out_shape=[jax.ShapeDtypeStruct((n_chunks, ROUTE_ROWS, ch), I32),
                   jax.ShapeDtypeStruct((n_chunks, ROUTE_ROWS, LANES), I32)],
        compiler_params=_cparams(1),
        name="moe_plan",
    )(rt, before, lower)
    tt_flat = tt[:, :2, :max_tiles].reshape(-1)
    n_exp, _, two_f = wgu.shape
    tps = MOE_TILES_PER_STEP
    w_specs, w_args = [], []
    for u in range(tps):
        expert_of = lambda c, i, tt, u=u: (tt[c * 2 * max_tiles + i * tps + u], 0, 0)
        w_specs += [pl.BlockSpec((1, d, two_f), expert_of), pl.BlockSpec((1, two_f // 2, d), expert_of)]
        w_args += [wgu, wd]
    return pl.pallas_call(
        functools.partial(_moe_routed_kernel, max_tiles=max_tiles, final_norm=final_norm),
        grid_spec=pltpu.PrefetchScalarGridSpec(
            num_scalar_prefetch=1,
            grid=(n_chunks, max_tiles // tps),
            in_specs=[pl.BlockSpec((1, ROUTE_ROWS, ch), lambda c, i, tt: (c, 0, 0), memory_space=pltpu.SMEM),
                      pl.BlockSpec((ROUTE_ROWS, ch), lambda c, i, tt: (0, c), memory_space=pltpu.SMEM),
                      pl.BlockSpec((ch, d // 2), lambda c, i, tt: (c, 0)),
                      pl.BlockSpec((ch, d), lambda c, i, tt: (c, 0))] + w_specs
            + [pl.BlockSpec((1, d), lambda c, i, tt: (0, 0))],
            out_specs=pl.BlockSpec((ch, d), lambda c, i, tt: (c, 0)),
            scratch_shapes=[pltpu.VMEM((max_tiles * MOE_ROWS, d), F32)],
        ),
        out_shape=jax.ShapeDtypeStruct((n, d), F32),
        compiler_params=_cparams(2),
        name="moe_routed",
    )(tt_flat, pos, rt, hnp, h, *w_args, lnf)


def _tile(n, pref):
    return pref if n % pref == 0 else n


def kernel(x_prompt, x_sample, cache_k_a, cache_v_a, cache_kidx_a, cache_k_b, cache_v_b, page_table,
           rel_bias_table, ln1_g, w_in, w_proj_a, w_proj_b, w_out, ln2_g, w_router_group, b_router_group,
           w_router_expert, b_router_expert, w_gate, w_up, w_down, ln_f_g):
    bsz, seq, d = x_prompt.shape
    bd, t_new, _ = x_sample.shape
    assert t_new == 1, "the sample group decodes one token per sequence"
    depth = w_in.shape[0]
    pool = cache_k_a.shape[1]
    n_pages = page_table.shape[1]
    past_len = n_pages * Q_BLOCK
    assert n_pages % SMALL_PAGES == 0 and n_pages % SB_PAGES == 0
    n_p, n_s = bsz * seq, bd * t_new
    d_expert = w_gate.shape[-1]

    xp, xs = x_prompt.reshape(n_p, d), x_sample.reshape(n_s, d)
    new_p, new_s = [], []
    for l in range(depth):
        w_pieces = _split_w_in(w_in[l])
        wpa, wpb, wout = (w.astype(BF16) for w in (w_proj_a[l], w_proj_b[l], w_out[l]))
        ln2 = ln2_g[l].reshape(1, d)
        wr = jnp.concatenate([w_router_group[l], jnp.moveaxis(w_router_expert[l], 0, 1).reshape(d, N_EXPERTS),
                              jnp.zeros((d, ROUTER_LANES - N_GROUPS - N_EXPERTS), F32)], axis=1).astype(BF16)
        br = jnp.concatenate([b_router_group[l], b_router_expert[l].reshape(N_EXPERTS),
                              jnp.zeros((ROUTER_LANES - N_GROUPS - N_EXPERTS,), F32)]).reshape(1, ROUTER_LANES)
        wgu = jnp.concatenate([w_gate[l], w_up[l]], axis=-1).reshape(N_EXPERTS, d, 2 * d_expert).astype(BF16)
        wd = w_down[l].reshape(N_EXPERTS, d_expert, d).astype(BF16)
        lnf = ln_f_g.reshape(1, d)

        def ffn(x2d, o_a, o_b, g_a, g_b, tm_merge, tm_moe):
            final_norm = l == depth - 1
            routed = x2d.shape[0] % MOE_CHUNK == 0
            h, hn, route = _merge(x2d, o_a, o_b, g_a, g_b, wpa, wpb, wout, ln2, wr, br, tm_merge, grouped=routed)
            if routed:
                return _moe_routed(hn, route, h, wgu, wd, lnf, final_norm)
            return _moe(hn, route, h, wgu, wd, lnf, tm_moe, final_norm)

        (qa, ka, va, kad, vad, qi, ki, kid, wi, qb, kb, vb, kb16, vb16, ga, gb) = _project(
            xp, ln1_g[l], w_pieces, _tile(seq, 512), seq_len=seq)
        r3 = lambda a: a.reshape(bsz, seq, a.shape[-1])
        n_top = min(TOPK_MAX, seq // 4)
        wit = jnp.swapaxes(r3(wi), 1, 2)
        vadt = jnp.swapaxes(vad.reshape(bsz, seq // Q_BLOCK, Q_BLOCK, 2 * LANES), 2, 3)
        o_at = _dsa_prompt(rel_bias_table, r3(qi), wit, r3(kid), r3(qa), r3(kad), vadt, n_top)
        o_a = jnp.swapaxes(o_at, 1, 2)
        o_b = _sb_prompt(r3(qb), r3(kb16), r3(vb16))
        xp = ffn(xp, o_a.reshape(n_p, WIDTH_A), o_b.reshape(n_p, WIDTH_B), ga, gb, _tile(n_p, 512), _tile(n_p, 1024))
        new_p.append((ka, va, ki, kb, vb))

        (qa, ka, va, _, _, qi, ki, _, wi, qb, kb, vb, _, _, ga, gb) = _project(xs, ln1_g[l], w_pieces, n_s)
        n_top = min(TOPK_MAX, (past_len + t_new) // 4)
        qi3 = qi.astype(F32).reshape(bd, N_IDX_HEADS, IDX_DIM)
        wi3 = wi.reshape(bd, N_IDX_HEADS, 1)
        paged_t = lambda c: jnp.moveaxis(c, 1, -1).reshape(pool, -1, Q_BLOCK)
        scores = _idx_sample(page_table, qi3, wi3, paged_t(cache_kidx_a[l]))
        madd, maddn = _select_sample(scores.reshape(bd, n_pages, LANES), qi3, wi3, ki.reshape(bd, 1, IDX_DIM), n_top)
        qa4 = qa.astype(F32).reshape(bd, N_HEADS_A, 1, HEAD_DIM) * (HEAD_DIM ** -0.5)
        kv_of_head = (np.arange(N_HEADS_A)[:, None] // GROUP_A) == np.arange(N_KV_A)[None, :]
        qz = jnp.where(kv_of_head[None, :, :, None], qa4, 0.0).reshape(bd, N_HEADS_A, N_KV_A * HEAD_DIM)
        logits = _dsa_logits(page_table, rel_bias_table, qz, madd.reshape(bd, n_pages, 1, LANES),
                             paged_t(cache_k_a[l]), past_len)
        o_a = _dsa_pv(page_table, rel_bias_table, logits, qz, maddn, ka.reshape(bd, 1, -1), va.reshape(bd, -1, 1),
                      paged_t(cache_v_a[l]))
        o_a = jnp.swapaxes(o_a, 1, 2)
        qb4 = qb.astype(F32).reshape(bd, N_HEADS_B, 1, HEAD_DIM) * (HEAD_DIM ** -0.5)
        own = np.eye(N_HEADS_B, dtype=bool)
        qbd = jnp.where(own[None, :, :, None], qb4, 0.0).reshape(bd, N_HEADS_B, WIDTH_B)
        o_b = _sb_sample(page_table, qbd, kb.reshape(bd, 1, -1), vb.reshape(bd, -1, 1),
                         paged_t(cache_k_b[l]), paged_t(cache_v_b[l]), past_len)
        xs = ffn(xs, o_a.reshape(n_s, WIDTH_A).astype(BF16), o_b.reshape(n_s, WIDTH_B).astype(BF16), ga, gb, n_s, n_s)
        new_s.append((ka, va, ki, kb, vb))

    outs = [xp.reshape(bsz, seq, d), xs.reshape(bd, t_new, d)]
    tails = ((N_KV_A, HEAD_DIM), (N_KV_A, HEAD_DIM), (IDX_DIM,), (N_HEADS_B, HEAD_DIM), (N_HEADS_B, HEAD_DIM))
    for idx, tail in enumerate(tails):
        outs.append(jnp.stack([jnp.moveaxis(r[idx].reshape((bsz,) + tail + (seq,)), -1, 1) for r in new_p]))
    for idx, tail in enumerate(tails):
        outs.append(jnp.stack([r[idx].reshape((bd, t_new) + tail) for r in new_s]))
    return tuple(outs)
```

```python
import functools
import math

import jax
import jax.numpy as jnp
import numpy as np
from jax import lax
from jax.experimental import pallas as pl
from jax.experimental.pallas import tpu as pltpu

F32 = jnp.float32
BF16 = jnp.bfloat16
I32 = jnp.int32

HEAD_DIM = 64
N_HEADS_A = 8
N_KV_A = 2
GROUP_A = N_HEADS_A // N_KV_A
N_IDX_HEADS = 8
IDX_DIM = 64
N_HEADS_B = 8
WIDTH_A = N_HEADS_A * HEAD_DIM
WIDTH_B = N_HEADS_B * HEAD_DIM
TOPK_MAX = 256
N_BUCKETS = 32
MAX_DISTANCE = 128
N_GROUPS = 4
EXPERTS_PER_GROUP = 8
N_EXPERTS = N_GROUPS * EXPERTS_PER_GROUP
TOP_E = 2
Q_BLOCK = 128
EPS = 1e-6

Q_TILE = 256

LANES = 128
VMEM_LIMIT_BYTES = 56 * 1024 * 1024

SMALL_PAGES = 32
SB_PAGES = 16
SELECT_SEQS = 8
MOE_CHUNK = 1024
MOE_ROWS = 128
MOE_TILES_PER_STEP = 3
MOE_UNROLL = 8
ROUTE_ROWS = 8
NEG = -1e30
INT_MIN = -(2 ** 31)
INT16_MIN = -(2 ** 15)
I16 = jnp.int16
ROUTER_LANES = LANES

_NT = (((1,), (1,)), ((), ()))


def _cparams(n_axes):
    return pltpu.CompilerParams(dimension_semantics=("arbitrary",) * n_axes,
                                vmem_limit_bytes=VMEM_LIMIT_BYTES)


def _const_spec(shape):
    nd = len(shape)
    return pl.BlockSpec(shape, lambda *_: (0,) * nd)


def _div_pow2(x, n):
    assert n & (n - 1) == 0
    return x >> (n.bit_length() - 1)


def _sigmoid(x):
    return 1.0 / (1.0 + jnp.exp(-x))


def _half_masks(dtype):
    lane = lax.broadcasted_iota(I32, (LANES, LANES), 1)
    lo = jnp.where(lane < HEAD_DIM, 1.0, 0.0).astype(dtype)
    hi = jnp.where(lane < HEAD_DIM, 0.0, 1.0).astype(dtype)
    return lo, hi


def _pair_block_diag(blk, lo, hi):
    return jnp.concatenate([blk * lo, blk * hi], axis=0)


def _row_half_masks(dtype):
    r = lax.broadcasted_iota(I32, (LANES, LANES), 0)
    top = jnp.where(r < HEAD_DIM, 1.0, 0.0).astype(dtype)
    bottom = jnp.where(r < HEAD_DIM, 0.0, 1.0).astype(dtype)
    return top, bottom


def _pair_block_diag_t(blk_t, top, bottom):
    return jnp.concatenate([blk_t * top, blk_t * bottom], axis=1)


def _fori_groups(n_groups, group, body, init):
    def trip(o, carry):
        for u in range(group):
            carry = body(o * group + u, carry)
        return carry
    return lax.fori_loop(0, n_groups, trip, init)


def _col_to_row(x):
    n = x.shape[0]
    eye = lax.broadcasted_iota(I32, (n, n), 0) == lax.broadcasted_iota(I32, (n, n), 1)
    return jnp.sum(jnp.where(eye, x, 0.0), axis=0, keepdims=True)


def _order_key(score):
    score = jnp.where(score == 0.0, 0.0, score)
    b = lax.bitcast_convert_type(score, I32)
    return b ^ ((b >> 31) & jnp.int32(0x7FFFFFFF))


def _t5_bucket(dist):
    n = jnp.maximum(dist, 0)
    max_exact = N_BUCKETS // 2
    nf = jnp.maximum(n, max_exact).astype(F32)
    large = max_exact + (jnp.log(nf / max_exact) / math.log(MAX_DISTANCE / max_exact)
                         * (N_BUCKETS - max_exact)).astype(I32)
    return jnp.where(n < max_exact, n, jnp.minimum(large, N_BUCKETS - 1))


def _proj_kernel(x_ref, g_ref, wqa, wkv, wkvd, wqi, wkiw, wqb, wkb, wvb, wga, wgb,
                 qa_o, ka_o, va_o, kad_o, vad_o, qi_o, ki_o, kid_o, wi_o,
                 qb_o, kb_o, vb_o, kb16_o, vb16_o, ga_o, gb_o, *, feature_major):
    x = x_ref[...]
    ms = jnp.mean(x * x, axis=-1, keepdims=True)
    xn = (x * lax.rsqrt(ms + EPS) * g_ref[...]).astype(BF16)

    def mm(w):
        return jnp.dot(xn, w[...], preferred_element_type=F32)

    def put_new_rows(o_ref, val):
        if feature_major:
            o_ref[0] = val.T
        else:
            o_ref[...] = val

    qa_o[...] = mm(wqa).astype(BF16)
    kv = mm(wkv)
    put_new_rows(ka_o, kv[:, :N_KV_A * HEAD_DIM])
    put_new_rows(va_o, kv[:, N_KV_A * HEAD_DIM:])
    kvd = mm(wkvd).astype(BF16)
    kad_o[...] = kvd[:, :2 * LANES]
    vad_o[...] = kvd[:, 2 * LANES:]
    qi_o[...] = mm(wqi).astype(BF16)
    kiw = mm(wkiw)
    put_new_rows(ki_o, kiw[:, :IDX_DIM])
    kid_o[...] = kiw[:, :LANES].astype(BF16)
    wi_o[...] = kiw[:, LANES:LANES + N_IDX_HEADS] * (N_IDX_HEADS ** -0.5)
    qb_o[...] = mm(wqb).astype(BF16)
    kb = mm(wkb)
    put_new_rows(kb_o, kb)
    kb16_o[...] = kb.astype(BF16)
    vb = mm(wvb)
    put_new_rows(vb_o, vb)
    vb16_o[...] = vb.astype(BF16)
    ga_o[...] = _sigmoid(mm(wga))
    gb_o[...] = _sigmoid(mm(wgb))


def _split_w_in(w_in):
    d_model = w_in.shape[0]
    widths = (WIDTH_A, N_KV_A * HEAD_DIM, N_KV_A * HEAD_DIM, N_IDX_HEADS * IDX_DIM, N_IDX_HEADS, IDX_DIM,
              WIDTH_B, WIDTH_B, WIDTH_B, d_model, d_model)
    points = [int(p) for p in np.cumsum(widths)[:-1]]
    qa, ka, va, qi, wi, ki, qb, kb, vb, ga, gb = jnp.split(w_in, points, axis=1)
    dup = lambda a: jnp.concatenate([a[:, :HEAD_DIM], a[:, :HEAD_DIM], a[:, HEAD_DIM:], a[:, HEAD_DIM:]], axis=1)
    wkv = jnp.concatenate([ka, va], axis=1)
    wkvd = jnp.concatenate([dup(ka), dup(va)], axis=1)
    wkiw = jnp.concatenate([ki, ki, wi, jnp.zeros((d_model, LANES - N_IDX_HEADS), w_in.dtype)], axis=1)
    return tuple(a.astype(BF16) for a in (qa, wkv, wkvd, qi, wkiw, qb, kb, vb, ga, gb))


_NEW_ROW_OUTPUTS = (1, 2, 6, 10, 11)


def _project(x2d, ln_g, w_pieces, tm, seq_len=None):
    n, d = x2d.shape
    assert n % tm == 0
    out_defs = [
        (WIDTH_A, BF16), (N_KV_A * HEAD_DIM, F32), (N_KV_A * HEAD_DIM, F32), (2 * LANES, BF16), (2 * LANES, BF16),
        (N_IDX_HEADS * IDX_DIM, BF16), (IDX_DIM, F32), (LANES, BF16), (N_IDX_HEADS, F32),
        (WIDTH_B, BF16), (WIDTH_B, F32), (WIDTH_B, F32), (WIDTH_B, BF16), (WIDTH_B, BF16), (d, F32), (d, F32)]
    row = lambda i: (i, 0)
    out_specs = [pl.BlockSpec((tm, w), row) for w, _ in out_defs]
    out_shape = [jax.ShapeDtypeStruct((n, w), dt) for w, dt in out_defs]
    if seq_len is not None:
        assert seq_len % tm == 0
        tiles = seq_len // tm
        for k in _NEW_ROW_OUTPUTS:
            w, dt = out_defs[k]
            out_specs[k] = pl.BlockSpec((1, w, tm), lambda i: (i // tiles, 0, i % tiles))
            out_shape[k] = jax.ShapeDtypeStruct((n // seq_len, w, seq_len), dt)
    return pl.pallas_call(
        functools.partial(_proj_kernel, feature_major=seq_len is not None),
        grid=(n // tm,),
        in_specs=[pl.BlockSpec((tm, d), row), _const_spec((1, d))] + [_const_spec(w.shape) for w in w_pieces],
        out_specs=out_specs,
        out_shape=out_shape,
        compiler_params=_cparams(1),
        name="proj",
    )(x2d, ln_g.reshape(1, d), *w_pieces)


def _dsa_prompt_kernel(tab_ref, qi_ref, wit_ref, kid_ref, qa_ref, kad_ref, vadt_ref, tril_ref, o_ref,
                       key_scr, hi16_scr, lo16_scr, lg_scr, bias_scr, acc_scr, *, n_top):
    b = pl.program_id(0)
    i = pl.program_id(1)
    kb, qt = Q_BLOCK, Q_TILE
    bpt = qt // kb
    row = lax.broadcasted_iota(I32, (kb, qt), 0)
    col = lax.broadcasted_iota(I32, (kb, qt), 1)
    lo, hi = _half_masks(BF16)
    top, bottom = _row_half_masks(BF16)
    d_min = 1 - bpt

    def block_dist(j):
        return i * bpt - j

    @pl.when((b == 0) & (i == 0))
    def _():
        for d in range(d_min, 3):
            bucket = _t5_bucket(d * kb + col - row)
            for h in range(N_HEADS_A):
                def sel(bk, acc):
                    return jnp.where(bucket == bk, tab_ref[bk, h], acc)
                tile = lax.fori_loop(0, N_BUCKETS, sel, jnp.zeros((kb, qt), F32))
                bias_scr[d - d_min, h // 2, (h % 2) * kb:(h % 2 + 1) * kb, :] = tile

    def causal(j):
        return block_dist(j) * kb + col - row >= 0

    w = wit_ref[0] * (IDX_DIM ** -0.5)
    wrows = [w[h:h + 1, :] for h in range(N_IDX_HEADS)]

    def score_body(j, carry):
        kblk = kid_ref[0, pl.ds(pl.multiple_of(j * kb, kb), kb), :]
        kbd = _pair_block_diag(kblk, lo, hi)
        acc = jnp.zeros((kb, qt), F32)
        for p in range(N_IDX_HEADS // 2):
            r = lax.dot_general(kbd, qi_ref[0, :, p * LANES:(p + 1) * LANES], _NT,
                                preferred_element_type=F32)
            acc = acc + jnp.maximum(r[:kb], 0.0) * wrows[2 * p] + jnp.maximum(r[kb:], 0.0) * wrows[2 * p + 1]
        key = _order_key(jnp.where(causal(j), acc, -jnp.inf))
        key_scr[j] = key
        hi16_scr[j] = (key >> 16).astype(I16)
        return carry

    _fori_groups(i + 1, bpt, score_body, 0)

    def count(ref, pred, dtype):
        def body(j, acc):
            return acc + jnp.where(pred(ref[j]), jnp.ones((), dtype), jnp.zeros((), dtype))
        acc = _fori_groups(i + 1, bpt, body, jnp.zeros((kb, qt), dtype))
        return jnp.sum(acc.astype(I32), axis=0, keepdims=True)

    def bisect16(ref, wanted):
        t0 = jnp.where(count(ref, lambda k: k >= 0, I16) >= wanted, 0, INT16_MIN).astype(I32)

        def bit_body(b, t):
            cand = t | lax.shift_left(jnp.int32(1), jnp.int32(14) - b)
            c16 = cand.astype(I16)
            return jnp.where(count(ref, lambda k: k >= c16, I16) >= wanted, cand, t)

        return lax.fori_loop(0, 15, bit_body, t0)

    t_hi = bisect16(hi16_scr, n_top)
    t_hi16 = t_hi.astype(I16)
    above = count(hi16_scr, lambda k: k > t_hi16, I16)

    def low_body(j, carry):
        low = ((key_scr[j] & 0xFFFF) + INT16_MIN).astype(I16)
        lo16_scr[j] = jnp.where(hi16_scr[j] == t_hi16, low, jnp.int16(INT16_MIN))
        return carry

    _fori_groups(i + 1, bpt, low_body, 0)
    t_lo = bisect16(lo16_scr, n_top - above)
    thr = lax.shift_left(t_hi, 16) | ((t_lo - INT16_MIN) & 0xFFFF)
    need = (n_top - count(key_scr, lambda k: k > thr, I32)).astype(F32)

    tril = tril_ref[...]

    n_pairs = N_HEADS_A // 2
    q_ps = [(qa_ref[0, :, p * LANES:(p + 1) * LANES].astype(F32) * (HEAD_DIM ** -0.5)).astype(BF16)
            for p in range(n_pairs)]

    def logits_body(j, carry):
        ties_before, maxes = carry[0], carry[1:]
        key = key_scr[j]
        eq = key == thr
        pre = jnp.dot(tril, jnp.where(eq, 1.0, 0.0).astype(BF16), preferred_element_type=F32)
        take = (key > thr) | (eq & (ties_before + pre <= need))
        madd = jnp.where(take & causal(j), 0.0, NEG)
        rows = pl.ds(pl.multiple_of(j * kb, kb), kb)
        dsel = jnp.minimum(block_dist(j), 2) - d_min
        out = [ties_before + pre[kb - 1:kb, :]]
        for p in range(n_pairs):
            c = (2 * p) // GROUP_A
            kbd = _pair_block_diag(kad_ref[0, rows, c * LANES:(c + 1) * LANES], lo, hi)
            lg = lax.dot_general(kbd, q_ps[p], _NT, preferred_element_type=F32) + bias_scr[dsel, p]
            for half in range(2):
                lgh = lg[half * kb:(half + 1) * kb] + madd
                lg_scr[j, p, half * kb:(half + 1) * kb, :] = lgh
                out.append(jnp.maximum(maxes[2 * p + half], jnp.max(lgh, axis=0, keepdims=True)))
        return tuple(out)

    neg = jnp.full((1, qt), NEG, F32)
    zero = jnp.zeros((1, qt), F32)
    maxes = _fori_groups(i + 1, bpt, logits_body, (zero,) + (neg,) * N_HEADS_A)[1:]
    acc_scr[...] = jnp.zeros(acc_scr.shape, F32)

    def pv_body(j, sums):
        out = []
        for p in range(n_pairs):
            c = (2 * p) // GROUP_A
            vbd = _pair_block_diag_t(vadt_ref[0, j, c * LANES:(c + 1) * LANES, :], top, bottom)
            p0 = jnp.exp(lg_scr[j, p, :kb, :] - maxes[2 * p])
            p1 = jnp.exp(lg_scr[j, p, kb:, :] - maxes[2 * p + 1])
            out += [sums[2 * p] + jnp.sum(p0, axis=0, keepdims=True),
                    sums[2 * p + 1] + jnp.sum(p1, axis=0, keepdims=True)]
            acc_scr[p] += jnp.dot(vbd, jnp.concatenate([p0, p1], axis=0).astype(BF16),
                                  preferred_element_type=F32)
        return tuple(out)

    sums = _fori_groups(i + 1, bpt, pv_body, (zero,) * N_HEADS_A)
    for p in range(n_pairs):
        denom = jnp.where(row < HEAD_DIM, sums[2 * p], sums[2 * p + 1])
        o_ref[0, p * LANES:(p + 1) * LANES, :] = (acc_scr[p] / denom).astype(o_ref.dtype)


def _dsa_prompt(bias_table, qi, wit, kid, qa, kad, vadt, n_top):
    bsz, seq, _ = qi.shape
    assert seq % Q_TILE == 0 and Q_TILE % Q_BLOCK == 0
    nkb = seq // Q_BLOCK
    n_bias = Q_TILE // Q_BLOCK + 2
    tril = (np.arange(Q_BLOCK)[:, None] >= np.arange(Q_BLOCK)[None, :]).astype(np.float32)
    qblk = lambda w: pl.BlockSpec((1, Q_TILE, w), lambda b, i: (b, i, 0))
    full = lambda w: pl.BlockSpec((1, seq, w), lambda b, i: (b, 0, 0))
    return pl.pallas_call(
        functools.partial(_dsa_prompt_kernel, n_top=n_top),
        grid=(bsz, seq // Q_TILE),
        in_specs=[pl.BlockSpec(memory_space=pltpu.SMEM), qblk(N_IDX_HEADS * IDX_DIM),
                  pl.BlockSpec((1, N_IDX_HEADS, Q_TILE), lambda b, i: (b, 0, i)), full(LANES),
                  qblk(WIDTH_A), full(2 * LANES),
                  pl.BlockSpec((1, nkb, 2 * LANES, Q_BLOCK), lambda b, i: (b, 0, 0, 0)),
                  _const_spec((Q_BLOCK, Q_BLOCK))],
        out_specs=pl.BlockSpec((1, WIDTH_A, Q_TILE), lambda b, i: (b, 0, i)),
        out_shape=jax.ShapeDtypeStruct((bsz, WIDTH_A, seq), BF16),
        scratch_shapes=[pltpu.VMEM((nkb, Q_BLOCK, Q_TILE), I32), pltpu.VMEM((nkb, Q_BLOCK, Q_TILE), I16),
                        pltpu.VMEM((nkb, Q_BLOCK, Q_TILE), I16),
                        pltpu.VMEM((nkb, N_HEADS_A // 2, 2 * Q_BLOCK, Q_TILE), F32),
                        pltpu.VMEM((n_bias, N_HEADS_A // 2, 2 * Q_BLOCK, Q_TILE), F32),
                        pltpu.VMEM((N_HEADS_A // 2, LANES, Q_TILE), F32)],
        compiler_params=_cparams(2),
        name="dsa_prompt",
    )(bias_table, qi, wit, kid, qa, kad, vadt, jnp.asarray(tril, BF16))


def _suffix_matrix(n_heads):
    n = n_heads * Q_BLOCK
    r = np.arange(n)
    same = (r[:, None] // Q_BLOCK) == (r[None, :] // Q_BLOCK)
    u = same & (r[:, None] > r[None, :])
    w = np.concatenate([u, same], axis=1).astype(np.float32)
    return jnp.asarray(np.concatenate([w, w], axis=0), BF16)


def _log_keep(z):
    return -(jnp.maximum(z, 0.0) + jnp.log(1.0 + jnp.exp(-jnp.abs(z))))


def _split_hi_lo(x):
    hi = x.astype(BF16)
    lo = (x - hi.astype(F32)).astype(BF16)
    return jnp.concatenate([hi, lo], axis=1)


def _sb_prompt_kernel(qb_ref, kb_ref, vb_ref, w_ref, o_ref, run_scr, acc_scr):
    i = pl.program_id(1)
    kb, qt = Q_BLOCK, Q_TILE
    bpt = qt // kb
    n_pairs = N_HEADS_B // 2
    row = lax.broadcasted_iota(I32, (qt, 2 * kb), 0)
    col = lax.broadcasted_iota(I32, (qt, 2 * kb), 1) & (kb - 1)
    lo, hi = _half_masks(BF16)
    q_ps = [(qb_ref[0, :, p * LANES:(p + 1) * LANES].astype(F32) * (HEAD_DIM ** -0.5)).astype(BF16)
            for p in range(n_pairs)]

    def block(j, edge, first):
        rows = pl.ds(pl.multiple_of(j * kb, kb), kb)
        if edge:
            strict = j * kb + col < i * qt + row
        for p in range(n_pairs):
            kbd = _pair_block_diag(kb_ref[0, rows, p * LANES:(p + 1) * LANES], lo, hi)
            vbd = _pair_block_diag(vb_ref[0, rows, p * LANES:(p + 1) * LANES], lo, hi)
            z = lax.dot_general(q_ps[p], kbd, _NT, preferred_element_type=F32)
            lk = _log_keep(z)
            if edge:
                lk = jnp.where(strict, lk, 0.0)
            res = jnp.dot(_split_hi_lo(lk), w_ref[...], preferred_element_type=F32)
            x = z + lk + res[:, :2 * kb]
            if not first:
                x = x + run_scr[p]
            a = jnp.exp(x)
            if edge:
                a = jnp.where(strict, a, 0.0)
            pv = jnp.dot(a.astype(BF16), vbd, preferred_element_type=F32)
            if first:
                run_scr[p] = res[:, 2 * kb:]
                acc_scr[p] = pv
            else:
                run_scr[p] += res[:, 2 * kb:]
                acc_scr[p] += pv

    for e in range(bpt):
        block((i + 1) * bpt - 1 - e, True, e == 0)

    def body(t, carry):
        block(i * bpt - 1 - t, False, False)
        return carry

    _fori_groups(i, bpt, body, 0)
    for p in range(n_pairs):
        o_ref[0, :, p * LANES:(p + 1) * LANES] = acc_scr[p].astype(o_ref.dtype)


def _sb_prompt(qb, kb16, vb16):
    bsz, seq, width = qb.shape
    assert seq % Q_TILE == 0 and Q_TILE % Q_BLOCK == 0
    wmat = _suffix_matrix(2)
    qblk = pl.BlockSpec((1, Q_TILE, width), lambda b, i: (b, i, 0))
    full = pl.BlockSpec((1, seq, width), lambda b, i: (b, 0, 0))
    return pl.pallas_call(
        _sb_prompt_kernel,
        grid=(bsz, seq // Q_TILE),
        in_specs=[qblk, full, full, _const_spec(wmat.shape)],
        out_specs=qblk,
        out_shape=jax.ShapeDtypeStruct((bsz, seq, width), BF16),
        scratch_shapes=[pltpu.VMEM((N_HEADS_B // 2, Q_TILE, 2 * Q_BLOCK), F32),
                        pltpu.VMEM((N_HEADS_B // 2, Q_TILE, LANES), F32)],
        compiler_params=_cparams(2),
        name="sb_prompt",
    )(qb, kb16, vb16, wmat)


def _page_specs(block_tail, n_pages, reverse, pages=None):
    pages = pages or SMALL_PAGES

    def spec(p):
        def index_map(b, g, pt):
            page = g * pages + p
            if reverse:
                page = n_pages - 1 - page
            return (pt[b, page],) + (0,) * len(block_tail)
        return pl.BlockSpec((1,) + block_tail, index_map)
    return [spec(p) for p in range(pages)]


def _idx_sample_kernel(pt_ref, q_ref, w_ref, *refs):
    k_refs, o_ref = refs[:SMALL_PAGES], refs[SMALL_PAGES]
    q = q_ref[0].astype(BF16)
    w = w_ref[0] * (IDX_DIM ** -0.5)
    k_all = jnp.concatenate([k[0].astype(BF16) for k in k_refs], axis=1)
    r = jnp.dot(q, k_all, preferred_element_type=F32)
    score = jnp.sum(jnp.maximum(r, 0.0) * w, axis=0, keepdims=True)
    for p in range(SMALL_PAGES):
        o_ref[0, p] = score[:, p * LANES:(p + 1) * LANES]


def _idx_sample(page_table, qi3, wi3, cache_kidx):
    bd, n_pages = page_table.shape
    per_b = lambda shape: pl.BlockSpec((1,) + shape, lambda b, g, pt: (b,) + (0,) * len(shape))
    return pl.pallas_call(
        _idx_sample_kernel,
        grid_spec=pltpu.PrefetchScalarGridSpec(
            num_scalar_prefetch=1,
            grid=(bd, n_pages // SMALL_PAGES),
            in_specs=[per_b((N_IDX_HEADS, IDX_DIM)), per_b((N_IDX_HEADS, 1))]
            + _page_specs((IDX_DIM, Q_BLOCK), n_pages, False),
            out_specs=pl.BlockSpec((1, SMALL_PAGES, 1, LANES), lambda b, g, pt: (b, g, 0, 0)),
        ),
        out_shape=jax.ShapeDtypeStruct((bd, n_pages, 1, LANES), F32),
        compiler_params=_cparams(2),
        name="idx_sample",
    )(page_table, qi3, wi3, *([cache_kidx] * SMALL_PAGES))


def _select_sample_kernel(sc_ref, q_ref, w_ref, k_ref, tri_ref, low_ref, madd_o, maddn_o, *, n_top):
    n_seq = sc_ref.shape[0]
    seqs = range(n_seq)

    def total(x):
        return jnp.sum(jnp.sum(x, axis=1, keepdims=True), axis=0, keepdims=True)

    keys, keys_new = [], []
    for s in seqs:
        keys.append(_order_key(sc_ref[s]))
        r_new = jnp.sum(q_ref[s] * k_ref[s], axis=1, keepdims=True)
        s_new = jnp.sum(jnp.maximum(r_new * (IDX_DIM ** -0.5), 0.0) * w_ref[s], axis=0, keepdims=True)
        keys_new.append(_order_key(s_new))

    def count(s, pred):
        return total(jnp.where(pred(keys[s]), 1, 0)) + jnp.where(pred(keys_new[s]), 1, 0)

    thrs = tuple(jnp.where(count(s, lambda k: k >= 0) >= n_top, 0, INT_MIN).astype(I32) for s in seqs)

    def bit_body(t, thrs):
        bit = lax.shift_left(jnp.int32(1), jnp.int32(30) - t)
        cands = [thr | bit for thr in thrs]
        return tuple(jnp.where(count(s, lambda k: k >= cands[s]) >= n_top, cands[s], thrs[s]) for s in seqs)

    thrs = lax.fori_loop(0, 31, bit_body, thrs)
    for s in seqs:
        key, key_new, thr = keys[s], keys_new[s], thrs[s]
        need = (n_top - count(s, lambda k: k > thr)).astype(F32)
        eq = jnp.where(key == thr, 1.0, 0.0).astype(BF16)
        pre = jnp.dot(eq, tri_ref[...], preferred_element_type=F32)
        tot = jnp.broadcast_to(pre[:, LANES - 1:LANES], pre.shape).astype(BF16)
        before = jnp.dot(low_ref[...], tot, preferred_element_type=F32)
        take = (key > thr) | ((key == thr) & (before + pre <= need))
        madd_o[s] = jnp.where(take, 0.0, NEG)
        ties_past = total(jnp.where(key == thr, 1.0, 0.0))
        take_new = (key_new > thr) | ((key_new == thr) & (ties_past + 1.0 <= need))
        maddn_o[s] = jnp.where(take_new, 0.0, NEG)


def _select_sample(scores, qi3, wi3, ki3, n_top):
    bd, n_pages, _ = scores.shape
    n_seq = math.gcd(bd, SELECT_SEQS)
    tri = jnp.asarray((np.arange(LANES)[:, None] <= np.arange(LANES)[None, :]).astype(np.float32), BF16)
    low = jnp.asarray((np.arange(n_pages)[:, None] > np.arange(n_pages)[None, :]).astype(np.float32), BF16)
    per_b = lambda shape: pl.BlockSpec((n_seq,) + shape, lambda b: (b,) + (0,) * len(shape))
    return pl.pallas_call(
        functools.partial(_select_sample_kernel, n_top=n_top),
        grid=(bd // n_seq,),
        in_specs=[per_b((n_pages, LANES)), per_b((N_IDX_HEADS, IDX_DIM)), per_b((N_IDX_HEADS, 1)),
                  per_b((1, IDX_DIM)), _const_spec(tri.shape), _const_spec(low.shape)],
        out_specs=[per_b((n_pages, LANES)), per_b((1, 1))],
        out_shape=[jax.ShapeDtypeStruct((bd, n_pages, LANES), F32), jax.ShapeDtypeStruct((bd, 1, 1), F32)],
        compiler_params=_cparams(1),
        name="select_sample",
    )(scores, qi3, wi3, ki3, tri, low)


def _sample_bias(tabt, dist):
    bucket = _t5_bucket(dist)
    out = jnp.zeros((tabt.shape[0], dist.shape[1]), F32)
    for bk in range(N_BUCKETS):
        out = jnp.where(bucket == bk, tabt[:, bk:bk + 1], out)
    return out


def _dsa_logits_kernel(pt_ref, tabt_ref, qz_ref, madd_ref, *refs, past_len):
    k_refs, o_ref = refs[:SMALL_PAGES], refs[SMALL_PAGES]
    g = pl.program_id(1)
    qz16 = qz_ref[0].astype(BF16)
    k_all = jnp.concatenate([k[0].astype(BF16) for k in k_refs], axis=1)
    lg = jnp.dot(qz16, k_all, preferred_element_type=F32)
    key_pos = g * (SMALL_PAGES * Q_BLOCK) + lax.broadcasted_iota(I32, (1, SMALL_PAGES * Q_BLOCK), 1)
    lg = lg + _sample_bias(tabt_ref[...], past_len - key_pos)
    for p in range(SMALL_PAGES):
        o_ref[0, p] = lg[:, p * LANES:(p + 1) * LANES] + madd_ref[0, p]


def _dsa_logits(page_table, bias_table, qz, madd, cache_kt, past_len):
    bd, n_pages = page_table.shape
    per_b = lambda shape: pl.BlockSpec((1,) + shape, lambda b, g, pt: (b,) + (0,) * len(shape))
    return pl.pallas_call(
        functools.partial(_dsa_logits_kernel, past_len=past_len),
        grid_spec=pltpu.PrefetchScalarGridSpec(
            num_scalar_prefetch=1,
            grid=(bd, n_pages // SMALL_PAGES),
            in_specs=[pl.BlockSpec((N_HEADS_A, N_BUCKETS), lambda b, g, pt: (0, 0)),
                      per_b((N_HEADS_A, LANES)),
                      pl.BlockSpec((1, SMALL_PAGES, 1, LANES), lambda b, g, pt: (b, g, 0, 0))]
            + _page_specs((N_KV_A * HEAD_DIM, Q_BLOCK), n_pages, False),
            out_specs=pl.BlockSpec((1, SMALL_PAGES, N_HEADS_A, LANES), lambda b, g, pt: (b, g, 0, 0)),
        ),
        out_shape=jax.ShapeDtypeStruct((bd, n_pages, N_HEADS_A, LANES), F32),
        compiler_params=_cparams(2),
        name="dsa_logits",
    )(page_table, bias_table.T, qz, madd, *([cache_kt] * SMALL_PAGES))


def _dsa_pv_kernel(pt_ref, tabt_ref, lg_ref, qz_ref, maddn_ref, kn_ref, vnc_ref, *refs):
    v_refs = refs[:SMALL_PAGES]
    o_ref, m_scr, lsum_scr, acc_scr = refs[SMALL_PAGES:]
    g = pl.program_id(1)

    def new_logit():
        lg = jnp.sum(qz_ref[0] * kn_ref[0], axis=1, keepdims=True)
        return lg + _sample_bias(tabt_ref[...], jnp.zeros((1, 1), I32)) + maddn_ref[0]

    @pl.when(g == 0)
    def _():
        m_past = jnp.max(jnp.max(lg_ref[0], axis=0), axis=1, keepdims=True)
        m_scr[...] = jnp.maximum(m_past, new_logit())
        lsum_scr[...] = jnp.zeros(lsum_scr.shape, F32)
        acc_scr[...] = jnp.zeros(acc_scr.shape, F32)

    m = m_scr[...]
    prs = [jnp.exp(lg_ref[0, g * SMALL_PAGES + p] - m) for p in range(SMALL_PAGES)]
    lsum_scr[...] += functools.reduce(lambda a, b: a + b, prs)
    v_all = jnp.concatenate([v[0].astype(BF16) for v in v_refs], axis=1)
    acc_scr[...] += lax.dot_general(v_all, jnp.concatenate(prs, axis=1).astype(BF16), _NT,
                                    preferred_element_type=F32)

    @pl.when(g == pl.num_programs(1) - 1)
    def _():
        p_new = jnp.exp(new_logit() - m)
        denom = jnp.sum(lsum_scr[...], axis=1, keepdims=True) + p_new
        out = (acc_scr[...] + vnc_ref[0] * _col_to_row(p_new)) / _col_to_row(denom)
        r = lax.broadcasted_iota(I32, out.shape, 0)
        h = lax.broadcasted_iota(I32, out.shape, 1)
        out = jnp.where(_div_pow2(r, HEAD_DIM) == _div_pow2(h, GROUP_A), out, 0.0)
        o_ref[0] = out[:HEAD_DIM] + out[HEAD_DIM:]


def _dsa_pv(page_table, bias_table, logits, qz, maddn, k_new, v_new_col, cache_vt):
    bd, n_pages = page_table.shape
    per_b = lambda shape: pl.BlockSpec((1,) + shape, lambda b, g, pt: (b,) + (0,) * len(shape))
    kv = N_KV_A * HEAD_DIM
    return pl.pallas_call(
        _dsa_pv_kernel,
        grid_spec=pltpu.PrefetchScalarGridSpec(
            num_scalar_prefetch=1,
            grid=(bd, n_pages // SMALL_PAGES),
            in_specs=[pl.BlockSpec((N_HEADS_A, N_BUCKETS), lambda b, g, pt: (0, 0)),
                      per_b((n_pages, N_HEADS_A, LANES)), per_b((N_HEADS_A, kv)), per_b((1, 1)),
                      per_b((1, kv)), per_b((kv, 1))] + _page_specs((kv, Q_BLOCK), n_pages, False),
            out_specs=per_b((HEAD_DIM, N_HEADS_A)),
            scratch_shapes=[pltpu.VMEM((N_HEADS_A, 1), F32), pltpu.VMEM((N_HEADS_A, LANES), F32),
                            pltpu.VMEM((kv, N_HEADS_A), F32)],
        ),
        out_shape=jax.ShapeDtypeStruct((bd, HEAD_DIM, N_HEADS_A), F32),
        compiler_params=_cparams(2),
        name="dsa_pv",
    )(page_table, bias_table.T, logits, qz, maddn, k_new, v_new_col, *([cache_vt] * SMALL_PAGES))


def _sb_sample_kernel(pt_ref, qbd_ref, kn_ref, vnc_ref, w_ref, *refs, past_len):
    k_refs = refs[:SB_PAGES]
    v_refs = refs[SB_PAGES:2 * SB_PAGES]
    o_ref, run_scr, acc_scr = refs[2 * SB_PAGES:]
    g = pl.program_id(1)
    n_steps = pl.num_programs(1)
    n_pages = n_steps * SB_PAGES
    qbd = qbd_ref[0]
    qbd16 = qbd.astype(BF16)
    lane = lax.broadcasted_iota(I32, (1, LANES), 1)
    q_pos = past_len

    @pl.when(g == 0)
    def _():
        pos = past_len + lax.broadcasted_iota(I32, (N_HEADS_B, 1), 1)
        keep = pos < q_pos
        z = jnp.sum(qbd * kn_ref[0], axis=1, keepdims=True)
        lk = jnp.where(keep, _log_keep(z), 0.0)
        a = jnp.where(keep, jnp.exp(z + lk), 0.0)
        run_scr[...] = jnp.broadcast_to(lk, run_scr.shape)
        acc_scr[...] = vnc_ref[0] * _col_to_row(a)

    pages = range(SB_PAGES)
    nh = N_HEADS_B
    k_all = jnp.concatenate([k_refs[p][0].astype(BF16) for p in pages], axis=1)
    z_all = jnp.dot(qbd16, k_all, preferred_element_type=F32)
    z = jnp.concatenate([z_all[:, p * LANES:(p + 1) * LANES] for p in pages], axis=0)
    visit = g * SB_PAGES + _div_pow2(lax.broadcasted_iota(I32, z.shape, 0), nh)
    key_pos = (n_pages - 1 - visit) * Q_BLOCK + lax.broadcasted_iota(I32, z.shape, 1)
    keep = key_pos < q_pos
    lk = jnp.where(keep, _log_keep(z), 0.0)
    res = jnp.dot(_split_hi_lo(lk), w_ref[...], preferred_element_type=F32)
    run = run_scr[...]
    runs = []
    for p in pages:
        runs.append(run)
        run = run + res[p * nh:(p + 1) * nh, LANES:]
    run_scr[...] = run
    a = jnp.where(keep, jnp.exp(z + lk + res[:, :LANES] + jnp.concatenate(runs, axis=0)), 0.0)
    a_all = jnp.concatenate([a[p * nh:(p + 1) * nh] for p in pages], axis=1).astype(BF16)
    v_all = jnp.concatenate([v_refs[p][0].astype(BF16) for p in pages], axis=1)
    acc_scr[...] += lax.dot_general(v_all, a_all, _NT, preferred_element_type=F32)

    @pl.when(g == n_steps - 1)
    def _():
        r = lax.broadcasted_iota(I32, acc_scr.shape, 0)
        h = lax.broadcasted_iota(I32, acc_scr.shape, 1)
        o_ref[0] = jnp.sum(jnp.where(_div_pow2(r, HEAD_DIM) == h, acc_scr[...], 0.0), axis=1, keepdims=True)


def _sb_sample(page_table, qbd, k_new, v_new_col, cache_kt, cache_vt, past_len):
    bd, n_pages = page_table.shape
    width = cache_kt.shape[1]
    wmat = _suffix_matrix(1)
    per_b = lambda shape: pl.BlockSpec((1,) + shape, lambda b, g, pt: (b,) + (0,) * len(shape))
    kv_specs = _page_specs((width, Q_BLOCK), n_pages, True, SB_PAGES)
    return pl.pallas_call(
        functools.partial(_sb_sample_kernel, past_len=past_len),
        grid_spec=pltpu.PrefetchScalarGridSpec(
            num_scalar_prefetch=1,
            grid=(bd, n_pages // SB_PAGES),
            in_specs=[per_b((N_HEADS_B, width)), per_b((1, width)), per_b((width, 1)),
                      pl.BlockSpec(wmat.shape, lambda b, g, pt: (0, 0))] + kv_specs + kv_specs,
            out_specs=per_b((width, 1)),
            scratch_shapes=[pltpu.VMEM((N_HEADS_B, LANES), F32), pltpu.VMEM((width, N_HEADS_B), F32)],
        ),
        out_shape=jax.ShapeDtypeStruct((bd, width, 1), F32),
        compiler_params=_cparams(2),
        name="sb_sample",
    )(page_table, qbd, k_new, v_new_col, wmat, *([cache_kt] * SB_PAGES), *([cache_vt] * SB_PAGES))


def _route(logits, axis):
    slot = lax.broadcasted_iota(I32, logits.shape, axis)
    big = jnp.int32(ROUTER_LANES)

    def softmax_over(mask):
        m = jnp.max(jnp.where(mask, logits, NEG), axis=axis, keepdims=True)
        e = jnp.where(mask, jnp.exp(logits - m), 0.0)
        return e / jnp.sum(e, axis=axis, keepdims=True)

    def top1(prob, mask):
        best = jnp.max(jnp.where(mask, prob, -1.0), axis=axis, keepdims=True)
        idx = jnp.min(jnp.where(mask & (prob == best), slot, big), axis=axis, keepdims=True)
        return best, idx

    is_group = slot < N_GROUPS
    g_w, g_idx = top1(softmax_over(is_group), is_group)
    e_slot = slot - N_GROUPS
    in_group = (e_slot >= 0) & (e_slot < N_EXPERTS) & (_div_pow2(e_slot, EXPERTS_PER_GROUP) == g_idx)
    p_e = softmax_over(in_group)
    p1, i1 = top1(p_e, in_group)
    p2, i2 = top1(p_e, in_group & (slot != i1))
    denom = p1 + p2
    return slot, i1, p1 / denom * g_w, i2, p2 / denom * g_w


def _merge_kernel(x_ref, oa_ref, ob_ref, ga_ref, gb_ref, wpa, wpb, wout, ln2_ref, wr, br,
                  h_o, hn_o, route_o, *, grouped):
    mixed = (ga_ref[...] * jnp.dot(oa_ref[...], wpa[...], preferred_element_type=F32)
             + gb_ref[...] * jnp.dot(ob_ref[...], wpb[...], preferred_element_type=F32))
    h = x_ref[...] + jnp.dot(mixed.astype(BF16), wout[...], preferred_element_type=F32)
    h_o[...] = h
    ms = jnp.mean(h * h, axis=-1, keepdims=True)
    hn = (h * lax.rsqrt(ms + EPS) * ln2_ref[...]).astype(BF16)

    if grouped:
        half = hn.shape[1] // 2
        hi = lax.bitcast_convert_type(hn[:, :half].astype(F32), jnp.uint32)
        lo = lax.bitcast_convert_type(hn[:, half:].astype(F32), jnp.uint32)
        hn_o[...] = hi | (lo >> 16)
        logits_t = lax.dot_general(wr[...], hn, _NT, preferred_element_type=F32) + br[...]
        _, i1, w1, i2, w2 = _route(logits_t, 0)
        rows = [(i1 - N_GROUPS).astype(F32), (i2 - N_GROUPS).astype(F32), w1, w2]
        route_o[...] = jnp.concatenate(rows + [jnp.zeros_like(w1)] * (route_o.shape[0] - len(rows)), axis=0)
    else:
        hn_o[...] = hn
        logits = jnp.dot(hn, wr[...], preferred_element_type=F32) + br[...]
        slot, i1, w1, i2, w2 = _route(logits, 1)
        route_o[...] = jnp.where(slot == i1, w1, 0.0) + jnp.where(slot == i2, w2, 0.0)


def _merge(x2d, o_a, o_b, g_a, g_b, wpa, wpb, wout, ln2, wr, br, tm, grouped):
    n, d = x2d.shape
    row = lambda w: pl.BlockSpec((tm, w), lambda i: (i, 0))
    if grouped:
        consts = (wpa, wpb, wout, ln2, wr.T, br.T)
        hn_spec, hn_shape = row(d // 2), jax.ShapeDtypeStruct((n, d // 2), jnp.uint32)
        rt_spec = pl.BlockSpec((ROUTE_ROWS, tm), lambda i: (0, i))
        rt_shape = jax.ShapeDtypeStruct((ROUTE_ROWS, n), F32)
    else:
        consts = (wpa, wpb, wout, ln2, wr, br)
        hn_spec, hn_shape = row(d), jax.ShapeDtypeStruct((n, d), BF16)
        rt_spec, rt_shape = row(ROUTER_LANES), jax.ShapeDtypeStruct((n, ROUTER_LANES), F32)
    return pl.pallas_call(
        functools.partial(_merge_kernel, grouped=grouped),
        grid=(n // tm,),
        in_specs=[row(d), row(WIDTH_A), row(WIDTH_B), row(d), row(d)] + [_const_spec(c.shape) for c in consts],
        out_specs=[row(d), hn_spec, rt_spec],
        out_shape=[jax.ShapeDtypeStruct((n, d), F32), hn_shape, rt_shape],
        compiler_params=_cparams(1),
        name="merge",
    )(x2d, o_a, o_b, g_a, g_b, *consts)


def _moe_kernel(hn_ref, comb_ref, h_ref, wgu_ref, wd_ref, lnf_ref, y_ref, acc_scr, *, final_norm):
    e = pl.program_id(1)

    @pl.when(e == 0)
    def _():
        acc_scr[...] = jnp.zeros(acc_scr.shape, F32)

    d_expert = wd_ref.shape[1]
    gu = jnp.dot(hn_ref[...], wgu_ref[0], preferred_element_type=F32)
    gate, up = gu[:, :d_expert], gu[:, d_expert:]
    comb = comb_ref[...]
    lane = lax.broadcasted_iota(I32, comb.shape, 1)
    c_e = jnp.sum(jnp.where(lane == e + N_GROUPS, comb, 0.0), axis=1, keepdims=True)
    hid = gate * _sigmoid(gate) * up * c_e
    acc_scr[...] += jnp.dot(hid.astype(BF16), wd_ref[0], preferred_element_type=F32)

    @pl.when(e == pl.num_programs(1) - 1)
    def _():
        xo = h_ref[...] + acc_scr[...]
        if final_norm:
            ms = jnp.mean(xo * xo, axis=-1, keepdims=True)
            xo = xo * lax.rsqrt(ms + EPS) * lnf_ref[...]
        y_ref[...] = xo


def _moe(hn, comb, h, wgu, wd, lnf, tm, final_norm):
    n, d = h.shape
    n_exp, _, two_f = wgu.shape
    row = lambda w: pl.BlockSpec((tm, w), lambda i, e: (i, 0))
    return pl.pallas_call(
        functools.partial(_moe_kernel, final_norm=final_norm),
        grid=(n // tm, n_exp),
        in_specs=[row(d), row(ROUTER_LANES), row(d),
                  pl.BlockSpec((1, d, two_f), lambda i, e: (e, 0, 0)),
                  pl.BlockSpec((1, two_f // 2, d), lambda i, e: (e, 0, 0)),
                  pl.BlockSpec((1, d), lambda i, e: (0, 0))],
        out_specs=row(d),
        out_shape=jax.ShapeDtypeStruct((n, d), F32),
        scratch_shapes=[pltpu.VMEM((tm, d), F32)],
        compiler_params=_cparams(2),
        name="moe",
    )(hn, comb, h, wgu, wd, lnf)


def _moe_plan_kernel(rt_ref, before_ref, lower_ref, pos_o, tt_o):
    ch = rt_ref.shape[1]
    e1 = rt_ref[0:1, :].astype(I32)
    e2 = rt_ref[1:2, :].astype(I32)
    sub = lax.broadcasted_iota(I32, (ROUTER_LANES, ch), 0)
    a1, a2 = sub == e1, sub == e2
    chosen = (jnp.where(a1, 1.0, 0.0) + jnp.where(a2, 1.0, 0.0)).astype(BF16)
    before = jnp.dot(chosen, before_ref[...], preferred_element_type=F32)
    cnt = before[:, ch - 1:ch] + chosen[:, ch - 1:ch].astype(F32)
    ntile = jnp.floor((cnt + (MOE_ROWS - 1)) * (1.0 / MOE_ROWS))
    base_t = jnp.dot(lower_ref[...], jnp.broadcast_to(ntile, (ROUTER_LANES, LANES)).astype(BF16),
                     preferred_element_type=F32)[:, 0:1]
    slot = before + base_t * MOE_ROWS
    pos = [jnp.sum(jnp.where(a, slot, 0.0), axis=0, keepdims=True) for a in (a1, a2)]
    pos_o[0] = jnp.concatenate(pos + [jnp.zeros_like(pos[0])] * (pos_o.shape[1] - 2), axis=0).astype(I32)

    tile = lax.broadcasted_iota(I32, (ROUTER_LANES, LANES), 1).astype(F32)
    exp = lax.broadcasted_iota(I32, (ROUTER_LANES, LANES), 0)
    is_exp = exp < N_EXPERTS
    te = jnp.sum(jnp.where(is_exp & (base_t + ntile <= tile), 1, 0), axis=0, keepdims=True)
    total = jnp.sum(jnp.where(is_exp, ntile, 0.0), axis=0, keepdims=True).astype(I32) + jnp.zeros_like(te)
    rows = [jnp.minimum(te, N_EXPERTS - 1), total]
    tt_o[0] = jnp.concatenate(rows + [jnp.zeros_like(te)] * (tt_o.shape[1] - len(rows)), axis=0)


def _moe_routed_kernel(tt_ref, pos_ref, rt_ref, hnp_ref, h_ref, *refs, max_tiles, final_norm):
    w_refs = refs[:2 * MOE_TILES_PER_STEP]
    lnf_ref, y_ref, rows_scr = refs[2 * MOE_TILES_PER_STEP:]
    c = pl.program_id(0)
    i = pl.program_id(1)
    ch, half = hnp_ref.shape
    n_tiles = tt_ref[(c * 2 + 1) * max_tiles]

    @pl.when(i == 0)
    def _():
        rows_scr[:, :half] = jnp.zeros((rows_scr.shape[0], half), F32)

        def place(t, carry):
            row = lax.bitcast_convert_type(hnp_ref[pl.ds(t, 1), :], F32)
            for k in range(TOP_E):
                rows_scr[pl.ds(pos_ref[0, k, t], 1), :half] = row
            return carry

        lax.fori_loop(0, ch, place, 0, unroll=MOE_UNROLL)

    def expert_tile(ti, wgu_ref, wd_ref):
        tile = pl.ds(pl.multiple_of(ti * MOE_ROWS, MOE_ROWS), MOE_ROWS)
        packed = lax.bitcast_convert_type(rows_scr[tile, :half], jnp.uint32)
        first = lax.bitcast_convert_type(packed & jnp.uint32(0xFFFF0000), F32).astype(BF16)
        second = lax.bitcast_convert_type(packed << 16, F32).astype(BF16)
        x = jnp.concatenate([first, second], axis=1)
        d_expert = wd_ref.shape[1]
        gu = jnp.dot(x, wgu_ref[0], preferred_element_type=F32)
        gate, up = gu[:, :d_expert], gu[:, d_expert:]
        hid = gate * _sigmoid(gate) * up
        rows_scr[tile, :] = jnp.dot(hid.astype(BF16), wd_ref[0], preferred_element_type=F32)

    for u in range(MOE_TILES_PER_STEP):
        ti = i * MOE_TILES_PER_STEP + u
        pl.when(ti < n_tiles)(functools.partial(expert_tile, ti, w_refs[2 * u], w_refs[2 * u + 1]))

    @pl.when(i == pl.num_programs(1) - 1)
    def _():
        def combine(t, carry):
            acc = h_ref[pl.ds(t, 1), :]
            for k in range(TOP_E):
                acc = acc + rt_ref[TOP_E + k, t] * rows_scr[pl.ds(pos_ref[0, k, t], 1), :]
            y_ref[pl.ds(t, 1), :] = acc
            return carry

        lax.fori_loop(0, ch, combine, 0, unroll=MOE_UNROLL)
        if final_norm:
            xo = y_ref[...]
            ms = jnp.mean(xo * xo, axis=-1, keepdims=True)
            y_ref[...] = xo * lax.rsqrt(ms + EPS) * lnf_ref[...]


def _moe_routed(hnp, rt, h, wgu, wd, lnf, final_norm):
    n, d = h.shape
    ch = MOE_CHUNK
    assert n % ch == 0
    n_chunks = n // ch
    max_tiles = TOP_E * ch // MOE_ROWS + N_EXPERTS
    assert max_tiles <= LANES and max_tiles % MOE_TILES_PER_STEP == 0
    before = jnp.asarray((np.arange(ch)[:, None] < np.arange(ch)[None, :]).astype(np.float32), BF16)
    lower = jnp.asarray((np.arange(ROUTER_LANES)[:, None] > np.arange(ROUTER_LANES)[None, :]).astype(np.float32),
                        BF16)
    pos, tt = pl.pallas_call(
        _moe_plan_kernel,
        grid=(n_chunks,),
        in_specs=[pl.BlockSpec((ROUTE_ROWS, ch), lambda c: (0, c)), _const_spec(before.shape),
                  _const_spec(lower.shape)],
        out_specs=[pl.BlockSpec((1, ROUTE_ROWS, ch), lambda c: (c, 0, 0)),
                   pl.BlockSpec((1, ROUTE_ROWS, LANES), lambda c: (c, 0, 0))],
        out_shape=[jax.ShapeDtypeStruct((n_chunks, ROUTE_ROWS, ch), I32),
                   jax.ShapeDtypeStruct((n_chunks, ROUTE_ROWS, LANES), I32)],
        compiler_params=_cparams(1),
        name="moe_plan",
    )(rt, before, lower)
    tt_flat = tt[:, :2, :max_tiles].reshape(-1)
    n_exp, _, two_f = wgu.shape
    tps = MOE_TILES_PER_STEP
    w_specs, w_args = [], []
    for u in range(tps):
        expert_of = lambda c, i, tt, u=u: (tt[c * 2 * max_tiles + i * tps + u], 0, 0)
        w_specs += [pl.BlockSpec((1, d, two_f), expert_of), pl.BlockSpec((1, two_f // 2, d), expert_of)]
        w_args += [wgu, wd]
    return pl.pallas_call(
        functools.partial(_moe_routed_kernel, max_tiles=max_tiles, final_norm=final_norm),
        grid_spec=pltpu.PrefetchScalarGridSpec(
            num_scalar_prefetch=1,
            grid=(n_chunks, max_tiles // tps),
            in_specs=[pl.BlockSpec((1, ROUTE_ROWS, ch), lambda c, i, tt: (c, 0, 0), memory_space=pltpu.SMEM),
                      pl.BlockSpec((ROUTE_ROWS, ch), lambda c, i, tt: (0, c), memory_space=pltpu.SMEM),
                      pl.BlockSpec((ch, d // 2), lambda c, i, tt: (c, 0)),
                      pl.BlockSpec((ch, d), lambda c, i, tt: (c, 0))] + w_specs
            + [pl.BlockSpec((1, d), lambda c, i, tt: (0, 0))],
            out_specs=pl.BlockSpec((ch, d), lambda c, i, tt: (c, 0)),
            scratch_shapes=[pltpu.VMEM((max_tiles * MOE_ROWS, d), F32)],
        ),
        out_shape=jax.ShapeDtypeStruct((n, d), F32),
        compiler_params=_cparams(2),
        name="moe_routed",
    )(tt_flat, pos, rt, hnp, h, *w_args, lnf)


def _tile(n, pref):
    return pref if n % pref == 0 else n


def kernel(x_prompt, x_sample, cache_k_a, cache_v_a, cache_kidx_a, cache_k_b, cache_v_b, page_table,
           rel_bias_table, ln1_g, w_in, w_proj_a, w_proj_b, w_out, ln2_g, w_router_group, b_router_group,
           w_router_expert, b_router_expert, w_gate, w_up, w_down, ln_f_g):
    bsz, seq, d = x_prompt.shape
    bd, t_new, _ = x_sample.shape
    assert t_new == 1, "the sample group decodes one token per sequence"
    depth = w_in.shape[0]
    pool = cache_k_a.shape[1]
    n_pages = page_table.shape[1]
    past_len = n_pages * Q_BLOCK
    assert n_pages % SMALL_PAGES == 0 and n_pages % SB_PAGES == 0
    n_p, n_s = bsz * seq, bd * t_new
    d_expert = w_gate.shape[-1]

    xp, xs = x_prompt.reshape(n_p, d), x_sample.reshape(n_s, d)
    new_p, new_s = [], []
    for l in range(depth):
        w_pieces = _split_w_in(w_in[l])
        wpa, wpb, wout = (w.astype(BF16) for w in (w_proj_a[l], w_proj_b[l], w_out[l]))
        ln2 = ln2_g[l].reshape(1, d)
        wr = jnp.concatenate([w_router_group[l], jnp.moveaxis(w_router_expert[l], 0, 1).reshape(d, N_EXPERTS),
                              jnp.zeros((d, ROUTER_LANES - N_GROUPS - N_EXPERTS), F32)], axis=1).astype(BF16)
        br = jnp.concatenate([b_router_group[l], b_router_expert[l].reshape(N_EXPERTS),
                              jnp.zeros((ROUTER_LANES - N_GROUPS - N_EXPERTS,), F32)]).reshape(1, ROUTER_LANES)
        wgu = jnp.concatenate([w_gate[l], w_up[l]], axis=-1).reshape(N_EXPERTS, d, 2 * d_expert).astype(BF16)
        wd = w_down[l].reshape(N_EXPERTS, d_expert, d).astype(BF16)
        lnf = ln_f_g.reshape(1, d)

        def ffn(x2d, o_a, o_b, g_a, g_b, tm_merge, tm_moe):
            final_norm = l == depth - 1
            routed = x2d.shape[0] % MOE_CHUNK == 0
            h, hn, route = _merge(x2d, o_a, o_b, g_a, g_b, wpa, wpb, wout, ln2, wr, br, tm_merge, grouped=routed)
            if routed:
                return _moe_routed(hn, route, h, wgu, wd, lnf, final_norm)
            return _moe(hn, route, h, wgu, wd, lnf, tm_moe, final_norm)

        (qa, ka, va, kad, vad, qi, ki, kid, wi, qb, kb, vb, kb16, vb16, ga, gb) = _project(
            xp, ln1_g[l], w_pieces, _tile(seq, 512), seq_len=seq)
        r3 = lambda a: a.reshape(bsz, seq, a.shape[-1])
        n_top = min(TOPK_MAX, seq // 4)
        wit = jnp.swapaxes(r3(wi), 1, 2)
        vadt = jnp.swapaxes(vad.reshape(bsz, seq // Q_BLOCK, Q_BLOCK, 2 * LANES), 2, 3)
        o_at = _dsa_prompt(rel_bias_table, r3(qi), wit, r3(kid), r3(qa), r3(kad), vadt, n_top)
        o_a = jnp.swapaxes(o_at, 1, 2)
        o_b = _sb_prompt(r3(qb), r3(kb16), r3(vb16))
        xp = ffn(xp, o_a.reshape(n_p, WIDTH_A), o_b.reshape(n_p, WIDTH_B), ga, gb, _tile(n_p, 512), _tile(n_p, 1024))
        new_p.append((ka, va, ki, kb, vb))

        (qa, ka, va, _, _, qi, ki, _, wi, qb, kb, vb, _, _, ga, gb) = _project(xs, ln1_g[l], w_pieces, n_s)
        n_top = min(TOPK_MAX, (past_len + t_new) // 4)
        qi3 = qi.astype(F32).reshape(bd, N_IDX_HEADS, IDX_DIM)
        wi3 = wi.reshape(bd, N_IDX_HEADS, 1)
        paged_t = lambda c: jnp.moveaxis(c, 1, -1).reshape(pool, -1, Q_BLOCK)
        scores = _idx_sample(page_table, qi3, wi3, paged_t(cache_kidx_a[l]))
        madd, maddn = _select_sample(scores.reshape(bd, n_pages, LANES), qi3, wi3, ki.reshape(bd, 1, IDX_DIM), n_top)
        qa4 = qa.astype(F32).reshape(bd, N_HEADS_A, 1, HEAD_DIM) * (HEAD_DIM ** -0.5)
        kv_of_head = (np.arange(N_HEADS_A)[:, None] // GROUP_A) == np.arange(N_KV_A)[None, :]
        qz = jnp.where(kv_of_head[None, :, :, None], qa4, 0.0).reshape(bd, N_HEADS_A, N_KV_A * HEAD_DIM)
        logits = _dsa_logits(page_table, rel_bias_table, qz, madd.reshape(bd, n_pages, 1, LANES),
                             paged_t(cache_k_a[l]), past_len)
        o_a = _dsa_pv(page_table, rel_bias_table, logits, qz, maddn, ka.reshape(bd, 1, -1), va.reshape(bd, -1, 1),
                      paged_t(cache_v_a[l]))
        o_a = jnp.swapaxes(o_a, 1, 2)
        qb4 = qb.astype(F32).reshape(bd, N_HEADS_B, 1, HEAD_DIM) * (HEAD_DIM ** -0.5)
        own = np.eye(N_HEADS_B, dtype=bool)
        qbd = jnp.where(own[None, :, :, None], qb4, 0.0).reshape(bd, N_HEADS_B, WIDTH_B)
        o_b = _sb_sample(page_table, qbd, kb.reshape(bd, 1, -1), vb.reshape(bd, -1, 1),
                         paged_t(cache_k_b[l]), paged_t(cache_v_b[l]), past_len)
        xs = ffn(xs, o_a.reshape(n_s, WIDTH_A).astype(BF16), o_b.reshape(n_s, WIDTH_B).astype(BF16), ga, gb, n_s, n_s)
        new_s.append((ka, va, ki, kb, vb))

    outs = [xp.reshape(bsz, seq, d), xs.reshape(bd, t_new, d)]
    tails = ((N_KV_A, HEAD_DIM), (N_KV_A, HEAD_DIM), (IDX_DIM,), (N_HEADS_B, HEAD_DIM), (N_HEADS_B, HEAD_DIM))
    for idx, tail in enumerate(tails):
        outs.append(jnp.stack([jnp.moveaxis(r[idx].reshape((bsz,) + tail + (seq,)), -1, 1) for r in new_p]))
    for idx, tail in enumerate(tails):
        outs.append(jnp.stack([r[idx].reshape((bd, t_new) + tail) for r in new_s]))
    return tuple(outs)
```

```python
import functools
import math

import jax
import jax.numpy as jnp
import numpy as np
from jax import lax
from jax.experimental import pallas as pl
from jax.experimental.pallas import tpu as pltpu

F32 = jnp.float32
BF16 = jnp.bfloat16
I32 = jnp.int32

HEAD_DIM = 64
N_HEADS_A = 8
N_KV_A = 2
GROUP_A = N_HEADS_A // N_KV_A
N_IDX_HEADS = 8
IDX_DIM = 64
N_HEADS_B = 8
WIDTH_A = N_HEADS_A * HEAD_DIM
WIDTH_B = N_HEADS_B * HEAD_DIM
TOPK_MAX = 256
N_BUCKETS = 32
MAX_DISTANCE = 128
N_GROUPS = 4
EXPERTS_PER_GROUP = 8
N_EXPERTS = N_GROUPS * EXPERTS_PER_GROUP
TOP_E = 2
Q_BLOCK = 128
EPS = 1e-6

Q_TILE = 256
SB_TILE = 512

LANES = 128
VMEM_LIMIT_BYTES = 56 * 1024 * 1024

SMALL_PAGES = 32
SB_PAGES = 16
SELECT_SEQS = 8
MOE_CHUNK = 1024
MOE_ROWS = 128
MOE_TILES_PER_STEP = 3
MOE_UNROLL = 8
ROUTE_ROWS = 8
NEG = -1e30
INT_MIN = -(2 ** 31)
INT16_MIN = -(2 ** 15)
I16 = jnp.int16
ROUTER_LANES = LANES

_NT = (((1,), (1,)), ((), ()))


def _cparams(n_axes):
    return pltpu.CompilerParams(dimension_semantics=("arbitrary",) * n_axes,
                                vmem_limit_bytes=VMEM_LIMIT_BYTES)


def _const_spec(shape):
    nd = len(shape)
    return pl.BlockSpec(shape, lambda *_: (0,) * nd)


def _div_pow2(x, n):
    assert n & (n - 1) == 0
    return x >> (n.bit_length() - 1)


def _sigmoid(x):
    return 1.0 / (1.0 + jnp.exp(-x))


def _half_masks(dtype):
    lane = lax.broadcasted_iota(I32, (LANES, LANES), 1)
    lo = jnp.where(lane < HEAD_DIM, 1.0, 0.0).astype(dtype)
    hi = jnp.where(lane < HEAD_DIM, 0.0, 1.0).astype(dtype)
    return lo, hi


def _pair_block_diag(blk, lo, hi):
    return jnp.concatenate([blk * lo, blk * hi], axis=0)


def _row_half_masks(dtype):
    r = lax.broadcasted_iota(I32, (LANES, LANES), 0)
    top = jnp.where(r < HEAD_DIM, 1.0, 0.0).astype(dtype)
    bottom = jnp.where(r < HEAD_DIM, 0.0, 1.0).astype(dtype)
    return top, bottom


def _pair_block_diag_t(blk_t, top, bottom):
    return jnp.concatenate([blk_t * top, blk_t * bottom], axis=1)


def _fori_groups(n_groups, group, body, init):
    def trip(o, carry):
        for u in range(group):
            carry = body(o * group + u, carry)
        return carry
    return lax.fori_loop(0, n_groups, trip, init)


def _col_to_row(x):
    n = x.shape[0]
    eye = lax.broadcasted_iota(I32, (n, n), 0) == lax.broadcasted_iota(I32, (n, n), 1)
    return jnp.sum(jnp.where(eye, x, 0.0), axis=0, keepdims=True)


def _order_key(score):
    score = jnp.where(score == 0.0, 0.0, score)
    b = lax.bitcast_convert_type(score, I32)
    return b ^ ((b >> 31) & jnp.int32(0x7FFFFFFF))


def _t5_bucket(dist):
    n = jnp.maximum(dist, 0)
    max_exact = N_BUCKETS // 2
    nf = jnp.maximum(n, max_exact).astype(F32)
    large = max_exact + (jnp.log(nf / max_exact) / math.log(MAX_DISTANCE / max_exact)
                         * (N_BUCKETS - max_exact)).astype(I32)
    return jnp.where(n < max_exact, n, jnp.minimum(large, N_BUCKETS - 1))


def _proj_kernel(x_ref, g_ref, wqa, wkv, wkvd, wqi, wkiw, wqb, wkb, wvb, wga, wgb,
                 qa_o, ka_o, va_o, kad_o, vad_o, qi_o, ki_o, kid_o, wi_o,
                 qb_o, kb_o, vb_o, kb16_o, vb16_o, ga_o, gb_o, *, feature_major):
    x = x_ref[...]
    ms = jnp.mean(x * x, axis=-1, keepdims=True)
    xn = (x * lax.rsqrt(ms + EPS) * g_ref[...]).astype(BF16)

    def mm(w):
        return jnp.dot(xn, w[...], preferred_element_type=F32)

    def put_new_rows(o_ref, val):
        if feature_major:
            o_ref[0] = val.T
        else:
            o_ref[...] = val

    qa_o[...] = mm(wqa).astype(BF16)
    kv = mm(wkv)
    put_new_rows(ka_o, kv[:, :N_KV_A * HEAD_DIM])
    put_new_rows(va_o, kv[:, N_KV_A * HEAD_DIM:])
    kvd = mm(wkvd).astype(BF16)
    kad_o[...] = kvd[:, :2 * LANES]
    vad_o[...] = kvd[:, 2 * LANES:]
    qi_o[...] = mm(wqi).astype(BF16)
    kiw = mm(wkiw)
    put_new_rows(ki_o, kiw[:, :IDX_DIM])
    kid_o[...] = kiw[:, :LANES].astype(BF16)
    wi_o[...] = kiw[:, LANES:LANES + N_IDX_HEADS] * (N_IDX_HEADS ** -0.5)
    qb_o[...] = mm(wqb).astype(BF16)
    kb = mm(wkb)
    put_new_rows(kb_o, kb)
    kb16_o[...] = kb.astype(BF16)
    vb = mm(wvb)
    put_new_rows(vb_o, vb)
    vb16_o[...] = vb.astype(BF16)
    ga_o[...] = _sigmoid(mm(wga))
    gb_o[...] = _sigmoid(mm(wgb))


def _split_w_in(w_in):
    d_model = w_in.shape[0]
    widths = (WIDTH_A, N_KV_A * HEAD_DIM, N_KV_A * HEAD_DIM, N_IDX_HEADS * IDX_DIM, N_IDX_HEADS, IDX_DIM,
              WIDTH_B, WIDTH_B, WIDTH_B, d_model, d_model)
    points = [int(p) for p in np.cumsum(widths)[:-1]]
    qa, ka, va, qi, wi, ki, qb, kb, vb, ga, gb = jnp.split(w_in, points, axis=1)
    dup = lambda a: jnp.concatenate([a[:, :HEAD_DIM], a[:, :HEAD_DIM], a[:, HEAD_DIM:], a[:, HEAD_DIM:]], axis=1)
    wkv = jnp.concatenate([ka, va], axis=1)
    wkvd = jnp.concatenate([dup(ka), dup(va)], axis=1)
    wkiw = jnp.concatenate([ki, ki, wi, jnp.zeros((d_model, LANES - N_IDX_HEADS), w_in.dtype)], axis=1)
    return tuple(a.astype(BF16) for a in (qa, wkv, wkvd, qi, wkiw, qb, kb, vb, ga, gb))


_NEW_ROW_OUTPUTS = (1, 2, 6, 10, 11)


def _project(x2d, ln_g, w_pieces, tm, seq_len=None):
    n, d = x2d.shape
    assert n % tm == 0
    out_defs = [
        (WIDTH_A, BF16), (N_KV_A * HEAD_DIM, F32), (N_KV_A * HEAD_DIM, F32), (2 * LANES, BF16), (2 * LANES, BF16),
        (N_IDX_HEADS * IDX_DIM, BF16), (IDX_DIM, F32), (LANES, BF16), (N_IDX_HEADS, F32),
        (WIDTH_B, BF16), (WIDTH_B, F32), (WIDTH_B, F32), (WIDTH_B, BF16), (WIDTH_B, BF16), (d, F32), (d, F32)]
    row = lambda i: (i, 0)
    out_specs = [pl.BlockSpec((tm, w), row) for w, _ in out_defs]
    out_shape = [jax.ShapeDtypeStruct((n, w), dt) for w, dt in out_defs]
    if seq_len is not None:
        assert seq_len % tm == 0
        tiles = seq_len // tm
        for k in _NEW_ROW_OUTPUTS:
            w, dt = out_defs[k]
            out_specs[k] = pl.BlockSpec((1, w, tm), lambda i: (i // tiles, 0, i % tiles))
            out_shape[k] = jax.ShapeDtypeStruct((n // seq_len, w, seq_len), dt)
    return pl.pallas_call(
        functools.partial(_proj_kernel, feature_major=seq_len is not None),
        grid=(n // tm,),
        in_specs=[pl.BlockSpec((tm, d), row), _const_spec((1, d))] + [_const_spec(w.shape) for w in w_pieces],
        out_specs=out_specs,
        out_shape=out_shape,
        compiler_params=_cparams(1),
        name="proj",
    )(x2d, ln_g.reshape(1, d), *w_pieces)


def _dsa_prompt_kernel(tab_ref, qi_ref, wit_ref, kid_ref, qa_ref, kad_ref, vadt_ref, tril_ref, o_ref,
                       key_scr, hi16_scr, lo16_scr, lg_scr, bias_scr, acc_scr, *, n_top):
    b = pl.program_id(0)
    i = pl.program_id(1)
    kb, qt = Q_BLOCK, Q_TILE
    bpt = qt // kb
    row = lax.broadcasted_iota(I32, (kb, qt), 0)
    col = lax.broadcasted_iota(I32, (kb, qt), 1)
    lo, hi = _half_masks(BF16)
    top, bottom = _row_half_masks(BF16)
    d_min = 1 - bpt

    def block_dist(j):
        return i * bpt - j

    @pl.when((b == 0) & (i == 0))
    def _():
        for d in range(d_min, 3):
            bucket = _t5_bucket(d * kb + col - row)
            for h in range(N_HEADS_A):
                def sel(bk, acc):
                    return jnp.where(bucket == bk, tab_ref[bk, h], acc)
                tile = lax.fori_loop(0, N_BUCKETS, sel, jnp.zeros((kb, qt), F32))
                bias_scr[d - d_min, h // 2, (h % 2) * kb:(h % 2 + 1) * kb, :] = tile

    def causal(j):
        return block_dist(j) * kb + col - row >= 0

    w = wit_ref[0] * (IDX_DIM ** -0.5)
    wrows = [w[h:h + 1, :] for h in range(N_IDX_HEADS)]

    def score_body(j, carry):
        kblk = kid_ref[0, pl.ds(pl.multiple_of(j * kb, kb), kb), :]
        kbd = _pair_block_diag(kblk, lo, hi)
        acc = jnp.zeros((kb, qt), F32)
        for p in range(N_IDX_HEADS // 2):
            r = lax.dot_general(kbd, qi_ref[0, :, p * LANES:(p + 1) * LANES], _NT,
                                preferred_element_type=F32)
            acc = acc + jnp.maximum(r[:kb], 0.0) * wrows[2 * p] + jnp.maximum(r[kb:], 0.0) * wrows[2 * p + 1]
        key = _order_key(jnp.where(causal(j), acc, -jnp.inf))
        key_scr[j] = key
        hi16_scr[j] = (key >> 16).astype(I16)
        return carry

    _fori_groups(i + 1, bpt, score_body, 0)

    def count(ref, pred, dtype):
        def body(j, acc):
            return acc + jnp.where(pred(ref[j]), jnp.ones((), dtype), jnp.zeros((), dtype))
        acc = _fori_groups(i + 1, bpt, body, jnp.zeros((kb, qt), dtype))
        return jnp.sum(acc.astype(I32), axis=0, keepdims=True)

    def bisect16(ref, wanted):
        t0 = jnp.where(count(ref, lambda k: k >= 0, I16) >= wanted, 0, INT16_MIN).astype(I32)

        def bit_body(b, t):
            cand = t | lax.shift_left(jnp.int32(1), jnp.int32(14) - b)
            c16 = cand.astype(I16)
            return jnp.where(count(ref, lambda k: k >= c16, I16) >= wanted, cand, t)

        return lax.fori_loop(0, 15, bit_body, t0)

    t_hi = bisect16(hi16_scr, n_top)
    t_hi16 = t_hi.astype(I16)
    above = count(hi16_scr, lambda k: k > t_hi16, I16)

    def low_body(j, carry):
        low = ((key_scr[j] & 0xFFFF) + INT16_MIN).astype(I16)
        lo16_scr[j] = jnp.where(hi16_scr[j] == t_hi16, low, jnp.int16(INT16_MIN))
        return carry

    _fori_groups(i + 1, bpt, low_body, 0)
    t_lo = bisect16(lo16_scr, n_top - above)
    thr = lax.shift_left(t_hi, 16) | ((t_lo - INT16_MIN) & 0xFFFF)
    need = (n_top - count(key_scr, lambda k: k > thr, I32)).astype(F32)

    tril = tril_ref[...]

    n_pairs = N_HEADS_A // 2
    q_ps = [(qa_ref[0, :, p * LANES:(p + 1) * LANES].astype(F32) * (HEAD_DIM ** -0.5)).astype(BF16)
            for p in range(n_pairs)]

    def logits_body(j, carry):
        ties_before, maxes = carry[0], carry[1:]
        key = key_scr[j]
        eq = key == thr
        pre = jnp.dot(tril, jnp.where(eq, 1.0, 0.0).astype(BF16), preferred_element_type=F32)
        take = (key > thr) | (eq & (ties_before + pre <= need))
        madd = jnp.where(take & causal(j), 0.0, NEG)
        rows = pl.ds(pl.multiple_of(j * kb, kb), kb)
        dsel = jnp.minimum(block_dist(j), 2) - d_min
        out = [ties_before + pre[kb - 1:kb, :]]
        for p in range(n_pairs):
            c = (2 * p) // GROUP_A
            kbd = _pair_block_diag(kad_ref[0, rows, c * LANES:(c + 1) * LANES], lo, hi)
            lg = lax.dot_general(kbd, q_ps[p], _NT, preferred_element_type=F32) + bias_scr[dsel, p]
            for half in range(2):
                lgh = lg[half * kb:(half + 1) * kb] + madd
                lg_scr[j, p, half * kb:(half + 1) * kb, :] = lgh
                out.append(jnp.maximum(maxes[2 * p + half], jnp.max(lgh, axis=0, keepdims=True)))
        return tuple(out)

    neg = jnp.full((1, qt), NEG, F32)
    zero = jnp.zeros((1, qt), F32)
    maxes = _fori_groups(i + 1, bpt, logits_body, (zero,) + (neg,) * N_HEADS_A)[1:]
    acc_scr[...] = jnp.zeros(acc_scr.shape, F32)

    def pv_body(j, sums):
        out = []
        for p in range(n_pairs):
            c = (2 * p) // GROUP_A
            vbd = _pair_block_diag_t(vadt_ref[0, j, c * LANES:(c + 1) * LANES, :], top, bottom)
            p0 = jnp.exp(lg_scr[j, p, :kb, :] - maxes[2 * p])
            p1 = jnp.exp(lg_scr[j, p, kb:, :] - maxes[2 * p + 1])
            out += [sums[2 * p] + jnp.sum(p0, axis=0, keepdims=True),
                    sums[2 * p + 1] + jnp.sum(p1, axis=0, keepdims=True)]
            acc_scr[p] += jnp.dot(vbd, jnp.concatenate([p0, p1], axis=0).astype(BF16),
                                  preferred_element_type=F32)
        return tuple(out)

    sums = _fori_groups(i + 1, bpt, pv_body, (zero,) * N_HEADS_A)
    for p in range(n_pairs):
        denom = jnp.where(row < HEAD_DIM, sums[2 * p], sums[2 * p + 1])
        o_ref[0, p * LANES:(p + 1) * LANES, :] = (acc_scr[p] / denom).astype(o_ref.dtype)


def _dsa_prompt(bias_table, qi, wit, kid, qa, kad, vadt, n_top):
    bsz, seq, _ = qi.shape
    assert seq % Q_TILE == 0 and Q_TILE % Q_BLOCK == 0
    nkb = seq // Q_BLOCK
    n_bias = Q_TILE // Q_BLOCK + 2
    tril = (np.arange(Q_BLOCK)[:, None] >= np.arange(Q_BLOCK)[None, :]).astype(np.float32)
    qblk = lambda w: pl.BlockSpec((1, Q_TILE, w), lambda b, i: (b, i, 0))
    full = lambda w: pl.BlockSpec((1, seq, w), lambda b, i: (b, 0, 0))
    return pl.pallas_call(
        functools.partial(_dsa_prompt_kernel, n_top=n_top),
        grid=(bsz, seq // Q_TILE),
        in_specs=[pl.BlockSpec(memory_space=pltpu.SMEM), qblk(N_IDX_HEADS * IDX_DIM),
                  pl.BlockSpec((1, N_IDX_HEADS, Q_TILE), lambda b, i: (b, 0, i)), full(LANES),
                  qblk(WIDTH_A), full(2 * LANES),
                  pl.BlockSpec((1, nkb, 2 * LANES, Q_BLOCK), lambda b, i: (b, 0, 0, 0)),
                  _const_spec((Q_BLOCK, Q_BLOCK))],
        out_specs=pl.BlockSpec((1, WIDTH_A, Q_TILE), lambda b, i: (b, 0, i)),
        out_shape=jax.ShapeDtypeStruct((bsz, WIDTH_A, seq), BF16),
        scratch_shapes=[pltpu.VMEM((nkb, Q_BLOCK, Q_TILE), I32), pltpu.VMEM((nkb, Q_BLOCK, Q_TILE), I16),
                        pltpu.VMEM((nkb, Q_BLOCK, Q_TILE), I16),
                        pltpu.VMEM((nkb, N_HEADS_A // 2, 2 * Q_BLOCK, Q_TILE), F32),
                        pltpu.VMEM((n_bias, N_HEADS_A // 2, 2 * Q_BLOCK, Q_TILE), F32),
                        pltpu.VMEM((N_HEADS_A // 2, LANES, Q_TILE), F32)],
        compiler_params=_cparams(2),
        name="dsa_prompt",
    )(bias_table, qi, wit, kid, qa, kad, vadt, jnp.asarray(tril, BF16))


def _suffix_matrix(n_heads):
    n = n_heads * Q_BLOCK
    r = np.arange(n)
    same = (r[:, None] // Q_BLOCK) == (r[None, :] // Q_BLOCK)
    u = same & (r[:, None] > r[None, :])
    w = np.concatenate([u, same], axis=1).astype(np.float32)
    return jnp.asarray(np.concatenate([w, w], axis=0), BF16)


def _log_keep(z):
    return -(jnp.maximum(z, 0.0) + jnp.log(1.0 + jnp.exp(-jnp.abs(z))))


def _split_hi_lo(x):
    hi = x.astype(BF16)
    lo = (x - hi.astype(F32)).astype(BF16)
    return jnp.concatenate([hi, lo], axis=1)


def _sb_prompt_kernel(qb_ref, kb_ref, vb_ref, w_ref, o_ref, run_scr, acc_scr):
    i = pl.program_id(1)
    kb, qt = Q_BLOCK, SB_TILE
    bpt = qt // kb
    n_pairs = N_HEADS_B // 2
    row = lax.broadcasted_iota(I32, (qt, 2 * kb), 0)
    col = lax.broadcasted_iota(I32, (qt, 2 * kb), 1) & (kb - 1)
    lo, hi = _half_masks(BF16)
    q_ps = [(qb_ref[0, :, p * LANES:(p + 1) * LANES].astype(F32) * (HEAD_DIM ** -0.5)).astype(BF16)
            for p in range(n_pairs)]

    def block(j, edge, first):
        rows = pl.ds(pl.multiple_of(j * kb, kb), kb)
        if edge:
            strict = j * kb + col < i * qt + row
        for p in range(n_pairs):
            kbd = _pair_block_diag(kb_ref[0, rows, p * LANES:(p + 1) * LANES], lo, hi)
            vbd = _pair_block_diag(vb_ref[0, rows, p * LANES:(p + 1) * LANES], lo, hi)
            z = lax.dot_general(q_ps[p], kbd, _NT, preferred_element_type=F32)
            lk = _log_keep(z)
            if edge:
                lk = jnp.where(strict, lk, 0.0)
            res = jnp.dot(_split_hi_lo(lk), w_ref[...], preferred_element_type=F32)
            x = z + lk + res[:, :2 * kb]
            if not first:
                x = x + run_scr[p]
            a = jnp.exp(x)
            if edge:
                a = jnp.where(strict, a, 0.0)
            pv = jnp.dot(a.astype(BF16), vbd, preferred_element_type=F32)
            if first:
                run_scr[p] = res[:, 2 * kb:]
                acc_scr[p] = pv
            else:
                run_scr[p] += res[:, 2 * kb:]
                acc_scr[p] += pv

    for e in range(bpt):
        block((i + 1) * bpt - 1 - e, True, e == 0)

    def body(t, carry):
        block(i * bpt - 1 - t, False, False)
        return carry

    _fori_groups(i, bpt, body, 0)
    for p in range(n_pairs):
        o_ref[0, :, p * LANES:(p + 1) * LANES] = acc_scr[p].astype(o_ref.dtype)


def _sb_prompt(qb, kb16, vb16):
    bsz, seq, width = qb.shape
    assert seq % SB_TILE == 0 and SB_TILE % Q_BLOCK == 0
    wmat = _suffix_matrix(2)
    qblk = pl.BlockSpec((1, SB_TILE, width), lambda b, i: (b, i, 0))
    full = pl.BlockSpec((1, seq, width), lambda b, i: (b, 0, 0))
    return pl.pallas_call(
        _sb_prompt_kernel,
        grid=(bsz, seq // SB_TILE),
        in_specs=[qblk, full, full, _const_spec(wmat.shape)],
        out_specs=qblk,
        out_shape=jax.ShapeDtypeStruct((bsz, seq, width), BF16),
        scratch_shapes=[pltpu.VMEM((N_HEADS_B // 2, SB_TILE, 2 * Q_BLOCK), F32),
                        pltpu.VMEM((N_HEADS_B // 2, SB_TILE, LANES), F32)],
        compiler_params=_cparams(2),
        name="sb_prompt",
    )(qb, kb16, vb16, wmat)


def _page_specs(block_tail, n_pages, reverse, pages=None):
    pages = pages or SMALL_PAGES

    def spec(p):
        def index_map(b, g, pt):
            page = g * pages + p
            if reverse:
                page = n_pages - 1 - page
            return (pt[b, page],) + (0,) * len(block_tail)
        return pl.BlockSpec((1,) + block_tail, index_map)
    return [spec(p) for p in range(pages)]


def _idx_sample_kernel(pt_ref, q_ref, w_ref, *refs):
    k_refs, o_ref = refs[:SMALL_PAGES], refs[SMALL_PAGES]
    q = q_ref[0].astype(BF16)
    w = w_ref[0] * (IDX_DIM ** -0.5)
    k_all = jnp.concatenate([k[0].astype(BF16) for k in k_refs], axis=1)
    r = jnp.dot(q, k_all, preferred_element_type=F32)
    score = jnp.sum(jnp.maximum(r, 0.0) * w, axis=0, keepdims=True)
    for p in range(SMALL_PAGES):
        o_ref[0, p] = score[:, p * LANES:(p + 1) * LANES]


def _idx_sample(page_table, qi3, wi3, cache_kidx):
    bd, n_pages = page_table.shape
    per_b = lambda shape: pl.BlockSpec((1,) + shape, lambda b, g, pt: (b,) + (0,) * len(shape))
    return pl.pallas_call(
        _idx_sample_kernel,
        grid_spec=pltpu.PrefetchScalarGridSpec(
            num_scalar_prefetch=1,
            grid=(bd, n_pages // SMALL_PAGES),
            in_specs=[per_b((N_IDX_HEADS, IDX_DIM)), per_b((N_IDX_HEADS, 1))]
            + _page_specs((IDX_DIM, Q_BLOCK), n_pages, False),
            out_specs=pl.BlockSpec((1, SMALL_PAGES, 1, LANES), lambda b, g, pt: (b, g, 0, 0)),
        ),
        out_shape=jax.ShapeDtypeStruct((bd, n_pages, 1, LANES), F32),
        compiler_params=_cparams(2),
        name="idx_sample",
    )(page_table, qi3, wi3, *([cache_kidx] * SMALL_PAGES))


def _select_sample_kernel(sc_ref, q_ref, w_ref, k_ref, tri_ref, low_ref, madd_o, maddn_o, *, n_top):
    n_seq = sc_ref.shape[0]
    seqs = range(n_seq)

    def total(x):
        return jnp.sum(jnp.sum(x, axis=1, keepdims=True), axis=0, keepdims=True)

    keys, keys_new = [], []
    for s in seqs:
        keys.append(_order_key(sc_ref[s]))
        r_new = jnp.sum(q_ref[s] * k_ref[s], axis=1, keepdims=True)
        s_new = jnp.sum(jnp.maximum(r_new * (IDX_DIM ** -0.5), 0.0) * w_ref[s], axis=0, keepdims=True)
        keys_new.append(_order_key(s_new))

    def count(s, pred):
        return total(jnp.where(pred(keys[s]), 1, 0)) + jnp.where(pred(keys_new[s]), 1, 0)

    thrs = tuple(jnp.where(count(s, lambda k: k >= 0) >= n_top, 0, INT_MIN).astype(I32) for s in seqs)

    def bit_body(t, thrs):
        bit = lax.shift_left(jnp.int32(1), jnp.int32(30) - t)
        cands = [thr | bit for thr in thrs]
        return tuple(jnp.where(count(s, lambda k: k >= cands[s]) >= n_top, cands[s], thrs[s]) for s in seqs)

    thrs = lax.fori_loop(0, 31, bit_body, thrs)
    for s in seqs:
        key, key_new, thr = keys[s], keys_new[s], thrs[s]
        need = (n_top - count(s, lambda k: k > thr)).astype(F32)
        eq = jnp.where(key == thr, 1.0, 0.0).astype(BF16)
        pre = jnp.dot(eq, tri_ref[...], preferred_element_type=F32)
        tot = jnp.broadcast_to(pre[:, LANES - 1:LANES], pre.shape).astype(BF16)
        before = jnp.dot(low_ref[...], tot, preferred_element_type=F32)
        take = (key > thr) | ((key == thr) & (before + pre <= need))
        madd_o[s] = jnp.where(take, 0.0, NEG)
        ties_past = total(jnp.where(key == thr, 1.0, 0.0))
        take_new = (key_new > thr) | ((key_new == thr) & (ties_past + 1.0 <= need))
        maddn_o[s] = jnp.where(take_new, 0.0, NEG)


def _select_sample(scores, qi3, wi3, ki3, n_top):
    bd, n_pages, _ = scores.shape
    n_seq = math.gcd(bd, SELECT_SEQS)
    tri = jnp.asarray((np.arange(LANES)[:, None] <= np.arange(LANES)[None, :]).astype(np.float32), BF16)
    low = jnp.asarray((np.arange(n_pages)[:, None] > np.arange(n_pages)[None, :]).astype(np.float32), BF16)
    per_b = lambda shape: pl.BlockSpec((n_seq,) + shape, lambda b: (b,) + (0,) * len(shape))
    return pl.pallas_call(
        functools.partial(_select_sample_kernel, n_top=n_top),
        grid=(bd // n_seq,),
        in_specs=[per_b((n_pages, LANES)), per_b((N_IDX_HEADS, IDX_DIM)), per_b((N_IDX_HEADS, 1)),
                  per_b((1, IDX_DIM)), _const_spec(tri.shape), _const_spec(low.shape)],
        out_specs=[per_b((n_pages, LANES)), per_b((1, 1))],
        out_shape=[jax.ShapeDtypeStruct((bd, n_pages, LANES), F32), jax.ShapeDtypeStruct((bd, 1, 1), F32)],
        compiler_params=_cparams(1),
        name="select_sample",
    )(scores, qi3, wi3, ki3, tri, low)


def _sample_bias(tabt, dist):
    bucket = _t5_bucket(dist)
    out = jnp.zeros((tabt.shape[0], dist.shape[1]), F32)
    for bk in range(N_BUCKETS):
        out = jnp.where(bucket == bk, tabt[:, bk:bk + 1], out)
    return out


def _dsa_logits_kernel(pt_ref, tabt_ref, qz_ref, madd_ref, *refs, past_len):
    k_refs, o_ref = refs[:SMALL_PAGES], refs[SMALL_PAGES]
    g = pl.program_id(1)
    qz16 = qz_ref[0].astype(BF16)
    k_all = jnp.concatenate([k[0].astype(BF16) for k in k_refs], axis=1)
    lg = jnp.dot(qz16, k_all, preferred_element_type=F32)
    key_pos = g * (SMALL_PAGES * Q_BLOCK) + lax.broadcasted_iota(I32, (1, SMALL_PAGES * Q_BLOCK), 1)
    lg = lg + _sample_bias(tabt_ref[...], past_len - key_pos)
    for p in range(SMALL_PAGES):
        o_ref[0, p] = lg[:, p * LANES:(p + 1) * LANES] + madd_ref[0, p]


def _dsa_logits(page_table, bias_table, qz, madd, cache_kt, past_len):
    bd, n_pages = page_table.shape
    per_b = lambda shape: pl.BlockSpec((1,) + shape, lambda b, g, pt: (b,) + (0,) * len(shape))
    return pl.pallas_call(
        functools.partial(_dsa_logits_kernel, past_len=past_len),
        grid_spec=pltpu.PrefetchScalarGridSpec(
            num_scalar_prefetch=1,
            grid=(bd, n_pages // SMALL_PAGES),
            in_specs=[pl.BlockSpec((N_HEADS_A, N_BUCKETS), lambda b, g, pt: (0, 0)),
                      per_b((N_HEADS_A, LANES)),
                      pl.BlockSpec((1, SMALL_PAGES, 1, LANES), lambda b, g, pt: (b, g, 0, 0))]
            + _page_specs((N_KV_A * HEAD_DIM, Q_BLOCK), n_pages, False),
            out_specs=pl.BlockSpec((1, SMALL_PAGES, N_HEADS_A, LANES), lambda b, g, pt: (b, g, 0, 0)),
        ),
        out_shape=jax.ShapeDtypeStruct((bd, n_pages, N_HEADS_A, LANES), F32),
        compiler_params=_cparams(2),
        name="dsa_logits",
    )(page_table, bias_table.T, qz, madd, *([cache_kt] * SMALL_PAGES))


def _dsa_pv_kernel(pt_ref, tabt_ref, lg_ref, qz_ref, maddn_ref, kn_ref, vnc_ref, *refs):
    v_refs = refs[:SMALL_PAGES]
    o_ref, m_scr, lsum_scr, acc_scr = refs[SMALL_PAGES:]
    g = pl.program_id(1)

    def new_logit():
        lg = jnp.sum(qz_ref[0] * kn_ref[0], axis=1, keepdims=True)
        return lg + _sample_bias(tabt_ref[...], jnp.zeros((1, 1), I32)) + maddn_ref[0]

    @pl.when(g == 0)
    def _():
        m_past = jnp.max(jnp.max(lg_ref[0], axis=0), axis=1, keepdims=True)
        m_scr[...] = jnp.maximum(m_past, new_logit())
        lsum_scr[...] = jnp.zeros(lsum_scr.shape, F32)
        acc_scr[...] = jnp.zeros(acc_scr.shape, F32)

    m = m_scr[...]
    prs = [jnp.exp(lg_ref[0, g * SMALL_PAGES + p] - m) for p in range(SMALL_PAGES)]
    lsum_scr[...] += functools.reduce(lambda a, b: a + b, prs)
    v_all = jnp.concatenate([v[0].astype(BF16) for v in v_refs], axis=1)
    acc_scr[...] += lax.dot_general(v_all, jnp.concatenate(prs, axis=1).astype(BF16), _NT,
                                    preferred_element_type=F32)

    @pl.when(g == pl.num_programs(1) - 1)
    def _():
        p_new = jnp.exp(new_logit() - m)
        denom = jnp.sum(lsum_scr[...], axis=1, keepdims=True) + p_new
        out = (acc_scr[...] + vnc_ref[0] * _col_to_row(p_new)) / _col_to_row(denom)
        r = lax.broadcasted_iota(I32, out.shape, 0)
        h = lax.broadcasted_iota(I32, out.shape, 1)
        out = jnp.where(_div_pow2(r, HEAD_DIM) == _div_pow2(h, GROUP_A), out, 0.0)
        o_ref[0] = out[:HEAD_DIM] + out[HEAD_DIM:]


def _dsa_pv(page_table, bias_table, logits, qz, maddn, k_new, v_new_col, cache_vt):
    bd, n_pages = page_table.shape
    per_b = lambda shape: pl.BlockSpec((1,) + shape, lambda b, g, pt: (b,) + (0,) * len(shape))
    kv = N_KV_A * HEAD_DIM
    return pl.pallas_call(
        _dsa_pv_kernel,
        grid_spec=pltpu.PrefetchScalarGridSpec(
            num_scalar_prefetch=1,
            grid=(bd, n_pages // SMALL_PAGES),
            in_specs=[pl.BlockSpec((N_HEADS_A, N_BUCKETS), lambda b, g, pt: (0, 0)),
                      per_b((n_pages, N_HEADS_A, LANES)), per_b((N_HEADS_A, kv)), per_b((1, 1)),
                      per_b((1, kv)), per_b((kv, 1))] + _page_specs((kv, Q_BLOCK), n_pages, False),
            out_specs=per_b((HEAD_DIM, N_HEADS_A)),
            scratch_shapes=[pltpu.VMEM((N_HEADS_A, 1), F32), pltpu.VMEM((N_HEADS_A, LANES), F32),
                            pltpu.VMEM((kv, N_HEADS_A), F32)],
        ),
        out_shape=jax.ShapeDtypeStruct((bd, HEAD_DIM, N_HEADS_A), F32),
        compiler_params=_cparams(2),
        name="dsa_pv",
    )(page_table, bias_table.T, logits, qz, maddn, k_new, v_new_col, *([cache_vt] * SMALL_PAGES))


def _sb_sample_kernel(pt_ref, qbd_ref, kn_ref, vnc_ref, w_ref, *refs, past_len):
    k_refs = refs[:SB_PAGES]
    v_refs = refs[SB_PAGES:2 * SB_PAGES]
    o_ref, run_scr, acc_scr = refs[2 * SB_PAGES:]
    g = pl.program_id(1)
    n_steps = pl.num_programs(1)
    n_pages = n_steps * SB_PAGES
    qbd = qbd_ref[0]
    qbd16 = qbd.astype(BF16)
    lane = lax.broadcasted_iota(I32, (1, LANES), 1)
    q_pos = past_len

    @pl.when(g == 0)
    def _():
        pos = past_len + lax.broadcasted_iota(I32, (N_HEADS_B, 1), 1)
        keep = pos < q_pos
        z = jnp.sum(qbd * kn_ref[0], axis=1, keepdims=True)
        lk = jnp.where(keep, _log_keep(z), 0.0)
        a = jnp.where(keep, jnp.exp(z + lk), 0.0)
        run_scr[...] = jnp.broadcast_to(lk, run_scr.shape)
        acc_scr[...] = vnc_ref[0] * _col_to_row(a)

    pages = range(SB_PAGES)
    nh = N_HEADS_B
    k_all = jnp.concatenate([k_refs[p][0].astype(BF16) for p in pages], axis=1)
    z_all = jnp.dot(qbd16, k_all, preferred_element_type=F32)
    z = jnp.concatenate([z_all[:, p * LANES:(p + 1) * LANES] for p in pages], axis=0)
    visit = g * SB_PAGES + _div_pow2(lax.broadcasted_iota(I32, z.shape, 0), nh)
    key_pos = (n_pages - 1 - visit) * Q_BLOCK + lax.broadcasted_iota(I32, z.shape, 1)
    keep = key_pos < q_pos
    lk = jnp.where(keep, _log_keep(z), 0.0)
    res = jnp.dot(_split_hi_lo(lk), w_ref[...], preferred_element_type=F32)
    run = run_scr[...]
    runs = []
    for p in pages:
        runs.append(run)
        run = run + res[p * nh:(p + 1) * nh, LANES:]
    run_scr[...] = run
    a = jnp.where(keep, jnp.exp(z + lk + res[:, :LANES] + jnp.concatenate(runs, axis=0)), 0.0)
    a_all = jnp.concatenate([a[p * nh:(p + 1) * nh] for p in pages], axis=1).astype(BF16)
    v_all = jnp.concatenate([v_refs[p][0].astype(BF16) for p in pages], axis=1)
    acc_scr[...] += lax.dot_general(v_all, a_all, _NT, preferred_element_type=F32)

    @pl.when(g == n_steps - 1)
    def _():
        r = lax.broadcasted_iota(I32, acc_scr.shape, 0)
        h = lax.broadcasted_iota(I32, acc_scr.shape, 1)
        o_ref[0] = jnp.sum(jnp.where(_div_pow2(r, HEAD_DIM) == h, acc_scr[...], 0.0), axis=1, keepdims=True)


def _sb_sample(page_table, qbd, k_new, v_new_col, cache_kt, cache_vt, past_len):
    bd, n_pages = page_table.shape
    width = cache_kt.shape[1]
    wmat = _suffix_matrix(1)
    per_b = lambda shape: pl.BlockSpec((1,) + shape, lambda b, g, pt: (b,) + (0,) * len(shape))
    kv_specs = _page_specs((width, Q_BLOCK), n_pages, True, SB_PAGES)
    return pl.pallas_call(
        functools.partial(_sb_sample_kernel, past_len=past_len),
        grid_spec=pltpu.PrefetchScalarGridSpec(
            num_scalar_prefetch=1,
            grid=(bd, n_pages // SB_PAGES),
            in_specs=[per_b((N_HEADS_B, width)), per_b((1, width)), per_b((width, 1)),
                      pl.BlockSpec(wmat.shape, lambda b, g, pt: (0, 0))] + kv_specs + kv_specs,
            out_specs=per_b((width, 1)),
            scratch_shapes=[pltpu.VMEM((N_HEADS_B, LANES), F32), pltpu.VMEM((width, N_HEADS_B), F32)],
        ),
        out_shape=jax.ShapeDtypeStruct((bd, width, 1), F32),
        compiler_params=_cparams(2),
        name="sb_sample",
    )(page_table, qbd, k_new, v_new_col, wmat, *([cache_kt] * SB_PAGES), *([cache_vt] * SB_PAGES))


def _route(logits, axis):
    slot = lax.broadcasted_iota(I32, logits.shape, axis)
    big = jnp.int32(ROUTER_LANES)

    def softmax_over(mask):
        m = jnp.max(jnp.where(mask, logits, NEG), axis=axis, keepdims=True)
        e = jnp.where(mask, jnp.exp(logits - m), 0.0)
        return e / jnp.sum(e, axis=axis, keepdims=True)

    def top1(prob, mask):
        best = jnp.max(jnp.where(mask, prob, -1.0), axis=axis, keepdims=True)
        idx = jnp.min(jnp.where(mask & (prob == best), slot, big), axis=axis, keepdims=True)
        return best, idx

    is_group = slot < N_GROUPS
    g_w, g_idx = top1(softmax_over(is_group), is_group)
    e_slot = slot - N_GROUPS
    in_group = (e_slot >= 0) & (e_slot < N_EXPERTS) & (_div_pow2(e_slot, EXPERTS_PER_GROUP) == g_idx)
    p_e = softmax_over(in_group)
    p1, i1 = top1(p_e, in_group)
    p2, i2 = top1(p_e, in_group & (slot != i1))
    denom = p1 + p2
    return slot, i1, p1 / denom * g_w, i2, p2 / denom * g_w


def _merge_kernel(x_ref, oa_ref, ob_ref, ga_ref, gb_ref, wpa, wpb, wout, ln2_ref, wr, br,
                  h_o, hn_o, route_o, *, grouped):
    mixed = (ga_ref[...] * jnp.dot(oa_ref[...], wpa[...], preferred_element_type=F32)
             + gb_ref[...] * jnp.dot(ob_ref[...], wpb[...], preferred_element_type=F32))
    h = x_ref[...] + jnp.dot(mixed.astype(BF16), wout[...], preferred_element_type=F32)
    h_o[...] = h
    ms = jnp.mean(h * h, axis=-1, keepdims=True)
    hn = (h * lax.rsqrt(ms + EPS) * ln2_ref[...]).astype(BF16)

    if grouped:
        half = hn.shape[1] // 2
        hi = lax.bitcast_convert_type(hn[:, :half].astype(F32), jnp.uint32)
        lo = lax.bitcast_convert_type(hn[:, half:].astype(F32), jnp.uint32)
        hn_o[...] = hi | (lo >> 16)
        logits_t = lax.dot_general(wr[...], hn, _NT, preferred_element_type=F32) + br[...]
        _, i1, w1, i2, w2 = _route(logits_t, 0)
        rows = [(i1 - N_GROUPS).astype(F32), (i2 - N_GROUPS).astype(F32), w1, w2]
        route_o[...] = jnp.concatenate(rows + [jnp.zeros_like(w1)] * (route_o.shape[0] - len(rows)), axis=0)
    else:
        hn_o[...] = hn
        logits = jnp.dot(hn, wr[...], preferred_element_type=F32) + br[...]
        slot, i1, w1, i2, w2 = _route(logits, 1)
        route_o[...] = jnp.where(slot == i1, w1, 0.0) + jnp.where(slot == i2, w2, 0.0)


def _merge(x2d, o_a, o_b, g_a, g_b, wpa, wpb, wout, ln2, wr, br, tm, grouped):
    n, d = x2d.shape
    row = lambda w: pl.BlockSpec((tm, w), lambda i: (i, 0))
    if grouped:
        consts = (wpa, wpb, wout, ln2, wr.T, br.T)
        hn_spec, hn_shape = row(d // 2), jax.ShapeDtypeStruct((n, d // 2), jnp.uint32)
        rt_spec = pl.BlockSpec((ROUTE_ROWS, tm), lambda i: (0, i))
        rt_shape = jax.ShapeDtypeStruct((ROUTE_ROWS, n), F32)
    else:
        consts = (wpa, wpb, wout, ln2, wr, br)
        hn_spec, hn_shape = row(d), jax.ShapeDtypeStruct((n, d), BF16)
        rt_spec, rt_shape = row(ROUTER_LANES), jax.ShapeDtypeStruct((n, ROUTER_LANES), F32)
    return pl.pallas_call(
        functools.partial(_merge_kernel, grouped=grouped),
        grid=(n // tm,),
        in_specs=[row(d), row(WIDTH_A), row(WIDTH_B), row(d), row(d)] + [_const_spec(c.shape) for c in consts],
        out_specs=[row(d), hn_spec, rt_spec],
        out_shape=[jax.ShapeDtypeStruct((n, d), F32), hn_shape, rt_shape],
        compiler_params=_cparams(1),
        name="merge",
    )(x2d, o_a, o_b, g_a, g_b, *consts)


def _moe_kernel(hn_ref, comb_ref, h_ref, wgu_ref, wd_ref, lnf_ref, y_ref, acc_scr, *, final_norm):
    e = pl.program_id(1)

    @pl.when(e == 0)
    def _():
        acc_scr[...] = jnp.zeros(acc_scr.shape, F32)

    d_expert = wd_ref.shape[1]
    gu = jnp.dot(hn_ref[...], wgu_ref[0], preferred_element_type=F32)
    gate, up = gu[:, :d_expert], gu[:, d_expert:]
    comb = comb_ref[...]
    lane = lax.broadcasted_iota(I32, comb.shape, 1)
    c_e = jnp.sum(jnp.where(lane == e + N_GROUPS, comb, 0.0), axis=1, keepdims=True)
    hid = gate * _sigmoid(gate) * up * c_e
    acc_scr[...] += jnp.dot(hid.astype(BF16), wd_ref[0], preferred_element_type=F32)

    @pl.when(e == pl.num_programs(1) - 1)
    def _():
        xo = h_ref[...] + acc_scr[...]
        if final_norm:
            ms = jnp.mean(xo * xo, axis=-1, keepdims=True)
            xo = xo * lax.rsqrt(ms + EPS) * lnf_ref[...]
        y_ref[...] = xo


def _moe(hn, comb, h, wgu, wd, lnf, tm, final_norm):
    n, d = h.shape
    n_exp, _, two_f = wgu.shape
    row = lambda w: pl.BlockSpec((tm, w), lambda i, e: (i, 0))
    return pl.pallas_call(
        functools.partial(_moe_kernel, final_norm=final_norm),
        grid=(n // tm, n_exp),
        in_specs=[row(d), row(ROUTER_LANES), row(d),
                  pl.BlockSpec((1, d, two_f), lambda i, e: (e, 0, 0)),
                  pl.BlockSpec((1, two_f // 2, d), lambda i, e: (e, 0, 0)),
                  pl.BlockSpec((1, d), lambda i, e: (0, 0))],
        out_specs=row(d),
        out_shape=jax.ShapeDtypeStruct((n, d), F32),
        scratch_shapes=[pltpu.VMEM((tm, d), F32)],
        compiler_params=_cparams(2),
        name="moe",
    )(hn, comb, h, wgu, wd, lnf)


def _moe_plan_kernel(rt_ref, before_ref, lower_ref, pos_o, tt_o):
    ch = rt_ref.shape[1]
    e1 = rt_ref[0:1, :].astype(I32)
    e2 = rt_ref[1:2, :].astype(I32)
    sub = lax.broadcasted_iota(I32, (ROUTER_LANES, ch), 0)
    a1, a2 = sub == e1, sub == e2
    chosen = (jnp.where(a1, 1.0, 0.0) + jnp.where(a2, 1.0, 0.0)).astype(BF16)
    before = jnp.dot(chosen, before_ref[...], preferred_element_type=F32)
    cnt = before[:, ch - 1:ch] + chosen[:, ch - 1:ch].astype(F32)
    ntile = jnp.floor((cnt + (MOE_ROWS - 1)) * (1.0 / MOE_ROWS))
    base_t = jnp.dot(lower_ref[...], jnp.broadcast_to(ntile, (ROUTER_LANES, LANES)).astype(BF16),
                     preferred_element_type=F32)[:, 0:1]
    slot = before + base_t * MOE_ROWS
    pos = [jnp.sum(jnp.where(a, slot, 0.0), axis=0, keepdims=True) for a in (a1, a2)]
    pos_o[0] = jnp.concatenate(pos + [jnp.zeros_like(pos[0])] * (pos_o.shape[1] - 2), axis=0).astype(I32)

    tile = lax.broadcasted_iota(I32, (ROUTER_LANES, LANES), 1).astype(F32)
    exp = lax.broadcasted_iota(I32, (ROUTER_LANES, LANES), 0)
    is_exp = exp < N_EXPERTS
    te = jnp.sum(jnp.where(is_exp & (base_t + ntile <= tile), 1, 0), axis=0, keepdims=True)
    total = jnp.sum(jnp.where(is_exp, ntile, 0.0), axis=0, keepdims=True).astype(I32) + jnp.zeros_like(te)
    rows = [jnp.minimum(te, N_EXPERTS - 1), total]
    tt_o[0] = jnp.concatenate(rows + [jnp.zeros_like(te)] * (tt_o.shape[1] - len(rows)), axis=0)


def _moe_routed_kernel(tt_ref, pos_ref, rt_ref, hnp_ref, h_ref, *refs, max_tiles, final_norm):
    w_refs = refs[:2 * MOE_TILES_PER_STEP]
    lnf_ref, y_ref, rows_scr = refs[2 * MOE_TILES_PER_STEP:]
    c = pl.program_id(0)
    i = pl.program_id(1)
    ch, half = hnp_ref.shape
    n_tiles = tt_ref[(c * 2 + 1) * max_tiles]

    @pl.when(i == 0)
    def _():
        rows_scr[:, :half] = jnp.zeros((rows_scr.shape[0], half), F32)

        def place(t, carry):
            row = lax.bitcast_convert_type(hnp_ref[pl.ds(t, 1), :], F32)
            for k in range(TOP_E):
                rows_scr[pl.ds(pos_ref[0, k, t], 1), :half] = row
            return carry

        lax.fori_loop(0, ch, place, 0, unroll=MOE_UNROLL)

    def expert_tile(ti, wgu_ref, wd_ref):
        tile = pl.ds(pl.multiple_of(ti * MOE_ROWS, MOE_ROWS), MOE_ROWS)
        packed = lax.bitcast_convert_type(rows_scr[tile, :half], jnp.uint32)
        first = lax.bitcast_convert_type(packed & jnp.uint32(0xFFFF0000), F32).astype(BF16)
        second = lax.bitcast_convert_type(packed << 16, F32).astype(BF16)
        x = jnp.concatenate([first, second], axis=1)
        d_expert = wd_ref.shape[1]
        gu = jnp.dot(x, wgu_ref[0], preferred_element_type=F32)
        gate, up = gu[:, :d_expert], gu[:, d_expert:]
        hid = gate * _sigmoid(gate) * up
        rows_scr[tile, :] = jnp.dot(hid.astype(BF16), wd_ref[0], preferred_element_type=F32)

    for u in range(MOE_TILES_PER_STEP):
        ti = i * MOE_TILES_PER_STEP + u
        pl.when(ti < n_tiles)(functools.partial(expert_tile, ti, w_refs[2 * u], w_refs[2 * u + 1]))

    @pl.when(i == pl.num_programs(1) - 1)
    def _():
        def combine(t, carry):
            acc = h_ref[pl.ds(t, 1), :]
            for k in range(TOP_E):
                acc = acc + rt_ref[TOP_E + k, t] * rows_scr[pl.ds(pos_ref[0, k, t], 1), :]
            y_ref[pl.ds(t, 1), :] = acc
            return carry

        lax.fori_loop(0, ch, combine, 0, unroll=MOE_UNROLL)
        if final_norm:
            xo = y_ref[...]
            ms = jnp.mean(xo * xo, axis=-1, keepdims=True)
            y_ref[...] = xo * lax.rsqrt(ms + EPS) * lnf_ref[...]


def _moe_routed(hnp, rt, h, wgu, wd, lnf, final_norm):
    n, d = h.shape
    ch = MOE_CHUNK
    assert n % ch == 0
    n_chunks = n // ch
    max_tiles = TOP_E * ch // MOE_ROWS + N_EXPERTS
    assert max_tiles <= LANES and max_tiles % MOE_TILES_PER_STEP == 0
    before = jnp.asarray((np.arange(ch)[:, None] < np.arange(ch)[None, :]).astype(np.float32), BF16)
    lower = jnp.asarray((np.arange(ROUTER_LANES)[:, None] > np.arange(ROUTER_LANES)[None, :]).astype(np.float32),
                        BF16)
    pos, tt = pl.pallas_call(
        _moe_plan_kernel,
        grid=(n_chunks,),
        in_specs=[pl.BlockSpec((ROUTE_ROWS, ch), lambda c: (0, c)), _const_spec(before.shape),
                  _const_spec(lower.shape)],
        out_specs=[pl.BlockSpec((1, ROUTE_ROWS, ch), lambda c: (c, 0, 0)),
                   pl.BlockSpec((1, ROUTE_ROWS, LANES), lambda c: (c, 0, 0))],
        out_shape=[jax.ShapeDtypeStruct((n_chunks, ROUTE_ROWS, ch), I32),
                   jax.ShapeDtypeStruct((n_chunks, ROUTE_ROWS, LANES), I32)],
        compiler_params=_cparams(1),
        name="moe_plan",
    )(rt, before, lower)
    tt_flat = tt[:, :2, :max_tiles].reshape(-1)
    n_exp, _, two_f = wgu.shape
    tps = MOE_TILES_PER_STEP
    w_specs, w_args = [], []
    for u in range(tps):
        expert_of = lambda c, i, tt, u=u: (tt[c * 2 * max_tiles + i * tps + u], 0, 0)
        w_specs += [pl.BlockSpec((1, d, two_f), expert_of), pl.BlockSpec((1, two_f // 2, d), expert_of)]
        w_args += [wgu, wd]
    return pl.pallas_call(
        functools.partial(_moe_routed_kernel, max_tiles=max_tiles, final_norm=final_norm),
        grid_spec=pltpu.PrefetchScalarGridSpec(
            num_scalar_prefetch=1,
            grid=(n_chunks, max_tiles // tps),
            in_specs=[pl.BlockSpec((1, ROUTE_ROWS, ch), lambda c, i, tt: (c, 0, 0), memory_space=pltpu.SMEM),
                      pl.BlockSpec((ROUTE_ROWS, ch), lambda c, i, tt: (0, c), memory_space=pltpu.SMEM),
                      pl.BlockSpec((ch, d // 2), lambda c, i, tt: (c, 0)),
                      pl.BlockSpec((ch, d), lambda c, i, tt: (c, 0))] + w_specs
            + [pl.BlockSpec((1, d), lambda c, i, tt: (0, 0))],
            out_specs=pl.BlockSpec((ch, d), lambda c, i, tt: (c, 0)),
            scratch_shapes=[pltpu.VMEM((max_tiles * MOE_ROWS, d), F32)],
        ),
        out_shape=jax.ShapeDtypeStruct((n, d), F32),
        compiler_params=_cparams(2),
        name="moe_routed",
    )(tt_flat, pos, rt, hnp, h, *w_args, lnf)


def _tile(n, pref):
    return pref if n % pref == 0 else n


def kernel(x_prompt, x_sample, cache_k_a, cache_v_a, cache_kidx_a, cache_k_b, cache_v_b, page_table,
           rel_bias_table, ln1_g, w_in, w_proj_a, w_proj_b, w_out, ln2_g, w_router_group, b_router_group,
           w_router_expert, b_router_expert, w_gate, w_up, w_down, ln_f_g):
    bsz, seq, d = x_prompt.shape
    bd, t_new, _ = x_sample.shape
    assert t_new == 1, "the sample group decodes one token per sequence"
    depth = w_in.shape[0]
    pool = cache_k_a.shape[1]
    n_pages = page_table.shape[1]
    past_len = n_pages * Q_BLOCK
    assert n_pages % SMALL_PAGES == 0 and n_pages % SB_PAGES == 0
    n_p, n_s = bsz * seq, bd * t_new
    d_expert = w_gate.shape[-1]

    xp, xs = x_prompt.reshape(n_p, d), x_sample.reshape(n_s, d)
    new_p, new_s = [], []
    for l in range(depth):
        w_pieces = _split_w_in(w_in[l])
        wpa, wpb, wout = (w.astype(BF16) for w in (w_proj_a[l], w_proj_b[l], w_out[l]))
        ln2 = ln2_g[l].reshape(1, d)
        wr = jnp.concatenate([w_router_group[l], jnp.moveaxis(w_router_expert[l], 0, 1).reshape(d, N_EXPERTS),
                              jnp.zeros((d, ROUTER_LANES - N_GROUPS - N_EXPERTS), F32)], axis=1).astype(BF16)
        br = jnp.concatenate([b_router_group[l], b_router_expert[l].reshape(N_EXPERTS),
                              jnp.zeros((ROUTER_LANES - N_GROUPS - N_EXPERTS,), F32)]).reshape(1, ROUTER_LANES)
        wgu = jnp.concatenate([w_gate[l], w_up[l]], axis=-1).reshape(N_EXPERTS, d, 2 * d_expert).astype(BF16)
        wd = w_down[l].reshape(N_EXPERTS, d_expert, d).astype(BF16)
        lnf = ln_f_g.reshape(1, d)

        def ffn(x2d, o_a, o_b, g_a, g_b, tm_merge, tm_moe):
            final_norm = l == depth - 1
            routed = x2d.shape[0] % MOE_CHUNK == 0
            h, hn, route = _merge(x2d, o_a, o_b, g_a, g_b, wpa, wpb, wout, ln2, wr, br, tm_merge, grouped=routed)
            if routed:
                return _moe_routed(hn, route, h, wgu, wd, lnf, final_norm)
            return _moe(hn, route, h, wgu, wd, lnf, tm_moe, final_norm)

        (qa, ka, va, kad, vad, qi, ki, kid, wi, qb, kb, vb, kb16, vb16, ga, gb) = _project(
            xp, ln1_g[l], w_pieces, _tile(seq, 512), seq_len=seq)
        r3 = lambda a: a.reshape(bsz, seq, a.shape[-1])
        n_top = min(TOPK_MAX, seq // 4)
        wit = jnp.swapaxes(r3(wi), 1, 2)
        vadt = jnp.swapaxes(vad.reshape(bsz, seq // Q_BLOCK, Q_BLOCK, 2 * LANES), 2, 3)
        o_at = _dsa_prompt(rel_bias_table, r3(qi), wit, r3(kid), r3(qa), r3(kad), vadt, n_top)
        o_a = jnp.swapaxes(o_at, 1, 2)
        o_b = _sb_prompt(r3(qb), r3(kb16), r3(vb16))
        xp = ffn(xp, o_a.reshape(n_p, WIDTH_A), o_b.reshape(n_p, WIDTH_B), ga, gb, _tile(n_p, 512), _tile(n_p, 1024))
        new_p.append((ka, va, ki, kb, vb))

        (qa, ka, va, _, _, qi, ki, _, wi, qb, kb, vb, _, _, ga, gb) = _project(xs, ln1_g[l], w_pieces, n_s)
        n_top = min(TOPK_MAX, (past_len + t_new) // 4)
        qi3 = qi.astype(F32).reshape(bd, N_IDX_HEADS, IDX_DIM)
        wi3 = wi.reshape(bd, N_IDX_HEADS, 1)
        paged_t = lambda c: jnp.moveaxis(c, 1, -1).reshape(pool, -1, Q_BLOCK)
        scores = _idx_sample(page_table, qi3, wi3, paged_t(cache_kidx_a[l]))
        madd, maddn = _select_sample(scores.reshape(bd, n_pages, LANES), qi3, wi3, ki.reshape(bd, 1, IDX_DIM), n_top)
        qa4 = qa.astype(F32).reshape(bd, N_HEADS_A, 1, HEAD_DIM) * (HEAD_DIM ** -0.5)
        kv_of_head = (np.arange(N_HEADS_A)[:, None] // GROUP_A) == np.arange(N_KV_A)[None, :]
        qz = jnp.where(kv_of_head[None, :, :, None], qa4, 0.0).reshape(bd, N_HEADS_A, N_KV_A * HEAD_DIM)
        logits = _dsa_logits(page_table, rel_bias_table, qz, madd.reshape(bd, n_pages, 1, LANES),
                             paged_t(cache_k_a[l]), past_len)
        o_a = _dsa_pv(page_table, rel_bias_table, logits, qz, maddn, ka.reshape(bd, 1, -1), va.reshape(bd, -1, 1),
                      paged_t(cache_v_a[l]))
        o_a = jnp.swapaxes(o_a, 1, 2)
        qb4 = qb.astype(F32).reshape(bd, N_HEADS_B, 1, HEAD_DIM) * (HEAD_DIM ** -0.5)
        own = np.eye(N_HEADS_B, dtype=bool)
        qbd = jnp.where(own[None, :, :, None], qb4, 0.0).reshape(bd, N_HEADS_B, WIDTH_B)
        o_b = _sb_sample(page_table, qbd, kb.reshape(bd, 1, -1), vb.reshape(bd, -1, 1),
                         paged_t(cache_k_b[l]), paged_t(cache_v_b[l]), past_len)
        xs = ffn(xs, o_a.reshape(n_s, WIDTH_A).astype(BF16), o_b.reshape(n_s, WIDTH_B).astype(BF16), ga, gb, n_s, n_s)
        new_s.append((ka, va, ki, kb, vb))

    outs = [xp.reshape(bsz, seq, d), xs.reshape(bd, t_new, d)]
    tails = ((N_KV_A, HEAD_DIM), (N_KV_A, HEAD_DIM), (IDX_DIM,), (N_HEADS_B, HEAD_DIM), (N_HEADS_B, HEAD_DIM))
    for idx, tail in enumerate(tails):
        outs.append(jnp.stack([jnp.moveaxis(r[idx].reshape((bsz,) + tail + (seq,)), -1, 1) for r in new_p]))
    for idx, tail in enumerate(tails):
        outs.append(jnp.stack([r[idx].reshape((bd, t_new) + tail) for r in new_s]))
    return tuple(outs)
```
